```python
import math
import jax, jax.numpy as jnp
from jax import lax
import numpy as np

D_MODEL = 1024
BATCH = 2
SEQ = 8192
DEPTH = 2
DEC_BATCH = 32
DEC_SEQ = 8
PAST_LEN = 16384
PAGE_SIZE = 128

GLA_HEADS = 4
GLA_DK = 64
GLA_DV = 128
GLA_GATE_RANK = 16
GLA_TAU = 16.0
GLA_CHUNK = 64
DSA_HEADS = 8
DSA_DH = 64
IDX_HEADS = 4
IDX_DIM = 64
DSA_TOPK = 256
Q_BLOCK = 128
IDX_W_SCALE = (IDX_HEADS ** -0.5) * (IDX_DIM ** -0.5)
POOL_WINDOWS = (2, 4, 8, 16)
POOL_GROUP = D_MODEL // 4
POOL_STATE = 16 - 1
N_EXPERTS = 32
TOP_K = 4
D_FF = D_MODEL
SWIGLU_ALPHA = 1.702
SWIGLU_LIMIT = 7.0
MOE_BLOCK = 256
DN_ALPHA = (2 * DEPTH) ** 0.25
DN_BETA = (8 * DEPTH) ** -0.25
LN_EPS = 1e-5
GLA_QK = GLA_HEADS * GLA_DK
GLA_V = GLA_HEADS * GLA_DV
DSA_W = DSA_HEADS * DSA_DH
SPLITS = (GLA_QK, GLA_QK, GLA_V, GLA_V, GLA_GATE_RANK,
          DSA_W, DSA_W, DSA_W, IDX_HEADS * IDX_DIM, IDX_DIM, IDX_HEADS)
D_IN = sum(SPLITS)
MIX_WIDTH = GLA_V + DSA_W

kernel_name = "gla_dsa_pool_moe_deepnorm_step"

F32 = jnp.float32


def layer_norm(x, g, b):
    xf = x.astype(F32)
    mu = xf.mean(-1, keepdims=True)
    var = jnp.square(xf - mu).mean(-1, keepdims=True)
    return ((xf - mu) * lax.rsqrt(var + LN_EPS) * g.astype(F32) + b.astype(F32)).astype(x.dtype)


def rms_norm(x, g):
    xf = x.astype(F32)
    return xf * lax.rsqrt(jnp.mean(xf * xf, -1, keepdims=True) + LN_EPS) * g.astype(F32)


def split_points():
    pts, acc = [], 0
    for s in SPLITS[:-1]:
        acc += s
        pts.append(acc)
    return pts


def gla_scan(q, k, v, logf, s0):
    B, T, H, _ = q.shape
    C = math.gcd(T, GLA_CHUNK)
    nc = T // C

    def to_chunks(a):
        return a.astype(F32).reshape(B, nc, C, H, a.shape[-1]).transpose(1, 0, 3, 2, 4)

    causal = jnp.tril(jnp.ones((C, C), bool))

    def step(S, inp):
        qc, kc, vc, gc = inp
        b = jnp.cumsum(gc, axis=2)
        inter = jnp.einsum('bhtd,bhdv->bhtv', qc * jnp.exp(b), S)
        diff = b[:, :, :, None, :] - b[:, :, None, :, :]
        decay = jnp.exp(jnp.where(causal[:, :, None], diff, -jnp.inf))
        att = jnp.einsum('bhtd,bhsd,bhtsd->bhts', qc, kc, decay)
        intra = jnp.einsum('bhts,bhsv->bhtv', att, vc)
        b_last = b[:, :, -1]
        S = jnp.exp(b_last)[..., None] * S + jnp.einsum(
            'bhsd,bhsv->bhdv', kc * jnp.exp(b_last[:, :, None] - b), vc)
        return S, inter + intra

    S, o = lax.scan(step, s0.astype(F32), (to_chunks(q), to_chunks(k), to_chunks(v), to_chunks(logf)))
    o = o.transpose(1, 0, 3, 2, 4).reshape(B, T, H, -1)
    return o, S


def dsa_block(q, qi, wi, q_pos, k_idx, n_sel, gather_kv):
    L = k_idx.shape[1]
    s = jnp.einsum('bqhd,bsd->bqhs', qi.astype(F32), k_idx.astype(F32))
    score = jnp.einsum('bqhs,bqh->bqs', jax.nn.relu(s), wi.astype(F32))
    admissible = jnp.arange(L)[None, :] <= q_pos[:, None]
    score = jnp.where(admissible[None], score, -jnp.inf)
    _, sel = lax.top_k(score, n_sel)
    valid = sel <= q_pos[None, :, None]
    kg, vg = gather_kv(sel)
    logits = jnp.einsum('bqhd,bqkhd->bqhk', q.astype(F32), kg) * (DSA_DH ** -0.5)
    logits = jnp.where(valid[:, :, None, :], logits, -jnp.inf)
    p = jax.nn.softmax(logits, axis=-1)
    return jnp.einsum('bqhk,bqkhd->bqhd', p, vg)


def dsa_prompt(q, qi, wi, k, v, k_idx):
    B, S = q.shape[:2]
    n_sel = min(DSA_TOPK, S // 4)
    nb = S // Q_BLOCK
    bidx = jnp.arange(B)[:, None, None]

    def gather(sel):
        return k[bidx, sel].astype(F32), v[bidx, sel].astype(F32)

    def blk(args):
        qb, qib, wib, pos = args
        return dsa_block(qb, qib, wib, pos, k_idx, n_sel, gather)

    rs = lambda a: a.reshape(B, nb, Q_BLOCK, *a.shape[2:]).swapaxes(0, 1)
    pos = jnp.arange(S).reshape(nb, Q_BLOCK)
    o = lax.map(blk, (rs(q), rs(qi), rs(wi), pos))
    return o.swapaxes(0, 1).reshape(B, S, DSA_HEADS, DSA_DH)


def dsa_sample(q, qi, wi, k_new, v_new, kidx_new, cache_k, cache_v, cache_idx_k, page_table):
    B, T = q.shape[:2]
    past = page_table.shape[1] * PAGE_SIZE
    n_sel = min(DSA_TOPK, (past + T) // 4)
    k_idx_past = cache_idx_k[page_table].reshape(B, past, IDX_DIM).astype(F32)
    k_idx = jnp.concatenate([k_idx_past, kidx_new.astype(F32)], axis=1)
    bidx = jnp.arange(B)[:, None, None]

    def gather(sel):
        in_past = (sel < past)[..., None, None]
        sp = jnp.minimum(sel, past - 1)
        phys = page_table[bidx, sp // PAGE_SIZE]
        off = sp % PAGE_SIZE
        sn = jnp.clip(sel - past, 0, T - 1)
        kg = jnp.where(in_past, cache_k[phys, off].astype(F32), k_new[bidx, sn].astype(F32))
        vg = jnp.where(in_past, cache_v[phys, off].astype(F32), v_new[bidx, sn].astype(F32))
        return kg, vg

    pos = past + jnp.arange(T)
    return dsa_block(q, qi, wi, pos, k_idx, n_sel, gather)


def ab_project(x, w_in, gla_fg_w2, gla_fg_b, idx_kn_g, idx_kn_b):
    B, T, _ = x.shape
    qa, ka, va, ga, fa, qb, kb, vb, qi, ki, wi = jnp.split(x @ w_in, split_points(), axis=-1)
    hd = lambda a, n: a.reshape(B, T, n, -1)
    logf = jax.nn.log_sigmoid((fa @ gla_fg_w2 + gla_fg_b).astype(F32)) / GLA_TAU
    gla_in = (hd(qa, GLA_HEADS) * (GLA_DK ** -0.5), hd(ka, GLA_HEADS), hd(va, GLA_HEADS),
              hd(logf, GLA_HEADS), hd(ga, GLA_HEADS))
    dsa_in = (hd(qb, DSA_HEADS), hd(qi, IDX_HEADS), wi * IDX_W_SCALE,
              hd(kb, DSA_HEADS), hd(vb, DSA_HEADS), layer_norm(ki, idx_kn_g, idx_kn_b))
    return gla_in, dsa_in


def ab_merge(o_gla, g_gla, o_dsa, gla_norm_g, w_out, dtype):
    B, T = o_gla.shape[:2]
    o_a = rms_norm(o_gla, gla_norm_g) * jax.nn.silu(g_gla.astype(F32))
    o = jnp.concatenate([o_a.reshape(B, T, GLA_V), o_dsa.reshape(B, T, DSA_W)], axis=-1)
    return o.astype(dtype) @ w_out


def pool_mix(x_ext, n_prev, pool_w, pool_scale):
    B, Lx, D = x_ext.shape
    xf = x_ext.astype(F32)
    cs = jnp.concatenate([jnp.zeros((B, 1, D), F32), jnp.cumsum(xf, axis=1)], axis=1)
    j = jnp.arange(n_prev, Lx)
    outs = []
    for g, w in enumerate(POOL_WINDOWS):
        sl = slice(g * POOL_GROUP, (g + 1) * POOL_GROUP)
        start = jnp.maximum(j - w + 1, 0)
        cnt = (j + 1 - start).astype(F32)[None, :, None]
        outs.append((cs[:, j + 1, sl] - cs[:, start, sl]) / cnt - xf[:, n_prev:, sl])
    d = jnp.stack(outs, axis=2)
    y = jnp.einsum('btgc,gce->btge', d, pool_w.astype(F32)).reshape(B, Lx - n_prev, D)
    return (y * pool_scale.astype(F32)).astype(x_ext.dtype)


def moe(x, w_router, b_router, w1, b1, w2, b2):
    shp = x.shape
    xt = x.reshape(-1, D_MODEL)
    N = xt.shape[0]
    logits = (xt @ w_router + b_router).astype(F32)
    top_val, top_idx = lax.top_k(logits, TOP_K)
    gate = jax.nn.softmax(top_val, axis=-1)
    A = N * TOP_K
    M = max(8, min(MOE_BLOCK, A // N_EXPERTS))
    n_blocks = -(-A // M) + N_EXPERTS
    P = n_blocks * M
    flat_e = top_idx.reshape(-1)
    order = jnp.argsort(flat_e, stable=True)
    se = flat_e[order]
    counts = jnp.bincount(flat_e, length=N_EXPERTS)
    padded = (counts + M - 1) // M * M
    pad_end = jnp.cumsum(padded)
    pad_start = pad_end - padded
    start = jnp.cumsum(counts) - counts
    dest = pad_start[se] + jnp.arange(A) - start[se]
    row_tok = jnp.full((P,), N, jnp.int32).at[dest].set((order // TOP_K).astype(jnp.int32))
    row_w = jnp.zeros((P,), F32).at[dest].set(gate.reshape(-1)[order])
    block_e = jnp.minimum(jnp.searchsorted(pad_end, jnp.arange(n_blocks) * M, side='right'),
                          N_EXPERTS - 1)
    xp = jnp.concatenate([xt, jnp.zeros((1, D_MODEL), xt.dtype)], axis=0)
    xb = xp[row_tok].reshape(n_blocks, M, D_MODEL)

    def expert_block(args):
        xblk, e = args
        h = (xblk @ w1[e] + b1[e]).astype(F32)
        gt = jnp.minimum(h[:, :D_FF], SWIGLU_LIMIT)
        up = jnp.clip(h[:, D_FF:], -SWIGLU_LIMIT, SWIGLU_LIMIT)
        act = (up + 1.0) * gt * jax.nn.sigmoid(SWIGLU_ALPHA * gt)
        return act.astype(xblk.dtype) @ w2[e] + b2[e]

    yb = lax.map(expert_block, (xb, block_e)).reshape(P, D_MODEL)
    y = jnp.zeros((N + 1, D_MODEL), F32).at[row_tok].add(yb.astype(F32) * row_w[:, None])[:N]
    return y.astype(x.dtype).reshape(shp)


def setup_inputs(seed: int = 0) -> dict:
    key = jax.random.key(seed)
    ks = iter(jax.random.split(key, 40))
    nrm = lambda shape, scale: jax.random.normal(next(ks), shape, F32) * scale
    n_pages = PAST_LEN // PAGE_SIZE
    n_used = DEC_BATCH * n_pages
    n_pool = n_used + (n_used + 3) // 4
    perm = jax.random.permutation(next(ks), n_pool)
    page_table = perm[:n_used].reshape(DEC_BATCH, n_pages).astype(jnp.int32)
    return {
        "x_prompt": nrm((BATCH, SEQ, D_MODEL), 1.0),
        "x_sample": nrm((DEC_BATCH, DEC_SEQ, D_MODEL), 1.0),
        "cache_k": nrm((n_pool, PAGE_SIZE, DSA_HEADS, DSA_DH), 1.0),
        "cache_v": nrm((n_pool, PAGE_SIZE, DSA_HEADS, DSA_DH), 1.0),
        "cache_idx_k": nrm((n_pool, PAGE_SIZE, IDX_DIM), 1.0),
        "page_table": page_table,
        "state_gla": nrm((DEC_BATCH, GLA_HEADS, GLA_DK, GLA_DV), 2.0),
        "state_pool": nrm((DEC_BATCH, POOL_STATE, D_MODEL), 1.0),
        "w_in": nrm((D_MODEL, D_IN), D_MODEL ** -0.5),
        "gla_fg_w2": nrm((GLA_GATE_RANK, GLA_QK), GLA_GATE_RANK ** -0.5),
        "gla_fg_b": nrm((GLA_QK,), 0.1),
        "gla_norm_g": 1.0 + nrm((GLA_DV,), 0.02),
        "idx_kn_g": 1.0 + nrm((IDX_DIM,), 0.02),
        "idx_kn_b": nrm((IDX_DIM,), 0.02),
        "w_out": nrm((MIX_WIDTH, D_MODEL), MIX_WIDTH ** -0.5 * DN_BETA),
        "pool_w": nrm((4, POOL_GROUP, POOL_GROUP), POOL_GROUP ** -0.5 * DN_BETA),
        "pool_scale": 1.0 + nrm((D_MODEL,), 0.02),
        "ln_mix_g": 1.0 + nrm((DEPTH, D_MODEL), 0.02),
        "ln_mix_b": nrm((DEPTH, D_MODEL), 0.02),
        "ln_ffn_g": 1.0 + nrm((DEPTH, D_MODEL), 0.02),
        "ln_ffn_b": nrm((DEPTH, D_MODEL), 0.02),
        "moe_router_w": nrm((DEPTH, D_MODEL, N_EXPERTS), D_MODEL ** -0.5),
        "moe_router_b": nrm((DEPTH, N_EXPERTS), 0.01),
        "moe_w1": nrm((DEPTH, N_EXPERTS, D_MODEL, 2 * D_FF), D_MODEL ** -0.5),
        "moe_b1": nrm((DEPTH, N_EXPERTS, 2 * D_FF), 0.02),
        "moe_w2": nrm((DEPTH, N_EXPERTS, D_FF, D_MODEL), D_FF ** -0.5 * DN_BETA),
        "moe_b2": nrm((DEPTH, N_EXPERTS, D_MODEL), 0.02),
    }


def reference(x_prompt, x_sample, cache_k, cache_v, cache_idx_k, page_table, state_gla, state_pool,
              w_in, gla_fg_w2, gla_fg_b, gla_norm_g, idx_kn_g, idx_kn_b, w_out,
              pool_w, pool_scale, ln_mix_g, ln_mix_b, ln_ffn_g, ln_ffn_b,
              moe_router_w, moe_router_b, moe_w1, moe_b1, moe_w2, moe_b2):
    xp, xs = x_prompt, x_sample
    ab_w = (w_in, gla_fg_w2, gla_fg_b, idx_kn_g, idx_kn_b)
    for layer in range(DEPTH):
        if layer % 2 == 0:
            (qa, ka, va, lf, ga), (qb, qi, wi, kb, vb, ki) = ab_project(xp, *ab_w)
            s0 = jnp.zeros((xp.shape[0], GLA_HEADS, GLA_DK, GLA_DV), F32)
            oa, gla_p = gla_scan(qa, ka, va, lf, s0)
            ob = dsa_prompt(qb, qi, wi, kb, vb, ki)
            mix_p = ab_merge(oa, ga, ob, gla_norm_g, w_out, xp.dtype)
            k_p, v_p, kidx_p = kb, vb, ki
            (qa, ka, va, lf, ga), (qb, qi, wi, kb, vb, ki) = ab_project(xs, *ab_w)
            oa, gla_s = gla_scan(qa, ka, va, lf, state_gla)
            ob = dsa_sample(qb, qi, wi, kb, vb, ki, cache_k, cache_v, cache_idx_k, page_table)
            mix_s = ab_merge(oa, ga, ob, gla_norm_g, w_out, xs.dtype)
            k_s, v_s, kidx_s = kb, vb, ki
        else:
            mix_p = pool_mix(xp, 0, pool_w, pool_scale)
            pool_p = xp[:, -POOL_STATE:]
            xs_ext = jnp.concatenate([state_pool.astype(xs.dtype), xs], axis=1)
            mix_s = pool_mix(xs_ext, POOL_STATE, pool_w, pool_scale)
            pool_s = xs_ext[:, -POOL_STATE:]
        xp = layer_norm(DN_ALPHA * xp + mix_p, ln_mix_g[layer], ln_mix_b[layer])
        xs = layer_norm(DN_ALPHA * xs + mix_s, ln_mix_g[layer], ln_mix_b[layer])
        moe_w = (moe_router_w[layer], moe_router_b[layer], moe_w1[layer], moe_b1[layer],
                 moe_w2[layer], moe_b2[layer])
        xp = layer_norm(DN_ALPHA * xp + moe(xp, *moe_w), ln_ffn_g[layer], ln_ffn_b[layer])
        xs = layer_norm(DN_ALPHA * xs + moe(xs, *moe_w), ln_ffn_g[layer], ln_ffn_b[layer])
    return (xp, xs, k_p, v_p, kidx_p, gla_p, pool_p, k_s, v_s, kidx_s, gla_s, pool_s)
```

```python
import functools
import math

import jax
import jax.numpy as jnp
from jax import lax
from jax.experimental import pallas as pl
from jax.experimental.pallas import tpu as pltpu

F32 = jnp.float32
BF16 = jnp.bfloat16
I32 = jnp.int32

D_MODEL = 1024
DEPTH = 2
PAGE_SIZE = 128
GLA_HEADS = 4
GLA_DK = 64
GLA_DV = 128
GLA_GATE_RANK = 16
GLA_TAU = 16.0
GLA_CHUNK = 64
DSA_HEADS = 8
DSA_DH = 64
IDX_HEADS = 4
IDX_DIM = 64
DSA_TOPK = 256
IDX_W_SCALE = (IDX_HEADS ** -0.5) * (IDX_DIM ** -0.5)
POOL_WINDOWS = (2, 4, 8, 16)
POOL_GROUP = D_MODEL // 4
POOL_STATE = 16 - 1
N_EXPERTS = 32
TOP_K = 4
D_FF = D_MODEL
SWIGLU_ALPHA = 1.702
SWIGLU_LIMIT = 7.0
DN_ALPHA = (2 * DEPTH) ** 0.25
LN_EPS = 1e-5
GLA_QK = GLA_HEADS * GLA_DK
GLA_V = GLA_HEADS * GLA_DV
DSA_W = DSA_HEADS * DSA_DH
IDX_W = IDX_HEADS * IDX_DIM

LANES = 128
SUBLANES = 8
VMEM_LIMIT_BYTES = 56 * 1024 * 1024

NEG_BIG = -1e30


def _params(sem, vmem=VMEM_LIMIT_BYTES):
    return pltpu.CompilerParams(dimension_semantics=sem, vmem_limit_bytes=vmem)


def _layer_norm(x, g, b):
    mu = jnp.mean(x, axis=-1, keepdims=True)
    xc = x - mu
    var = jnp.mean(xc * xc, axis=-1, keepdims=True)
    return xc * lax.rsqrt(var + LN_EPS) * g + b


def _row_spec(tm, w):
    return pl.BlockSpec((tm, w), lambda i: (i, 0))


def _full_spec(a):
    nd = a.ndim
    return pl.BlockSpec(a.shape, lambda i: (0,) * nd)


_MAIN_W = (GLA_QK, GLA_QK, GLA_V, GLA_V, DSA_W, DSA_W, DSA_W, IDX_W)
_MAIN_OFF = tuple(sum(_MAIN_W[:i]) for i in range(len(_MAIN_W) + 1))


def _proj_kernel(x_ref, wm_ref, ws_ref, fgw_ref, fgb_ref, kng_ref, knb_ref,
                 qa_ref, ka_ref, va_ref, ga_ref, lf_ref, qb_ref, kb_ref, vb_ref,
                 qi_ref, ki_ref, wi_ref, kb16_ref, vb16_ref):
    xb = x_ref[...].astype(BF16)

    def mm(n):
        return jnp.dot(xb, wm_ref[:, _MAIN_OFF[n]:_MAIN_OFF[n + 1]], preferred_element_type=F32)

    qa_ref[...] = mm(0) * (GLA_DK ** -0.5)
    ka_ref[...] = mm(1)
    va_ref[...] = mm(2)
    ga_ref[...] = mm(3)
    qb_ref[...] = mm(4)
    kb = mm(5)
    kb_ref[...] = kb
    kb16_ref[...] = kb.astype(BF16)
    vb = mm(6)
    vb_ref[...] = vb
    vb16_ref[...] = vb.astype(BF16)
    qi_ref[...] = mm(7)

    small = jnp.dot(xb, ws_ref[...], preferred_element_type=F32)
    ki_ref[...] = _layer_norm(small[:, :IDX_DIM], kng_ref[...], knb_ref[...])
    fa = small[:, IDX_DIM:IDX_DIM + GLA_GATE_RANK]
    z = jnp.dot(fa, fgw_ref[...], preferred_element_type=F32,
                precision=lax.Precision.HIGHEST) + fgb_ref[...]
    lf_ref[...] = (jnp.minimum(z, 0.0) - jnp.log(1.0 + jnp.exp(-jnp.abs(z)))) * (1.0 / GLA_TAU)
    wi_ref[...] = small * IDX_W_SCALE


def _project(x2d, w_in, gla_fg_w2, gla_fg_b, idx_kn_g, idx_kn_b, tm):
    n = x2d.shape[0]
    pts = [0]
    for s in (GLA_QK, GLA_QK, GLA_V, GLA_V, GLA_GATE_RANK, DSA_W, DSA_W, DSA_W, IDX_W, IDX_DIM, IDX_HEADS):
        pts.append(pts[-1] + s)
    seg = lambda i: w_in[:, pts[i]:pts[i + 1]]
    wm = jnp.concatenate([seg(0), seg(1), seg(2), seg(3), seg(5), seg(6), seg(7), seg(8)], axis=1).astype(BF16)
    pad = LANES - IDX_DIM - GLA_GATE_RANK - IDX_HEADS
    ws = jnp.concatenate([seg(9), seg(4), seg(10), jnp.zeros((D_MODEL, pad), w_in.dtype)], axis=1).astype(BF16)
    fgb = gla_fg_b.reshape(1, GLA_QK)
    kng = idx_kn_g.reshape(1, IDX_DIM)
    knb = idx_kn_b.reshape(1, IDX_DIM)
    widths = (GLA_QK, GLA_QK, GLA_V, GLA_V, GLA_QK, DSA_W, DSA_W, DSA_W, IDX_W, IDX_DIM, LANES)
    out_shape = [jax.ShapeDtypeStruct((n, w), F32) for w in widths]
    out_shape += [jax.ShapeDtypeStruct((n, DSA_W), BF16)] * 2
    out_specs = [_row_spec(tm, w) for w in widths] + [_row_spec(tm, DSA_W)] * 2
    return pl.pallas_call(
        _proj_kernel,
        grid=(n // tm,),
        in_specs=[_row_spec(tm, D_MODEL), _full_spec(wm), _full_spec(ws), _full_spec(gla_fg_w2),
                  _full_spec(fgb), _full_spec(kng), _full_spec(knb)],
        out_specs=out_specs,
        out_shape=out_shape,
        compiler_params=_params(("parallel",)),
        name="proj",
    )(x2d, wm, ws, gla_fg_w2, fgb, kng, knb)


def _gla_kernel(q_ref, k_ref, v_ref, lf_ref, ga_ref, s0_ref, ng_ref, o_ref, sout_ref, s_scr, *, chunk):
    c = pl.program_id(1)
    nc = pl.num_programs(1)

    @pl.when(c == 0)
    def _():
        s_scr[...] = s0_ref[0]

    g = lf_ref[...]
    ri = lax.broadcasted_iota(I32, (chunk, chunk), 0)
    ci = lax.broadcasted_iota(I32, (chunk, chunk), 1)
    causal = ci <= ri
    tri = jnp.where(causal, 1.0, 0.0).astype(F32)
    b = jnp.dot(tri, g, preferred_element_type=F32, precision=lax.Precision.HIGHEST)
    mid = chunk // 2
    b_mid = b[mid:mid + 1, :]
    b_last = b[chunk - 1:chunk, :]
    q = q_ref[...]
    k = k_ref[...]
    q_in = q * jnp.exp(b)
    q_rel = q * jnp.exp(b - b_mid)
    k_rel = k * jnp.exp(b_mid - b)
    k_out = k * jnp.exp(b_last - b)
    dec_last = jnp.exp(b_last)
    v = v_ref[...]
    ga = ga_ref[...]
    ng = ng_ref[...]
    eye = jnp.where(lax.broadcasted_iota(I32, (GLA_DK, GLA_DK), 0)
                    == lax.broadcasted_iota(I32, (GLA_DK, GLA_DK), 1), 1.0, 0.0).astype(F32)
    for h in range(GLA_HEADS):
        ks = slice(h * GLA_DK, (h + 1) * GLA_DK)
        vs = slice(h * GLA_DV, (h + 1) * GLA_DV)
        s_h = s_scr[h]
        vh = v[:, vs]
        inter = jnp.dot(q_in[:, ks], s_h, preferred_element_type=F32)
        att = lax.dot_general(q_rel[:, ks], k_rel[:, ks], (((1,), (1,)), ((), ())),
                              preferred_element_type=F32)
        att = jnp.where(causal, att, 0.0)
        o = inter + jnp.dot(att, vh, preferred_element_type=F32)
        kv = lax.dot_general(k_out[:, ks], vh, (((0,), (0,)), ((), ())), preferred_element_type=F32)
        s_scr[h] = jnp.dot(eye * dec_last[:, ks], s_h, preferred_element_type=F32,
                           precision=lax.Precision.HIGHEST) + kv
        ms = jnp.mean(o * o, axis=-1, keepdims=True)
        gh = ga[:, vs]
        o_ref[:, vs] = o * lax.rsqrt(ms + LN_EPS) * ng * (gh * jax.nn.sigmoid(gh))

    @pl.when(c == nc - 1)
    def _():
        sout_ref[0] = s_scr[...]


def _gla(qa, ka, va, lf, ga, s0, gla_norm_g, batch, seq):
    chunk = math.gcd(seq, GLA_CHUNK)
    nc = seq // chunk
    ng = gla_norm_g.reshape(1, GLA_DV)
    spec = lambda w: pl.BlockSpec((chunk, w), lambda b, c: (b * nc + c, 0))
    sspec = pl.BlockSpec((1, GLA_HEADS, GLA_DK, GLA_DV), lambda b, c: (b, 0, 0, 0))
    return pl.pallas_call(
        functools.partial(_gla_kernel, chunk=chunk),
        grid=(batch, nc),
        in_specs=[spec(GLA_QK), spec(GLA_QK), spec(GLA_V), spec(GLA_QK), spec(GLA_V), sspec,
                  pl.BlockSpec((1, GLA_DV), lambda b, c: (0, 0))],
        out_specs=[spec(GLA_V), sspec],
        out_shape=[jax.ShapeDtypeStruct((batch * seq, GLA_V), F32),
                   jax.ShapeDtypeStruct((batch, GLA_HEADS, GLA_DK, GLA_DV), F32)],
        scratch_shapes=[pltpu.VMEM((GLA_HEADS, GLA_DK, GLA_DV), F32)],
        compiler_params=_params(("parallel", "arbitrary")),
        name="gla",
    )(qa, ka, va, lf, ga, s0, ng)


def _merge_kernel(x_ref, oa_ref, ob_ref, w_ref, g_ref, b_ref, o_ref):
    mix = jnp.dot(oa_ref[...].astype(BF16), w_ref[:GLA_V, :], preferred_element_type=F32)
    mix = mix + jnp.dot(ob_ref[...].astype(BF16), w_ref[GLA_V:, :], preferred_element_type=F32)
    o_ref[...] = _layer_norm(DN_ALPHA * x_ref[...] + mix, g_ref[...], b_ref[...])


def _merge(x2d, oa, ob, w_out, g, b, tm):
    n = x2d.shape[0]
    w16 = w_out.astype(BF16)
    g2, b2 = g.reshape(1, D_MODEL), b.reshape(1, D_MODEL)
    return pl.pallas_call(
        _merge_kernel,
        grid=(n // tm,),
        in_specs=[_row_spec(tm, D_MODEL), _row_spec(tm, GLA_V), _row_spec(tm, DSA_W),
                  _full_spec(w16), _full_spec(g2), _full_spec(b2)],
        out_specs=_row_spec(tm, D_MODEL),
        out_shape=jax.ShapeDtypeStruct((n, D_MODEL), F32),
        compiler_params=_params(("parallel",)),
        name="merge",
    )(x2d, oa, ob, w16, g2, b2)


_KEY_NEG_INF = -2139095041
_KEY_POS_INF = 2139095040
_WI_LANE = IDX_DIM + GLA_GATE_RANK


def _unkey(kk):
    return lax.bitcast_convert_type(jnp.where(kk < 0, kk ^ 0x7FFFFFFF, kk), F32)


def _fold_lanes(m):
    out = m[:, :LANES]
    for t in range(1, m.shape[1] // LANES):
        out = out + m[:, t * LANES:(t + 1) * LANES]
    return out


def _kth_largest_threshold(read_block, n_blocks, rows, k, active):
    kf = float(k)

    def count(pred):
        def body(c, acc):
            return acc + _fold_lanes(jnp.where(pred(read_block(c)), 1.0, 0.0))
        acc = lax.fori_loop(0, n_blocks, body, jnp.zeros((rows, LANES), F32))
        return jnp.sum(acc, axis=-1, keepdims=True)

    def cond(st):
        return st[4] > 0

    def body(st):
        it, lo, hi, cnt_lo, _ = st
        mid = (lo >> 1) + (hi >> 1) + (lo & hi & 1)
        cand = _unkey(mid)
        cnt = count(lambda s: s >= cand)
        ge = cnt >= kf
        lo = jnp.where(ge, mid, lo)
        cnt_lo = jnp.where(ge, cnt, cnt_lo)
        hi = jnp.where(ge, hi, mid)
        done = (cnt_lo == kf) | (hi == lo + 1) | jnp.logical_not(active)
        go = jnp.max(jnp.where(done, 0, 1)).astype(I32)
        go = jnp.where(it < 40, go, 0)
        return it + 1, lo, hi, cnt_lo, go

    lo0 = jnp.full((rows, 1), _KEY_NEG_INF, I32)
    hi0 = jnp.full((rows, 1), _KEY_POS_INF, I32)
    go0 = jnp.max(jnp.where(active, 1, 0)).astype(I32)
    _, lo, _, _, _ = lax.while_loop(cond, body, (jnp.int32(0), lo0, hi0, jnp.full((rows, 1), -1.0, F32), go0))
    thr = jnp.where(active, _unkey(lo), -jnp.inf)
    n_gt = count(lambda s: s > thr)
    n_ge = count(lambda s: s >= thr)
    need = jnp.where(active, kf - n_gt, 0.0)
    ties = jnp.logical_and(active, (n_ge - n_gt) > need)
    tie_flag = jnp.max(jnp.where(ties, 1, 0)).astype(I32)
    return thr, need, tie_flag


def _selection_bias(s, thr, need, tie_flag, tie_cnt_ref, bias_ref):
    w = s.shape[1]

    @pl.when(tie_flag == 0)
    def _():
        sel = jnp.logical_and(s >= thr, s > -jnp.inf)
        bias_ref[...] = jnp.where(sel, 0.0, NEG_BIG)

    @pl.when(tie_flag != 0)
    def _():
        eq = s == thr
        eqf = jnp.where(eq, 1.0, 0.0)
        before = (lax.broadcasted_iota(I32, (w, w), 0) < lax.broadcasted_iota(I32, (w, w), 1))
        upper = jnp.where(before, 1.0, 0.0).astype(BF16)
        rank = tie_cnt_ref[...] + jnp.dot(eqf.astype(BF16), upper, preferred_element_type=F32)
        sel = jnp.logical_or(s > thr, jnp.logical_and(eq, rank < need))
        bias_ref[...] = jnp.where(sel, 0.0, NEG_BIG)
        tie_cnt_ref[...] = tie_cnt_ref[...] + jnp.sum(eqf, axis=-1, keepdims=True)


def _dsa_prompt_kernel(qb_ref, qi_ref, wi_ref, kit_ref, k_hbm, v_hbm, o_ref,
                       k_scr, v_scr, sc_scr, qm_scr, m_scr, l_scr, acc_scr, bias_scr, tie_scr, sem,
                       *, q_blk, k_blk, seq, n_sel):
    b = pl.program_id(0)
    i = pl.program_id(1)

    @pl.when(i == 0)
    def _():
        ck = pltpu.make_async_copy(k_hbm.at[pl.ds(b * seq, seq)], k_scr, sem.at[0])
        cv = pltpu.make_async_copy(v_hbm.at[pl.ds(b * seq, seq)], v_scr, sem.at[1])
        ck.start()
        cv.start()
        ck.wait()
        cv.wait()

    q0 = i * q_blk
    n_blocks = (q0 + q_blk + k_blk - 1) // k_blk
    row_g = q0 + lax.broadcasted_iota(I32, (q_blk, 1), 0)

    qi16 = qi_ref[...].astype(BF16)
    qi_h = [qi16[:, h * IDX_DIM:(h + 1) * IDX_DIM] for h in range(IDX_HEADS)]
    wi = wi_ref[...]
    w_h = [wi[:, _WI_LANE + h:_WI_LANE + h + 1] for h in range(IDX_HEADS)]

    def score_body(c, carry):
        off = pl.multiple_of(c * k_blk, k_blk)
        kc = kit_ref[0, :, pl.ds(off, k_blk)].astype(BF16)
        s = w_h[0] * jnp.maximum(jnp.dot(qi_h[0], kc, preferred_element_type=F32), 0.0)
        for h in range(1, IDX_HEADS):
            s = s + w_h[h] * jnp.maximum(jnp.dot(qi_h[h], kc, preferred_element_type=F32), 0.0)
        col = off + lax.broadcasted_iota(I32, (1, k_blk), 1)
        sc_scr[:, pl.ds(off, k_blk)] = jnp.where(col <= row_g, s, -jnp.inf)
        return carry

    lax.fori_loop(0, n_blocks, score_body, 0)

    def read_block(c):
        return sc_scr[:, pl.ds(pl.multiple_of(c * k_blk, k_blk), k_blk)]

    active = (row_g + 1) > n_sel
    thr, need, tie_flag = _kth_largest_threshold(read_block, n_blocks, q_blk, n_sel, active)

    lane_lo = lax.broadcasted_iota(I32, (1, LANES), 1) < DSA_DH
    qs = qb_ref[...] * (DSA_DH ** -0.5)
    for p in range(DSA_HEADS // 2):
        qp = qs[:, p * LANES:(p + 1) * LANES]
        qm_scr[2 * p] = jnp.where(lane_lo, qp, 0.0).astype(BF16)
        qm_scr[2 * p + 1] = jnp.where(lane_lo, 0.0, qp).astype(BF16)
    m_scr[...] = jnp.full(m_scr.shape, NEG_BIG, F32)
    l_scr[...] = jnp.zeros(l_scr.shape, F32)
    acc_scr[...] = jnp.zeros(acc_scr.shape, F32)
    tie_scr[...] = jnp.zeros(tie_scr.shape, F32)

    def att_body(j, carry):
        off = pl.multiple_of(j * k_blk, k_blk)
        _selection_bias(sc_scr[:, pl.ds(off, k_blk)], thr, need, tie_flag, tie_scr, bias_scr)
        bias = bias_scr[...]
        for p in range(DSA_HEADS // 2):
            kp = k_scr[pl.ds(off, k_blk), p * LANES:(p + 1) * LANES]
            vp = v_scr[pl.ds(off, k_blk), p * LANES:(p + 1) * LANES]
            pv, al = [], []
            for half in range(2):
                h = 2 * p + half
                logit = lax.dot_general(qm_scr[h], kp, (((1,), (1,)), ((), ())),
                                        preferred_element_type=F32) + bias
                m_old = m_scr[h]
                m_new = jnp.maximum(m_old, jnp.max(logit, axis=-1, keepdims=True))
                alpha = jnp.exp(m_old - m_new)
                pr = jnp.exp(logit - m_new)
                l_scr[h] = alpha * l_scr[h] + jnp.sum(pr, axis=-1, keepdims=True)
                m_scr[h] = m_new
                pv.append(jnp.dot(pr.astype(BF16), vp, preferred_element_type=F32))
                al.append(alpha)
            cols = slice(p * LANES, (p + 1) * LANES)
            acc_scr[:, cols] = (acc_scr[:, cols] * jnp.where(lane_lo, al[0], al[1])
                                + jnp.where(lane_lo, pv[0], pv[1]))
        return carry

    lax.fori_loop(0, n_blocks, att_body, 0)

    for p in range(DSA_HEADS // 2):
        cols = slice(p * LANES, (p + 1) * LANES)
        o_ref[:, cols] = acc_scr[:, cols] / jnp.where(lane_lo, l_scr[2 * p], l_scr[2 * p + 1])


def _dsa_prompt(qb, qi, wi, kit, k16, v16, batch, seq, q_blk, k_blk):
    n_sel = min(DSA_TOPK, seq // 4)
    nq = seq // q_blk
    qspec = lambda w: pl.BlockSpec((q_blk, w), lambda b, i: (b * nq + i, 0))
    return pl.pallas_call(
        functools.partial(_dsa_prompt_kernel, q_blk=q_blk, k_blk=k_blk, seq=seq, n_sel=n_sel),
        grid=(batch, nq),
        in_specs=[qspec(DSA_W), qspec(IDX_W), qspec(LANES),
                  pl.BlockSpec((1, IDX_DIM, seq), lambda b, i: (b, 0, 0)),
                  pl.BlockSpec(memory_space=pl.ANY), pl.BlockSpec(memory_space=pl.ANY)],
        out_specs=qspec(DSA_W),
        out_shape=jax.ShapeDtypeStruct((batch * seq, DSA_W), F32),
        scratch_shapes=[pltpu.VMEM((seq, DSA_W), BF16), pltpu.VMEM((seq, DSA_W), BF16),
                        pltpu.VMEM((q_blk, seq), F32),
                        pltpu.VMEM((DSA_HEADS, q_blk, LANES), BF16),
                        pltpu.VMEM((DSA_HEADS, q_blk, 1), F32), pltpu.VMEM((DSA_HEADS, q_blk, 1), F32),
                        pltpu.VMEM((q_blk, DSA_W), F32), pltpu.VMEM((q_blk, k_blk), F32),
                        pltpu.VMEM((q_blk, 1), F32), pltpu.SemaphoreType.DMA((2,))],
        compiler_params=_params(("arbitrary", "arbitrary")),
        name="dsa_prompt",
    )(qb, qi, wi, kit, k16, v16)


PAGES_PER_STEP = 8
_STEP_KEYS = PAGES_PER_STEP * PAGE_SIZE


def _page_specs(width, n_pages, clamp_last):
    specs = []
    for s in range(PAGES_PER_STEP):
        def imap(b, c, pt, s=s):
            page = jnp.minimum(c * PAGES_PER_STEP + s, n_pages - PAGES_PER_STEP + s) if clamp_last \
                else c * PAGES_PER_STEP + s
            return (pt[b, page], 0, 0)
        specs.append(pl.BlockSpec((1, PAGE_SIZE, width), imap))
    return specs


def _stack_heads(qi):
    return jnp.concatenate([qi[:, h * IDX_DIM:(h + 1) * IDX_DIM] for h in range(IDX_HEADS)], axis=0)


def _idx_score(qi4, wi, keys, t):
    s4 = lax.dot_general(qi4, keys, (((1,), (1,)), ((), ())), preferred_element_type=F32)
    s = wi[:, _WI_LANE:_WI_LANE + 1] * jnp.maximum(s4[:t], 0.0)
    for h in range(1, IDX_HEADS):
        s = s + wi[:, _WI_LANE + h:_WI_LANE + h + 1] * jnp.maximum(s4[h * t:(h + 1) * t], 0.0)
    return s


def _dsa_sample_score_kernel(pt_ref, qi_ref, wi_ref, kin_ref, *rest, t, past, n_sel):
    pages = rest[:PAGES_PER_STEP]
    sc_ref, thr_ref, need_ref, flag_ref = rest[PAGES_PER_STEP:]
    c = pl.program_id(1)
    nch = pl.num_programs(1)
    qi4 = _stack_heads(qi_ref[...]).astype(BF16)
    wi = wi_ref[...]
    keys = jnp.concatenate([p[0] for p in pages], axis=0).astype(BF16)
    off = pl.multiple_of(c * _STEP_KEYS, _STEP_KEYS)
    sc_ref[0, :, pl.ds(off, _STEP_KEYS)] = _idx_score(qi4, wi, keys, t)

    @pl.when(c == nch - 1)
    def _():
        knew = jnp.concatenate([kin_ref[...], jnp.zeros((LANES - t, IDX_DIM), F32)], axis=0).astype(BF16)
        s = _idx_score(qi4, wi, knew, t)
        col = lax.broadcasted_iota(I32, (t, LANES), 1)
        row = lax.broadcasted_iota(I32, (t, LANES), 0)
        sc_ref[0, :, past:past + LANES] = jnp.where(col <= row, s, -jnp.inf)
        active = (past + 1 + lax.broadcasted_iota(I32, (t, 1), 0)) > n_sel
        thr, need, flag = _kth_largest_threshold(lambda _: sc_ref[0], 1, t, n_sel, active)
        thr_ref[0] = thr
        need_ref[0] = need
        flag_ref[0] = jnp.zeros((t, 1), F32) + flag.astype(F32)


def _dsa_sample_attn_kernel(pt_ref, qb_ref, sc_ref, thr_ref, need_ref, flag_ref, kn_ref, vn_ref, *rest,
                            t, past):
    kpages = rest[:PAGES_PER_STEP]
    vpages = rest[PAGES_PER_STEP:2 * PAGES_PER_STEP]
    o_ref, m_scr, l_scr, acc_scr, bias_scr, biasn_scr, tie_scr = rest[2 * PAGES_PER_STEP:]
    c = pl.program_id(1)
    nch = pl.num_programs(1) - 1
    rows = DSA_HEADS * t

    @pl.when(c == 0)
    def _():
        m_scr[...] = jnp.full(m_scr.shape, NEG_BIG, F32)
        l_scr[...] = jnp.zeros(l_scr.shape, F32)
        acc_scr[...] = jnp.zeros(acc_scr.shape, F32)
        tie_scr[...] = jnp.zeros(tie_scr.shape, F32)

    blockmask = (lax.broadcasted_iota(I32, (rows, DSA_W), 0) // t
                 == lax.broadcasted_iota(I32, (rows, DSA_W), 1) // DSA_DH)
    qs = qb_ref[...] * (DSA_DH ** -0.5)
    qbd = jnp.where(blockmask, jnp.concatenate([qs] * DSA_HEADS, axis=0), 0.0).astype(BF16)
    thr = thr_ref[0]
    need = need_ref[0]
    flag = jnp.max(flag_ref[0]).astype(I32)

    def update(bias, kc, vc):
        logit = lax.dot_general(qbd, kc, (((1,), (1,)), ((), ())), preferred_element_type=F32)
        logit = logit + jnp.concatenate([bias] * DSA_HEADS, axis=0)
        m_old = m_scr[...]
        m_new = jnp.maximum(m_old, jnp.max(logit, axis=-1, keepdims=True))
        alpha = jnp.exp(m_old - m_new)
        pr = jnp.exp(logit - m_new)
        l_scr[...] = alpha * l_scr[...] + jnp.sum(pr, axis=-1, keepdims=True)
        m_scr[...] = m_new
        acc_scr[...] = alpha * acc_scr[...] + jnp.dot(pr.astype(BF16), vc, preferred_element_type=F32)

    @pl.when(c < nch)
    def _():
        off = pl.multiple_of(c * _STEP_KEYS, _STEP_KEYS)
        _selection_bias(sc_ref[0, :, pl.ds(off, _STEP_KEYS)], thr, need, flag, tie_scr, bias_scr)
        kc = jnp.concatenate([p[0] for p in kpages], axis=0).astype(BF16)
        vc = jnp.concatenate([p[0] for p in vpages], axis=0).astype(BF16)
        update(bias_scr[...], kc, vc)

    @pl.when(c == nch)
    def _():
        _selection_bias(sc_ref[0, :, past:past + LANES], thr, need, flag, tie_scr, biasn_scr)
        zpad = jnp.zeros((LANES - t, DSA_W), F32)
        kc = jnp.concatenate([kn_ref[...], zpad], axis=0).astype(BF16)
        vc = jnp.concatenate([vn_ref[...], zpad], axis=0).astype(BF16)
        update(biasn_scr[...], kc, vc)
        outn = jnp.where(blockmask, acc_scr[...] / l_scr[...], 0.0)
        out = outn[:t]
        for h in range(1, DSA_HEADS):
            out = out + outn[h * t:(h + 1) * t]
        o_ref[...] = out


def _dsa_sample(qb, qi, wi, ki_new, k_new, v_new, cache_k, cache_v, cache_idx_k, page_table, batch, t):
    n_pool = cache_k.shape[0]
    n_pages = page_table.shape[1]
    past = n_pages * PAGE_SIZE
    n_sel = min(DSA_TOPK, (past + t) // 4)
    nch = n_pages // PAGES_PER_STEP
    lp = past + LANES
    ck = cache_k.reshape(n_pool, PAGE_SIZE, DSA_W)
    cv = cache_v.reshape(n_pool, PAGE_SIZE, DSA_W)
    rspec = lambda w: pl.BlockSpec((t, w), lambda b, c, pt: (b, 0))
    bspec = lambda w: pl.BlockSpec((1, t, w), lambda b, c, pt: (b, 0, 0))
    scores, thr, need, flag = pl.pallas_call(
        functools.partial(_dsa_sample_score_kernel, t=t, past=past, n_sel=n_sel),
        grid_spec=pltpu.PrefetchScalarGridSpec(
            num_scalar_prefetch=1,
            grid=(batch, nch),
            in_specs=[rspec(IDX_W), rspec(LANES), rspec(IDX_DIM)] + _page_specs(IDX_DIM, n_pages, False),
            out_specs=[bspec(lp), bspec(1), bspec(1), bspec(1)],
        ),
        out_shape=[jax.ShapeDtypeStruct((batch, t, lp), F32)] + [jax.ShapeDtypeStruct((batch, t, 1), F32)] * 3,
        compiler_params=_params(("arbitrary", "arbitrary")),
        name="dsa_sample_score",
    )(page_table, qi, wi, ki_new, *([cache_idx_k] * PAGES_PER_STEP))
    rows = DSA_HEADS * t
    return pl.pallas_call(
        functools.partial(_dsa_sample_attn_kernel, t=t, past=past),
        grid_spec=pltpu.PrefetchScalarGridSpec(
            num_scalar_prefetch=1,
            grid=(batch, nch + 1),
            in_specs=[rspec(DSA_W), bspec(lp), bspec(1), bspec(1), bspec(1), rspec(DSA_W), rspec(DSA_W)]
            + _page_specs(DSA_W, n_pages, True) + _page_specs(DSA_W, n_pages, True),
            out_specs=rspec(DSA_W),
            scratch_shapes=[pltpu.VMEM((rows, 1), F32), pltpu.VMEM((rows, 1), F32),
                            pltpu.VMEM((rows, DSA_W), F32), pltpu.VMEM((t, _STEP_KEYS), F32),
                            pltpu.VMEM((t, LANES), F32), pltpu.VMEM((t, 1), F32)],
        ),
        out_shape=jax.ShapeDtypeStruct((batch * t, DSA_W), F32),
        compiler_params=_params(("arbitrary", "arbitrary")),
        name="dsa_sample_attn",
    )(page_table, qb, scores, thr, need, flag, k_new, v_new, *([ck] * PAGES_PER_STEP), *([cv] * PAGES_PER_STEP))


MOE_ROWS = 256
ROUTE_TILE = 512
TOKEN_TILE = 256


def _split_bf16(a):
    hi = a.astype(BF16)
    lo = (a - hi.astype(F32)).astype(BF16)
    return hi, lo


def _router_kernel(x_ref, whi_ref, wlo_ref, b_ref, idx_ref, gate_ref, rank_ref, cnt_ref, carry_scr):
    i = pl.program_id(0)
    tm = x_ref.shape[0]

    @pl.when(i == 0)
    def _():
        carry_scr[...] = jnp.zeros(carry_scr.shape, F32)

    xhi, xlo = _split_bf16(x_ref[...])
    whi = whi_ref[...]
    logits = (jnp.dot(xhi, whi, preferred_element_type=F32)
              + jnp.dot(xlo, whi, preferred_element_type=F32)
              + jnp.dot(xhi, wlo_ref[...], preferred_element_type=F32)) + b_ref[...]
    lane = lax.broadcasted_iota(I32, (tm, N_EXPERTS), 1)
    slot = lax.broadcasted_iota(I32, (tm, TOP_K), 1)
    vals, idxs = [], []
    cur = logits
    for _ in range(TOP_K):
        m = jnp.max(cur, axis=-1, keepdims=True)
        ix = jnp.min(jnp.where(cur == m, lane, N_EXPERTS), axis=-1, keepdims=True)
        vals.append(m)
        idxs.append(ix)
        cur = jnp.where(lane == ix, -jnp.inf, cur)
    es = [jnp.exp(v - vals[0]) for v in vals]
    denom = es[0] + es[1] + es[2] + es[3]
    onehot = jnp.zeros((tm, N_EXPERTS), F32)
    for ix in idxs:
        onehot = onehot + jnp.where(lane == ix, 1.0, 0.0)
    earlier = (lax.broadcasted_iota(I32, (tm, tm), 1) < lax.broadcasted_iota(I32, (tm, tm), 0))
    excl = jnp.dot(jnp.where(earlier, 1.0, 0.0).astype(BF16), onehot.astype(BF16),
                   preferred_element_type=F32) + carry_scr[...]
    idx_out = jnp.zeros((tm, TOP_K), I32)
    gate_out = jnp.zeros((tm, TOP_K), F32)
    rank_out = jnp.zeros((tm, TOP_K), F32)
    for k in range(TOP_K):
        rk = jnp.sum(jnp.where(lane == idxs[k], excl, 0.0), axis=-1, keepdims=True)
        idx_out = jnp.where(slot == k, idxs[k], idx_out)
        gate_out = jnp.where(slot == k, es[k] / denom, gate_out)
        rank_out = jnp.where(slot == k, rk, rank_out)
    idx_ref[...] = idx_out
    gate_ref[...] = gate_out
    rank_ref[...] = rank_out.astype(I32)
    total = carry_scr[...] + jnp.sum(onehot, axis=0, keepdims=True)
    carry_scr[...] = total
    cnt_ref[...] = total


def _router(x, w_router, b_router):
    n = x.shape[0]
    tm = ROUTE_TILE if n % ROUTE_TILE == 0 else TOKEN_TILE
    whi, wlo = _split_bf16(w_router)
    b2 = b_router.reshape(1, N_EXPERTS)
    kspec = pl.BlockSpec((tm, TOP_K), lambda i: (i, 0))
    return pl.pallas_call(
        _router_kernel,
        grid=(n // tm,),
        in_specs=[_row_spec(tm, D_MODEL), _full_spec(whi), _full_spec(wlo), _full_spec(b2)],
        out_specs=[kspec, kspec, kspec, pl.BlockSpec((1, N_EXPERTS), lambda i: (0, 0))],
        out_shape=[jax.ShapeDtypeStruct((n, TOP_K), I32), jax.ShapeDtypeStruct((n, TOP_K), F32),
                   jax.ShapeDtypeStruct((n, TOP_K), I32), jax.ShapeDtypeStruct((1, N_EXPERTS), F32)],
        scratch_shapes=[pltpu.VMEM((1, N_EXPERTS), F32)],
        compiler_params=_params(("arbitrary",)),
        name="router",
    )(x, whi, wlo, b2)


def _row_copy(src, src_row, dst, dst_row, sem):
    return pltpu.make_async_copy(src.at[pl.ds(src_row, 1)], dst.at[pl.ds(dst_row, 1)], sem)


def _dispatch_kernel(dest_ref, x_ref, xs_in, xs_out, sem):
    del xs_in
    tm = x_ref.shape[0]

    def issue(r, c):
        for k in range(TOP_K):
            _row_copy(x_ref, r, xs_out, dest_ref[r * TOP_K + k], sem).start()
        return c

    lax.fori_loop(0, tm, issue, 0)

    def drain(r, c):
        for k in range(TOP_K):
            _row_copy(x_ref, r, xs_out, dest_ref[r * TOP_K + k], sem).wait()
        return c

    lax.fori_loop(0, tm, drain, 0)


def _dispatch(x, dest_flat, n_rows):
    n = x.shape[0]
    tm = TOKEN_TILE
    return pl.pallas_call(
        _dispatch_kernel,
        grid=(n // tm,),
        in_specs=[pl.BlockSpec((tm * TOP_K,), lambda i: (i,), memory_space=pltpu.SMEM),
                  _row_spec(tm, D_MODEL), pl.BlockSpec(memory_space=pl.ANY)],
        out_specs=pl.BlockSpec(memory_space=pl.ANY),
        out_shape=jax.ShapeDtypeStruct((n_rows, D_MODEL), F32),
        scratch_shapes=[pltpu.SemaphoreType.DMA(())],
        input_output_aliases={2: 0},
        compiler_params=_params(("arbitrary",)),
        name="moe_dispatch",
    )(dest_flat, x, jnp.zeros((n_rows, D_MODEL), F32))


def _expert_kernel(be_ref, na_ref, x_ref, w1_ref, b1_ref, w2_ref, b2_ref, o_ref, w1b_scr, w2b_scr):
    i = pl.program_id(0)
    active = i < na_ref[0]
    changed = jnp.logical_or(i == 0, be_ref[i] != be_ref[jnp.maximum(i - 1, 0)])

    @pl.when(jnp.logical_and(active, changed))
    def _():
        w1b_scr[...] = w1_ref[0].astype(BF16)
        w2b_scr[...] = w2_ref[0].astype(BF16)

    @pl.when(active)
    def _():
        h = jnp.dot(x_ref[...].astype(BF16), w1b_scr[...], preferred_element_type=F32) + b1_ref[0]
        gt = jnp.minimum(h[:, :D_FF], SWIGLU_LIMIT)
        up = jnp.clip(h[:, D_FF:], -SWIGLU_LIMIT, SWIGLU_LIMIT)
        act = (up + 1.0) * gt * jax.nn.sigmoid(SWIGLU_ALPHA * gt)
        o_ref[...] = jnp.dot(act.astype(BF16), w2b_scr[...], preferred_element_type=F32) + b2_ref[0]

    @pl.when(jnp.logical_not(active))
    def _():
        o_ref[...] = jnp.zeros(o_ref.shape, F32)


def _experts(xs, block_e, n_active, w1, b1, w2, b2):
    n_rows = xs.shape[0]
    nblk = n_rows // MOE_ROWS
    w1 = w1.reshape(-1, D_MODEL, 2 * D_FF)
    w2 = w2.reshape(-1, D_FF, D_MODEL)
    b1r = b1.reshape(-1, 1, 2 * D_FF)
    b2r = b2.reshape(-1, 1, D_MODEL)
    last = lambda i, na: jnp.minimum(i, na[0] - 1)
    return pl.pallas_call(
        _expert_kernel,
        grid_spec=pltpu.PrefetchScalarGridSpec(
            num_scalar_prefetch=2,
            grid=(nblk,),
            in_specs=[pl.BlockSpec((MOE_ROWS, D_MODEL), lambda i, be, na: (last(i, na), 0)),
                      pl.BlockSpec((1, D_MODEL, 2 * D_FF), lambda i, be, na: (be[i], 0, 0)),
                      pl.BlockSpec((1, 1, 2 * D_FF), lambda i, be, na: (be[i], 0, 0)),
                      pl.BlockSpec((1, D_FF, D_MODEL), lambda i, be, na: (be[i], 0, 0)),
                      pl.BlockSpec((1, 1, D_MODEL), lambda i, be, na: (be[i], 0, 0))],
            out_specs=pl.BlockSpec((MOE_ROWS, D_MODEL), lambda i, be, na: (i, 0)),
            scratch_shapes=[pltpu.VMEM((D_MODEL, 2 * D_FF), BF16), pltpu.VMEM((D_FF, D_MODEL), BF16)],
        ),
        out_shape=jax.ShapeDtypeStruct((n_rows, D_MODEL), F32),
        compiler_params=_params(("arbitrary",)),
        name="moe_experts",
    )(block_e, n_active, xs, w1, b1r, w2, b2r)


def _combine_kernel(dest_ref, x_ref, gate_ref, g_ref, b_ref, yb_hbm, o_ref, buf, sem):
    tm = x_ref.shape[0]

    def issue(r, c):
        for k in range(TOP_K):
            _row_copy(yb_hbm, dest_ref[r * TOP_K + k], buf.at[k], r, sem).start()
        return c

    lax.fori_loop(0, tm, issue, 0)

    def drain(r, c):
        for k in range(TOP_K):
            _row_copy(yb_hbm, dest_ref[r * TOP_K + k], buf.at[k], r, sem).wait()
        return c

    lax.fori_loop(0, tm, drain, 0)
    gate = gate_ref[...]
    y = gate[:, 0:1] * buf[0]
    for k in range(1, TOP_K):
        y = y + gate[:, k:k + 1] * buf[k]
    o_ref[...] = _layer_norm(DN_ALPHA * x_ref[...] + y, g_ref[...], b_ref[...])


def _combine(x, yb, dest_flat, gate, g, b):
    n = x.shape[0]
    tm = TOKEN_TILE
    g2, b2 = g.reshape(1, D_MODEL), b.reshape(1, D_MODEL)
    return pl.pallas_call(
        _combine_kernel,
        grid=(n // tm,),
        in_specs=[pl.BlockSpec((tm * TOP_K,), lambda i: (i,), memory_space=pltpu.SMEM),
                  _row_spec(tm, D_MODEL), pl.BlockSpec((tm, TOP_K), lambda i: (i, 0)),
                  _full_spec(g2), _full_spec(b2), pl.BlockSpec(memory_space=pl.ANY)],
        out_specs=_row_spec(tm, D_MODEL),
        out_shape=jax.ShapeDtypeStruct((n, D_MODEL), F32),
        scratch_shapes=[pltpu.VMEM((TOP_K, tm, D_MODEL), F32), pltpu.SemaphoreType.DMA(())],
        compiler_params=_params(("arbitrary",)),
        name="moe_combine",
    )(dest_flat, x, gate, g2, b2, yb)


def _moe_layer(x, layer, w_router, b_router, w1, b1, w2, b2, g, b):
    n = x.shape[0]
    idx, gate, rank, counts = _router(x, w_router, b_router)
    nblk = (n * TOP_K) // MOE_ROWS + N_EXPERTS
    cnt = counts[0].astype(I32)
    padded = (cnt + MOE_ROWS - 1) // MOE_ROWS * MOE_ROWS
    pad_end = jnp.cumsum(padded)
    pad_start = pad_end - padded
    dest = (pad_start[idx] + rank).reshape(-1)
    n_active = (pad_end[-1] // MOE_ROWS).astype(I32)
    blk = jnp.arange(nblk, dtype=I32)
    blk = jnp.minimum(blk, n_active - 1)
    block_e = jnp.minimum(jnp.searchsorted(pad_end, blk * MOE_ROWS, side="right"), N_EXPERTS - 1).astype(I32)
    block_e = block_e + layer * N_EXPERTS
    xs = _dispatch(x, dest, nblk * MOE_ROWS)
    yb = _experts(xs, block_e, n_active.reshape(1), w1, b1, w2, b2)
    return _combine(x, yb, dest, gate, g, b)


POOL_HALO = 16


def _pool_kernel(x_ref, halo_ref, w_ref, sc_ref, g_ref, b_ref, o_ref, ext_scr, *, tiles_per_seq, n_prev):
    tm = x_ref.shape[0]
    tile = pl.program_id(0) % tiles_per_seq
    x = x_ref[...]
    halo = halo_ref[...]
    if n_prev == 0:
        halo = jnp.where(tile == 0, 0.0, halo)
    ext_scr[0:POOL_HALO] = halo
    ext_scr[POOL_HALO:POOL_HALO + tm] = x
    pos = n_prev + tile * tm + lax.broadcasted_iota(I32, (tm, 1), 0)
    parts = []
    for g, w in enumerate(POOL_WINDOWS):
        cols = slice(g * POOL_GROUP, (g + 1) * POOL_GROUP)
        xg = x[:, cols]
        s = xg
        for d in range(1, w):
            s = s + ext_scr[POOL_HALO - d:POOL_HALO - d + tm, cols]
        cnt = jnp.minimum(pos + 1, w).astype(F32)
        dg = s / cnt - xg
        parts.append(jnp.dot(dg.astype(BF16), w_ref[g], preferred_element_type=F32))
    mix = jnp.concatenate(parts, axis=1) * sc_ref[...]
    o_ref[...] = _layer_norm(DN_ALPHA * x + mix, g_ref[...], b_ref[...])


def _pool(x, halo_src, pool_w, pool_scale, g, b, tm, tiles_per_seq, n_prev, halo_map):
    n = x.shape[0]
    w16 = pool_w.astype(BF16)
    sc = pool_scale.reshape(1, D_MODEL)
    g2, b2 = g.reshape(1, D_MODEL), b.reshape(1, D_MODEL)
    return pl.pallas_call(
        functools.partial(_pool_kernel, tiles_per_seq=tiles_per_seq, n_prev=n_prev),
        grid=(n // tm,),
        in_specs=[_row_spec(tm, D_MODEL), pl.BlockSpec((POOL_HALO, D_MODEL), halo_map),
                  _full_spec(w16), _full_spec(sc), _full_spec(g2), _full_spec(b2)],
        out_specs=_row_spec(tm, D_MODEL),
        out_shape=jax.ShapeDtypeStruct((n, D_MODEL), F32),
        scratch_shapes=[pltpu.VMEM((POOL_HALO + tm, D_MODEL), F32)],
        compiler_params=_params(("parallel",)),
        name="pool",
    )(x, halo_src, w16, sc, g2, b2)


PROJ_TILE = 512
DSA_Q_BLOCK = 256
DSA_K_BLOCK = 512
POOL_TILE = 512


def kernel(x_prompt, x_sample, cache_k, cache_v, cache_idx_k, page_table, state_gla, state_pool,
           w_in, gla_fg_w2, gla_fg_b, gla_norm_g, idx_kn_g, idx_kn_b, w_out,
           pool_w, pool_scale, ln_mix_g, ln_mix_b, ln_ffn_g, ln_ffn_b,
           moe_router_w, moe_router_b, moe_w1, moe_b1, moe_w2, moe_b2):
    bp, sp, _ = x_prompt.shape
    bs, ts, _ = x_sample.shape
    n_p, n_s = bp * sp, bs * ts
    xp = x_prompt.reshape(n_p, D_MODEL)
    xs = x_sample.reshape(n_s, D_MODEL)
    proj_w = (w_in, gla_fg_w2, gla_fg_b, idx_kn_g, idx_kn_b)

    qa, ka, va, ga, lf, qb, kb, vb, qi, ki, wi, kb16, vb16 = _project(xp, *proj_w, tm=PROJ_TILE)
    oa_p, gla_p = _gla(qa, ka, va, lf, ga, jnp.zeros((bp, GLA_HEADS, GLA_DK, GLA_DV), F32), gla_norm_g, bp, sp)
    kit = jnp.swapaxes(ki.reshape(bp, sp, IDX_DIM), 1, 2)
    ob_p = _dsa_prompt(qb, qi, wi, kit, kb16, vb16, bp, sp, DSA_Q_BLOCK, DSA_K_BLOCK)
    x1p = _merge(xp, oa_p, ob_p, w_out, ln_mix_g[0], ln_mix_b[0], PROJ_TILE)
    k_p = kb.reshape(bp, sp, DSA_HEADS, DSA_DH)
    v_p = vb.reshape(bp, sp, DSA_HEADS, DSA_DH)
    kidx_p = ki.reshape(bp, sp, IDX_DIM)

    qa, ka, va, ga, lf, qb, kb, vb, qi, ki, wi, _, _ = _project(xs, *proj_w, tm=n_s)
    oa_s, gla_s = _gla(qa, ka, va, lf, ga, state_gla, gla_norm_g, bs, ts)
    ob_s = _dsa_sample(qb, qi, wi, ki, kb, vb, cache_k, cache_v, cache_idx_k, page_table, bs, ts)
    x1s = _merge(xs, oa_s, ob_s, w_out, ln_mix_g[0], ln_mix_b[0], n_s)
    k_s = kb.reshape(bs, ts, DSA_HEADS, DSA_DH)
    v_s = vb.reshape(bs, ts, DSA_HEADS, DSA_DH)
    kidx_s = ki.reshape(bs, ts, IDX_DIM)

    x_all = _moe_layer(jnp.concatenate([x1p, x1s], axis=0), 0, moe_router_w[0], moe_router_b[0],
                       moe_w1, moe_b1, moe_w2, moe_b2, ln_ffn_g[0], ln_ffn_b[0])
    xp, xs = x_all[:n_p], x_all[n_p:]

    pool_p = xp.reshape(bp, sp, D_MODEL)[:, -POOL_STATE:]
    xs3 = xs.reshape(bs, ts, D_MODEL)
    pool_s = jnp.concatenate([state_pool, xs3], axis=1)[:, -POOL_STATE:]
    per_seq = sp // POOL_TILE
    halo_step = POOL_TILE // POOL_HALO
    x2p = _pool(xp, xp, pool_w, pool_scale, ln_mix_g[1], ln_mix_b[1], POOL_TILE, per_seq, 0,
                lambda i: (jnp.maximum(i * halo_step - 1, 0), 0))
    halo_s = jnp.concatenate([jnp.zeros((bs, POOL_HALO - POOL_STATE, D_MODEL), F32), state_pool], axis=1)
    x2s = _pool(xs, halo_s.reshape(bs * POOL_HALO, D_MODEL), pool_w, pool_scale, ln_mix_g[1], ln_mix_b[1],
                ts, 1, POOL_STATE, lambda i: (i, 0))
    x_all = _moe_layer(jnp.concatenate([x2p, x2s], axis=0), 1, moe_router_w[1], moe_router_b[1],
                       moe_w1, moe_b1, moe_w2, moe_b2, ln_ffn_g[1], ln_ffn_b[1])
    y_p = x_all[:n_p].reshape(bp, sp, D_MODEL)
    y_s = x_all[n_p:].reshape(bs, ts, D_MODEL)
    return (y_p, y_s, k_p, v_p, kidx_p, gla_p, pool_p, k_s, v_s, kidx_s, gla_s, pool_s)
```

```python
import functools
import math

import jax
import jax.numpy as jnp
from jax import lax
from jax.experimental import pallas as pl
from jax.experimental.pallas import tpu as pltpu

F32 = jnp.float32
BF16 = jnp.bfloat16
I32 = jnp.int32

D_MODEL = 1024
DEPTH = 2
PAGE_SIZE = 128
GLA_HEADS = 4
GLA_DK = 64
GLA_DV = 128
GLA_GATE_RANK = 16
GLA_TAU = 16.0
GLA_CHUNK = 64
DSA_HEADS = 8
DSA_DH = 64
IDX_HEADS = 4
IDX_DIM = 64
DSA_TOPK = 256
IDX_W_SCALE = (IDX_HEADS ** -0.5) * (IDX_DIM ** -0.5)
POOL_WINDOWS = (2, 4, 8, 16)
POOL_GROUP = D_MODEL // 4
POOL_STATE = 16 - 1
N_EXPERTS = 32
TOP_K = 4
D_FF = D_MODEL
SWIGLU_ALPHA = 1.702
SWIGLU_LIMIT = 7.0
DN_ALPHA = (2 * DEPTH) ** 0.25
LN_EPS = 1e-5
GLA_QK = GLA_HEADS * GLA_DK
GLA_V = GLA_HEADS * GLA_DV
DSA_W = DSA_HEADS * DSA_DH
IDX_W = IDX_HEADS * IDX_DIM

LANES = 128
SUBLANES = 8
VMEM_LIMIT_BYTES = 56 * 1024 * 1024

NEG_BIG = -1e30


def _params(sem, vmem=VMEM_LIMIT_BYTES):
    return pltpu.CompilerParams(dimension_semantics=sem, vmem_limit_bytes=vmem)


def _layer_norm(x, g, b):
    mu = jnp.mean(x, axis=-1, keepdims=True)
    xc = x - mu
    var = jnp.mean(xc * xc, axis=-1, keepdims=True)
    return xc * lax.rsqrt(var + LN_EPS) * g + b


def _row_spec(tm, w):
    return pl.BlockSpec((tm, w), lambda i: (i, 0))


def _full_spec(a):
    nd = a.ndim
    return pl.BlockSpec(a.shape, lambda i: (0,) * nd)


_MAIN_W = (GLA_QK, GLA_QK, GLA_V, GLA_V, DSA_W, DSA_W, DSA_W, IDX_W)
_MAIN_OFF = tuple(sum(_MAIN_W[:i]) for i in range(len(_MAIN_W) + 1))


def _proj_kernel(x_ref, wm_ref, ws_ref, fgw_ref, fgb_ref, kng_ref, knb_ref,
                 qa_ref, ka_ref, va_ref, ga_ref, lf_ref, qb_ref, kb_ref, vb_ref,
                 qi_ref, ki_ref, wi_ref, kb16_ref, vt16_ref, ki16_ref):
    xb = x_ref[...].astype(BF16)

    def mm(n):
        return jnp.dot(xb, wm_ref[:, _MAIN_OFF[n]:_MAIN_OFF[n + 1]], preferred_element_type=F32)

    qa_ref[...] = mm(0) * (GLA_DK ** -0.5)
    ka_ref[...] = mm(1)
    va_ref[...] = mm(2)
    ga_ref[...] = mm(3)
    qb_ref[...] = mm(4)
    kb = mm(5)
    kb_ref[...] = kb
    kb16_ref[...] = kb.astype(BF16)
    vb = mm(6)
    vb_ref[...] = vb
    vt16_ref[...] = vb.T.astype(BF16)
    qi_ref[...] = mm(7)

    small = jnp.dot(xb, ws_ref[...], preferred_element_type=F32)
    ki = _layer_norm(small[:, :IDX_DIM], kng_ref[...], knb_ref[...])
    ki_ref[...] = ki
    ki16_ref[...] = ki.astype(BF16)
    fa = small[:, IDX_DIM:IDX_DIM + GLA_GATE_RANK]
    z = jnp.dot(fa, fgw_ref[...], preferred_element_type=F32,
                precision=lax.Precision.HIGHEST) + fgb_ref[...]
    lf_ref[...] = (jnp.minimum(z, 0.0) - jnp.log(1.0 + jnp.exp(-jnp.abs(z)))) * (1.0 / GLA_TAU)
    wi_ref[...] = small * IDX_W_SCALE


def _project(x2d, w_in, gla_fg_w2, gla_fg_b, idx_kn_g, idx_kn_b, tm):
    n = x2d.shape[0]
    pts = [0]
    for s in (GLA_QK, GLA_QK, GLA_V, GLA_V, GLA_GATE_RANK, DSA_W, DSA_W, DSA_W, IDX_W, IDX_DIM, IDX_HEADS):
        pts.append(pts[-1] + s)
    seg = lambda i: w_in[:, pts[i]:pts[i + 1]]
    wm = jnp.concatenate([seg(0), seg(1), seg(2), seg(3), seg(5), seg(6), seg(7), seg(8)], axis=1).astype(BF16)
    pad = LANES - IDX_DIM - GLA_GATE_RANK - IDX_HEADS
    ws = jnp.concatenate([seg(9), seg(4), seg(10), jnp.zeros((D_MODEL, pad), w_in.dtype)], axis=1).astype(BF16)
    fgb = gla_fg_b.reshape(1, GLA_QK)
    kng = idx_kn_g.reshape(1, IDX_DIM)
    knb = idx_kn_b.reshape(1, IDX_DIM)
    widths = (GLA_QK, GLA_QK, GLA_V, GLA_V, GLA_QK, DSA_W, DSA_W, DSA_W, IDX_W, IDX_DIM, LANES)
    out_shape = [jax.ShapeDtypeStruct((n, w), F32) for w in widths]
    out_shape += [jax.ShapeDtypeStruct((n, DSA_W), BF16), jax.ShapeDtypeStruct((DSA_W, n), BF16),
                  jax.ShapeDtypeStruct((n, IDX_DIM), BF16)]
    out_specs = [_row_spec(tm, w) for w in widths]
    out_specs += [_row_spec(tm, DSA_W), pl.BlockSpec((DSA_W, tm), lambda i: (0, i)), _row_spec(tm, IDX_DIM)]
    return pl.pallas_call(
        _proj_kernel,
        grid=(n // tm,),
        in_specs=[_row_spec(tm, D_MODEL), _full_spec(wm), _full_spec(ws), _full_spec(gla_fg_w2),
                  _full_spec(fgb), _full_spec(kng), _full_spec(knb)],
        out_specs=out_specs,
        out_shape=out_shape,
        compiler_params=_params(("parallel",)),
        name="proj",
    )(x2d, wm, ws, gla_fg_w2, fgb, kng, knb)


def _gla_kernel(q_ref, k_ref, v_ref, lf_ref, ga_ref, s0_ref, ng_ref, o_ref, sout_ref, s_scr, *, chunk):
    c = pl.program_id(1)
    nc = pl.num_programs(1)

    @pl.when(c == 0)
    def _():
        s_scr[...] = s0_ref[0]

    g = lf_ref[...]
    ri = lax.broadcasted_iota(I32, (chunk, chunk), 0)
    ci = lax.broadcasted_iota(I32, (chunk, chunk), 1)
    causal = ci <= ri
    tri = jnp.where(causal, 1.0, 0.0).astype(F32)
    b = jnp.dot(tri, g, preferred_element_type=F32, precision=lax.Precision.HIGHEST)
    mid = chunk // 2
    b_mid = b[mid:mid + 1, :]
    b_last = b[chunk - 1:chunk, :]
    q = q_ref[...]
    k = k_ref[...]
    q_in = q * jnp.exp(b)
    q_rel = q * jnp.exp(b - b_mid)
    k_rel = k * jnp.exp(b_mid - b)
    k_out = k * jnp.exp(b_last - b)
    dec_last = jnp.exp(b_last)
    v = v_ref[...]
    ga = ga_ref[...]
    ng = ng_ref[...]
    eye = jnp.where(lax.broadcasted_iota(I32, (GLA_DK, GLA_DK), 0)
                    == lax.broadcasted_iota(I32, (GLA_DK, GLA_DK), 1), 1.0, 0.0).astype(F32)
    for h in range(GLA_HEADS):
        ks = slice(h * GLA_DK, (h + 1) * GLA_DK)
        vs = slice(h * GLA_DV, (h + 1) * GLA_DV)
        s_h = s_scr[h]
        vh = v[:, vs]
        inter = jnp.dot(q_in[:, ks], s_h, preferred_element_type=F32)
        att = lax.dot_general(q_rel[:, ks], k_rel[:, ks], (((1,), (1,)), ((), ())),
                              preferred_element_type=F32)
        att = jnp.where(causal, att, 0.0)
        o = inter + jnp.dot(att, vh, preferred_element_type=F32)
        kv = lax.dot_general(k_out[:, ks], vh, (((0,), (0,)), ((), ())), preferred_element_type=F32)
        s_scr[h] = jnp.dot(eye * dec_last[:, ks], s_h, preferred_element_type=F32,
                           precision=lax.Precision.HIGHEST) + kv
        ms = jnp.mean(o * o, axis=-1, keepdims=True)
        gh = ga[:, vs]
        o_ref[:, vs] = o * lax.rsqrt(ms + LN_EPS) * ng * (gh * jax.nn.sigmoid(gh))

    @pl.when(c == nc - 1)
    def _():
        sout_ref[0] = s_scr[...]


def _gla(qa, ka, va, lf, ga, s0, gla_norm_g, batch, seq):
    chunk = math.gcd(seq, GLA_CHUNK)
    nc = seq // chunk
    ng = gla_norm_g.reshape(1, GLA_DV)
    spec = lambda w: pl.BlockSpec((chunk, w), lambda b, c: (b * nc + c, 0))
    sspec = pl.BlockSpec((1, GLA_HEADS, GLA_DK, GLA_DV), lambda b, c: (b, 0, 0, 0))
    return pl.pallas_call(
        functools.partial(_gla_kernel, chunk=chunk),
        grid=(batch, nc),
        in_specs=[spec(GLA_QK), spec(GLA_QK), spec(GLA_V), spec(GLA_QK), spec(GLA_V), sspec,
                  pl.BlockSpec((1, GLA_DV), lambda b, c: (0, 0))],
        out_specs=[spec(GLA_V), sspec],
        out_shape=[jax.ShapeDtypeStruct((batch * seq, GLA_V), F32),
                   jax.ShapeDtypeStruct((batch, GLA_HEADS, GLA_DK, GLA_DV), F32)],
        scratch_shapes=[pltpu.VMEM((GLA_HEADS, GLA_DK, GLA_DV), F32)],
        compiler_params=_params(("parallel", "arbitrary")),
        name="gla",
    )(qa, ka, va, lf, ga, s0, ng)


def _merge_kernel(x_ref, oa_ref, ob_ref, w_ref, g_ref, b_ref, o_ref):
    mix = jnp.dot(oa_ref[...].astype(BF16), w_ref[:GLA_V, :], preferred_element_type=F32)
    mix = mix + jnp.dot(ob_ref[...].astype(BF16), w_ref[GLA_V:, :], preferred_element_type=F32)
    o_ref[...] = _layer_norm(DN_ALPHA * x_ref[...] + mix, g_ref[...], b_ref[...])


def _merge(x2d, oa, ob, w_out, g, b, tm):
    n = x2d.shape[0]
    w16 = w_out.astype(BF16)
    g2, b2 = g.reshape(1, D_MODEL), b.reshape(1, D_MODEL)
    return pl.pallas_call(
        _merge_kernel,
        grid=(n // tm,),
        in_specs=[_row_spec(tm, D_MODEL), _row_spec(tm, GLA_V), _row_spec(tm, DSA_W),
                  _full_spec(w16), _full_spec(g2), _full_spec(b2)],
        out_specs=_row_spec(tm, D_MODEL),
        out_shape=jax.ShapeDtypeStruct((n, D_MODEL), F32),
        compiler_params=_params(("parallel",)),
        name="merge",
    )(x2d, oa, ob, w16, g2, b2)


_KEY_NEG_INF = -2139095041
_KEY_POS_INF = 2139095040
_WI_LANE = IDX_DIM + GLA_GATE_RANK


def _unkey(kk):
    return lax.bitcast_convert_type(jnp.where(kk < 0, kk ^ 0x7FFFFFFF, kk), F32)


def _fold_keys(m, key_axis):
    if key_axis == 1:
        out = m[:, :LANES]
        for t in range(1, m.shape[1] // LANES):
            out = out + m[:, t * LANES:(t + 1) * LANES]
        return out
    return jnp.sum(m.reshape(m.shape[0] // SUBLANES, SUBLANES, m.shape[1]), axis=0)


def _kth_largest_threshold(read_block, n_blocks, rows, k, active, key_axis=1):
    kf = float(k)
    qshape = (rows, 1) if key_axis == 1 else (1, rows)
    part = (rows, LANES) if key_axis == 1 else (SUBLANES, rows)

    def count(pred):
        def body(c, acc):
            return acc + _fold_keys(jnp.where(pred(read_block(c)), 1.0, 0.0), key_axis)
        acc = lax.fori_loop(0, n_blocks, body, jnp.zeros(part, F32))
        return jnp.sum(acc, axis=key_axis, keepdims=True)

    def cond(st):
        return st[4] > 0

    def body(st):
        it, lo, hi, cnt_lo, _ = st
        mid = (lo >> 1) + (hi >> 1) + (lo & hi & 1)
        cand = _unkey(mid)
        cnt = count(lambda s: s >= cand)
        ge = cnt >= kf
        lo = jnp.where(ge, mid, lo)
        cnt_lo = jnp.where(ge, cnt, cnt_lo)
        hi = jnp.where(ge, hi, mid)
        done = (cnt_lo == kf) | (hi == lo + 1) | jnp.logical_not(active)
        go = jnp.max(jnp.where(done, 0, 1)).astype(I32)
        go = jnp.where(it < 40, go, 0)
        return it + 1, lo, hi, cnt_lo, go

    lo0 = jnp.full(qshape, _KEY_NEG_INF, I32)
    hi0 = jnp.full(qshape, _KEY_POS_INF, I32)
    go0 = jnp.max(jnp.where(active, 1, 0)).astype(I32)
    _, lo, _, _, _ = lax.while_loop(cond, body, (jnp.int32(0), lo0, hi0, jnp.full(qshape, -1.0, F32), go0))
    thr = jnp.where(active, _unkey(lo), -jnp.inf)
    n_gt = count(lambda s: s > thr)
    n_ge = count(lambda s: s >= thr)
    need = jnp.where(active, kf - n_gt, 0.0)
    ties = jnp.logical_and(active, (n_ge - n_gt) > need)
    tie_flag = jnp.max(jnp.where(ties, 1, 0)).astype(I32)
    return thr, need, tie_flag


def _selection_bias(s, thr, need, tie_flag, tie_cnt_ref, bias_ref, key_axis=1):
    w = s.shape[key_axis]

    @pl.when(tie_flag == 0)
    def _():
        sel = jnp.logical_and(s >= thr, s > -jnp.inf)
        bias_ref[...] = jnp.where(sel, 0.0, NEG_BIG)

    @pl.when(tie_flag != 0)
    def _():
        eq = s == thr
        eqf = jnp.where(eq, 1.0, 0.0)
        r_i = lax.broadcasted_iota(I32, (w, w), 0)
        c_i = lax.broadcasted_iota(I32, (w, w), 1)
        if key_axis == 1:
            earlier = jnp.where(r_i < c_i, 1.0, 0.0).astype(BF16)
            rank = jnp.dot(eqf.astype(BF16), earlier, preferred_element_type=F32)
        else:
            earlier = jnp.where(c_i < r_i, 1.0, 0.0).astype(BF16)
            rank = jnp.dot(earlier, eqf.astype(BF16), preferred_element_type=F32)
        rank = rank + tie_cnt_ref[...]
        sel = jnp.logical_or(s > thr, jnp.logical_and(eq, rank < need))
        bias_ref[...] = jnp.where(sel, 0.0, NEG_BIG)
        tie_cnt_ref[...] = tie_cnt_ref[...] + jnp.sum(eqf, axis=key_axis, keepdims=True)


def _dsa_prompt_kernel(qb_ref, qi_ref, wi_ref, ki_ref, k_hbm, vt_hbm, o_ref,
                       k_scr, vt_scr, sc_scr, qm_scr, m_scr, l_scr, acc_scr, bias_scr, tie_scr, sem,
                       *, q_blk, k_blk, seq, n_sel):
    b = pl.program_id(0)
    i = pl.program_id(1)

    @pl.when(i == 0)
    def _():
        ck = pltpu.make_async_copy(k_hbm.at[pl.ds(b * seq, seq)], k_scr, sem.at[0])
        cv = pltpu.make_async_copy(vt_hbm.at[:, pl.ds(b * seq, seq)], vt_scr, sem.at[1])
        ck.start()
        cv.start()
        ck.wait()
        cv.wait()

    q0 = i * q_blk
    n_blocks = (q0 + q_blk + k_blk - 1) // k_blk
    q_pos = q0 + lax.broadcasted_iota(I32, (1, q_blk), 1)

    qit = qi_ref[...].T.astype(BF16)
    qit_h = [qit[h * IDX_DIM:(h + 1) * IDX_DIM] for h in range(IDX_HEADS)]
    wit = wi_ref[...].T
    w_h = [wit[_WI_LANE + h:_WI_LANE + h + 1] for h in range(IDX_HEADS)]

    def score_body(c, carry):
        off = pl.multiple_of(c * k_blk, k_blk)
        kc = ki_ref[pl.ds(off, k_blk), :]
        s = w_h[0] * jnp.maximum(jnp.dot(kc, qit_h[0], preferred_element_type=F32), 0.0)
        for h in range(1, IDX_HEADS):
            s = s + w_h[h] * jnp.maximum(jnp.dot(kc, qit_h[h], preferred_element_type=F32), 0.0)
        k_pos = off + lax.broadcasted_iota(I32, (k_blk, 1), 0)
        sc_scr[pl.ds(off, k_blk), :] = jnp.where(k_pos <= q_pos, s, -jnp.inf)
        return carry

    lax.fori_loop(0, n_blocks, score_body, 0)

    def read_block(c):
        return sc_scr[pl.ds(pl.multiple_of(c * k_blk, k_blk), k_blk), :]

    active = (q_pos + 1) > n_sel
    thr, need, tie_flag = _kth_largest_threshold(read_block, n_blocks, q_blk, n_sel, active, key_axis=0)

    qt = (qb_ref[...] * (DSA_DH ** -0.5)).T
    row_lo = lax.broadcasted_iota(I32, (LANES, 1), 0) < DSA_DH
    for p in range(DSA_HEADS // 2):
        qp = qt[p * LANES:(p + 1) * LANES]
        qm_scr[2 * p] = jnp.where(row_lo, qp, 0.0).astype(BF16)
        qm_scr[2 * p + 1] = jnp.where(row_lo, 0.0, qp).astype(BF16)
    m_scr[...] = jnp.full(m_scr.shape, NEG_BIG, F32)
    l_scr[...] = jnp.zeros(l_scr.shape, F32)
    acc_scr[...] = jnp.zeros(acc_scr.shape, F32)
    tie_scr[...] = jnp.zeros(tie_scr.shape, F32)

    def att_body(j, carry):
        off = pl.multiple_of(j * k_blk, k_blk)
        _selection_bias(sc_scr[pl.ds(off, k_blk), :], thr, need, tie_flag, tie_scr, bias_scr, key_axis=0)
        bias = bias_scr[...]

        def qk(h):
            p = h // 2
            kp = k_scr[pl.ds(off, k_blk), p * LANES:(p + 1) * LANES]
            return jnp.dot(kp, qm_scr[h], preferred_element_type=F32)

        prs, alphas = [], []
        for h in range(DSA_HEADS):
            logit = qk(h) + bias
            m_old = m_scr[h]
            m_new = jnp.maximum(m_old, jnp.max(logit, axis=0, keepdims=True))
            alpha = jnp.exp(m_old - m_new)
            pe = jnp.exp(logit - m_new)
            l_scr[h] = alpha * l_scr[h] + jnp.sum(pe, axis=0, keepdims=True)
            m_scr[h] = m_new
            prs.append(pe.astype(BF16))
            alphas.append(alpha)
        for h in range(DSA_HEADS):
            rows = slice(h * DSA_DH, (h + 1) * DSA_DH)
            vth = vt_scr[rows, pl.ds(off, k_blk)]
            acc_scr[rows, :] = alphas[h] * acc_scr[rows, :] + jnp.dot(vth, prs[h], preferred_element_type=F32)
        return carry

    lax.fori_loop(0, n_blocks, att_body, 0)

    for h in range(DSA_HEADS):
        rows = slice(h * DSA_DH, (h + 1) * DSA_DH)
        acc_scr[rows, :] = acc_scr[rows, :] / l_scr[h]
    o_ref[...] = acc_scr[...].T


def _dsa_prompt(qb, qi, wi, ki16, k16, vt16, batch, seq, q_blk, k_blk):
    n_sel = min(DSA_TOPK, seq // 4)
    nq = seq // q_blk
    qspec = lambda w: pl.BlockSpec((q_blk, w), lambda b, i: (b * nq + i, 0))
    return pl.pallas_call(
        functools.partial(_dsa_prompt_kernel, q_blk=q_blk, k_blk=k_blk, seq=seq, n_sel=n_sel),
        grid=(batch, nq),
        in_specs=[qspec(DSA_W), qspec(IDX_W), qspec(LANES),
                  pl.BlockSpec((seq, IDX_DIM), lambda b, i: (b, 0)),
                  pl.BlockSpec(memory_space=pl.ANY), pl.BlockSpec(memory_space=pl.ANY)],
        out_specs=qspec(DSA_W),
        out_shape=jax.ShapeDtypeStruct((batch * seq, DSA_W), F32),
        scratch_shapes=[pltpu.VMEM((seq, DSA_W), BF16), pltpu.VMEM((DSA_W, seq), BF16),
                        pltpu.VMEM((seq, q_blk), F32),
                        pltpu.VMEM((DSA_HEADS, LANES, q_blk), BF16),
                        pltpu.VMEM((DSA_HEADS, 1, q_blk), F32), pltpu.VMEM((DSA_HEADS, 1, q_blk), F32),
                        pltpu.VMEM((DSA_W, q_blk), F32), pltpu.VMEM((k_blk, q_blk), F32),
                        pltpu.VMEM((1, q_blk), F32), pltpu.SemaphoreType.DMA((2,))],
        compiler_params=_params(("arbitrary", "arbitrary")),
        name="dsa_prompt",
    )(qb, qi, wi, ki16, k16, vt16)


PAGES_PER_STEP = 8
SCORE_PAGES_PER_STEP = 32
_STEP_KEYS = PAGES_PER_STEP * PAGE_SIZE
_NT = (((1,), (1,)), ((), ()))


def _page_specs(page_shape, n_pages, per_step, clamp_last):
    specs = []
    zeros = (0,) * len(page_shape)
    for s in range(per_step):
        def imap(b, c, pt, s=s):
            page = jnp.minimum(c * per_step + s, n_pages - per_step + s) if clamp_last else c * per_step + s
            return (pt[b, page],) + zeros
        specs.append(pl.BlockSpec((1,) + page_shape, imap))
    return specs


def _stack_heads(qi):
    return jnp.concatenate([qi[:, h * IDX_DIM:(h + 1) * IDX_DIM] for h in range(IDX_HEADS)], axis=0)


def _idx_score(s4, wi, t):
    s = wi[:, _WI_LANE:_WI_LANE + 1] * jnp.maximum(s4[:t], 0.0)
    for h in range(1, IDX_HEADS):
        s = s + wi[:, _WI_LANE + h:_WI_LANE + h + 1] * jnp.maximum(s4[h * t:(h + 1) * t], 0.0)
    return s


def _dsa_sample_score_kernel(pt_ref, qi_ref, wi_ref, kin_ref, *rest, t, past, n_sel):
    pages = rest[:SCORE_PAGES_PER_STEP]
    sc_ref, thr_ref, need_ref, flag_ref = rest[SCORE_PAGES_PER_STEP:]
    c = pl.program_id(1)
    nch = pl.num_programs(1)
    step_keys = SCORE_PAGES_PER_STEP * PAGE_SIZE
    qi4 = _stack_heads(qi_ref[...]).astype(BF16)
    wi = wi_ref[...]
    keys_t = jnp.concatenate([p[0] for p in pages], axis=1).astype(BF16)
    off = pl.multiple_of(c * step_keys, step_keys)
    sc_ref[0, :, pl.ds(off, step_keys)] = _idx_score(
        jnp.dot(qi4, keys_t, preferred_element_type=F32), wi, t)

    @pl.when(c == nch - 1)
    def _():
        knew = jnp.concatenate([kin_ref[...], jnp.zeros((LANES - t, IDX_DIM), F32)], axis=0).astype(BF16)
        s = _idx_score(lax.dot_general(qi4, knew, _NT, preferred_element_type=F32), wi, t)
        col = lax.broadcasted_iota(I32, (t, LANES), 1)
        row = lax.broadcasted_iota(I32, (t, LANES), 0)
        sc_ref[0, :, past:past + LANES] = jnp.where(col <= row, s, -jnp.inf)
        active = (past + 1 + lax.broadcasted_iota(I32, (t, 1), 0)) > n_sel
        thr, need, flag = _kth_largest_threshold(lambda _: sc_ref[0], 1, t, n_sel, active)
        thr_ref[0] = thr
        need_ref[0] = need
        flag_ref[0] = jnp.zeros((t, 1), F32) + flag.astype(F32)


def _dsa_sample_attn_kernel(pt_ref, qb_ref, sc_ref, thr_ref, need_ref, flag_ref, kn_ref, vn_ref, *rest,
                            t, past):
    kpages = rest[:PAGES_PER_STEP]
    vpages = rest[PAGES_PER_STEP:2 * PAGES_PER_STEP]
    o_ref, m_scr, l_scr, acc_scr, bias_scr, biasn_scr, tie_scr = rest[2 * PAGES_PER_STEP:]
    c = pl.program_id(1)
    nch = pl.num_programs(1) - 1

    @pl.when(c == 0)
    def _():
        m_scr[...] = jnp.full(m_scr.shape, NEG_BIG, F32)
        l_scr[...] = jnp.zeros(l_scr.shape, F32)
        acc_scr[...] = jnp.zeros(acc_scr.shape, F32)
        tie_scr[...] = jnp.zeros(tie_scr.shape, F32)

    rows = DSA_HEADS * t
    blockmask = (lax.broadcasted_iota(I32, (rows, DSA_W), 0) // t
                 == lax.broadcasted_iota(I32, (rows, DSA_W), 1) // DSA_DH)
    qs = qb_ref[...] * (DSA_DH ** -0.5)
    qbd = jnp.where(blockmask, jnp.concatenate([qs] * DSA_HEADS, axis=0), 0.0).astype(BF16)
    thr = thr_ref[0]
    need = need_ref[0]
    flag = jnp.max(flag_ref[0]).astype(I32)

    def update(bias, k_op, v_op, keys_on_lanes):
        if keys_on_lanes:
            logit = jnp.dot(qbd, k_op, preferred_element_type=F32)
        else:
            logit = lax.dot_general(qbd, k_op, _NT, preferred_element_type=F32)
        logit = logit + jnp.concatenate([bias] * DSA_HEADS, axis=0)
        m_old = m_scr[...]
        m_new = jnp.maximum(m_old, jnp.max(logit, axis=-1, keepdims=True))
        alpha = jnp.exp(m_old - m_new)
        pr = jnp.exp(logit - m_new).astype(BF16)
        l_scr[...] = alpha * l_scr[...] + jnp.sum(pr.astype(F32), axis=-1, keepdims=True)
        m_scr[...] = m_new
        if keys_on_lanes:
            pv = lax.dot_general(pr, v_op, _NT, preferred_element_type=F32)
        else:
            pv = jnp.dot(pr, v_op, preferred_element_type=F32)
        acc_scr[...] = alpha * acc_scr[...] + pv

    @pl.when(c < nch)
    def _():
        off = pl.multiple_of(c * _STEP_KEYS, _STEP_KEYS)
        _selection_bias(sc_ref[0, :, pl.ds(off, _STEP_KEYS)], thr, need, flag, tie_scr, bias_scr)
        kc = jnp.concatenate([p[0].reshape(DSA_W, PAGE_SIZE) for p in kpages], axis=1).astype(BF16)
        vc = jnp.concatenate([p[0].reshape(DSA_W, PAGE_SIZE) for p in vpages], axis=1).astype(BF16)
        update(bias_scr[...], kc, vc, True)

    @pl.when(c == nch)
    def _():
        _selection_bias(sc_ref[0, :, past:past + LANES], thr, need, flag, tie_scr, biasn_scr)
        zpad = jnp.zeros((LANES - t, DSA_W), F32)
        kc = jnp.concatenate([kn_ref[...], zpad], axis=0).astype(BF16)
        vc = jnp.concatenate([vn_ref[...], zpad], axis=0).astype(BF16)
        update(biasn_scr[...], kc, vc, False)
        outn = jnp.where(blockmask, acc_scr[...] / l_scr[...], 0.0)
        out = outn[:t]
        for h in range(1, DSA_HEADS):
            out = out + outn[h * t:(h + 1) * t]
        o_ref[...] = out


def _dsa_sample(qb, qi, wi, ki_new, k_new, v_new, cache_k, cache_v, cache_idx_k, page_table, batch, t):
    n_pages = page_table.shape[1]
    past = n_pages * PAGE_SIZE
    n_sel = min(DSA_TOPK, (past + t) // 4)
    nch = n_pages // PAGES_PER_STEP
    nch_score = n_pages // SCORE_PAGES_PER_STEP
    lp = past + LANES
    kv_page = (DSA_HEADS, DSA_DH, PAGE_SIZE)
    ck = jnp.transpose(cache_k, (0, 2, 3, 1))
    cv = jnp.transpose(cache_v, (0, 2, 3, 1))
    cik = jnp.transpose(cache_idx_k, (0, 2, 1))
    rspec = lambda w: pl.BlockSpec((t, w), lambda b, c, pt: (b, 0))
    bspec = lambda w: pl.BlockSpec((1, t, w), lambda b, c, pt: (b, 0, 0))
    scores, thr, need, flag = pl.pallas_call(
        functools.partial(_dsa_sample_score_kernel, t=t, past=past, n_sel=n_sel),
        grid_spec=pltpu.PrefetchScalarGridSpec(
            num_scalar_prefetch=1,
            grid=(batch, nch_score),
            in_specs=[rspec(IDX_W), rspec(LANES), rspec(IDX_DIM)]
            + _page_specs((IDX_DIM, PAGE_SIZE), n_pages, SCORE_PAGES_PER_STEP, False),
            out_specs=[bspec(lp), bspec(1), bspec(1), bspec(1)],
        ),
        out_shape=[jax.ShapeDtypeStruct((batch, t, lp), F32)] + [jax.ShapeDtypeStruct((batch, t, 1), F32)] * 3,
        compiler_params=_params(("arbitrary", "arbitrary")),
        name="dsa_sample_score",
    )(page_table, qi, wi, ki_new, *([cik] * SCORE_PAGES_PER_STEP))
    return pl.pallas_call(
        functools.partial(_dsa_sample_attn_kernel, t=t, past=past),
        grid_spec=pltpu.PrefetchScalarGridSpec(
            num_scalar_prefetch=1,
            grid=(batch, nch + 1),
            in_specs=[rspec(DSA_W), bspec(lp), bspec(1), bspec(1), bspec(1), rspec(DSA_W), rspec(DSA_W)]
            + _page_specs(kv_page, n_pages, PAGES_PER_STEP, True)
            + _page_specs(kv_page, n_pages, PAGES_PER_STEP, True),
            out_specs=rspec(DSA_W),
            scratch_shapes=[pltpu.VMEM((DSA_HEADS * t, 1), F32), pltpu.VMEM((DSA_HEADS * t, 1), F32),
                            pltpu.VMEM((DSA_HEADS * t, DSA_W), F32), pltpu.VMEM((t, _STEP_KEYS), F32),
                            pltpu.VMEM((t, LANES), F32), pltpu.VMEM((t, 1), F32)],
        ),
        out_shape=jax.ShapeDtypeStruct((batch * t, DSA_W), F32),
        compiler_params=_params(("arbitrary", "arbitrary")),
        name="dsa_sample_attn",
    )(page_table, qb, scores, thr, need, flag, k_new, v_new,
      *([ck] * PAGES_PER_STEP), *([cv] * PAGES_PER_STEP))


MOE_ROWS = 256
ROUTE_TILE = 512
TOKEN_TILE = 256


def _split_bf16(a):
    hi = a.astype(BF16)
    lo = (a - hi.astype(F32)).astype(BF16)
    return hi, lo


def _router_kernel(x_ref, whi_ref, wlo_ref, b_ref, idx_ref, gate_ref, rank_ref, cnt_ref, carry_scr):
    i = pl.program_id(0)
    tm = x_ref.shape[0]

    @pl.when(i == 0)
    def _():
        carry_scr[...] = jnp.zeros(carry_scr.shape, F32)

    xhi, xlo = _split_bf16(x_ref[...])
    whi = whi_ref[...]
    logits = (jnp.dot(xhi, whi, preferred_element_type=F32)
              + jnp.dot(xlo, whi, preferred_element_type=F32)
              + jnp.dot(xhi, wlo_ref[...], preferred_element_type=F32)) + b_ref[...]
    lane = lax.broadcasted_iota(I32, (tm, N_EXPERTS), 1)
    slot = lax.broadcasted_iota(I32, (tm, TOP_K), 1)
    vals, idxs = [], []
    cur = logits
    for _ in range(TOP_K):
        m = jnp.max(cur, axis=-1, keepdims=True)
        ix = jnp.min(jnp.where(cur == m, lane, N_EXPERTS), axis=-1, keepdims=True)
        vals.append(m)
        idxs.append(ix)
        cur = jnp.where(lane == ix, -jnp.inf, cur)
    es = [jnp.exp(v - vals[0]) for v in vals]
    denom = es[0] + es[1] + es[2] + es[3]
    onehot = jnp.zeros((tm, N_EXPERTS), F32)
    for ix in idxs:
        onehot = onehot + jnp.where(lane == ix, 1.0, 0.0)
    earlier = (lax.broadcasted_iota(I32, (tm, tm), 1) < lax.broadcasted_iota(I32, (tm, tm), 0))
    excl = jnp.dot(jnp.where(earlier, 1.0, 0.0).astype(BF16), onehot.astype(BF16),
                   preferred_element_type=F32) + carry_scr[...]
    idx_out = jnp.zeros((tm, TOP_K), I32)
    gate_out = jnp.zeros((tm, TOP_K), F32)
    rank_out = jnp.zeros((tm, TOP_K), F32)
    for k in range(TOP_K):
        rk = jnp.sum(jnp.where(lane == idxs[k], excl, 0.0), axis=-1, keepdims=True)
        idx_out = jnp.where(slot == k, idxs[k], idx_out)
        gate_out = jnp.where(slot == k, es[k] / denom, gate_out)
        rank_out = jnp.where(slot == k, rk, rank_out)
    idx_ref[...] = idx_out
    gate_ref[...] = gate_out
    rank_ref[...] = rank_out.astype(I32)
    total = carry_scr[...] + jnp.sum(onehot, axis=0, keepdims=True)
    carry_scr[...] = total
    cnt_ref[...] = total


def _router(x, w_router, b_router):
    n = x.shape[0]
    tm = ROUTE_TILE if n % ROUTE_TILE == 0 else TOKEN_TILE
    whi, wlo = _split_bf16(w_router)
    b2 = b_router.reshape(1, N_EXPERTS)
    kspec = pl.BlockSpec((tm, TOP_K), lambda i: (i, 0))
    return pl.pallas_call(
        _router_kernel,
        grid=(n // tm,),
        in_specs=[_row_spec(tm, D_MODEL), _full_spec(whi), _full_spec(wlo), _full_spec(b2)],
        out_specs=[kspec, kspec, kspec, pl.BlockSpec((1, N_EXPERTS), lambda i: (0, 0))],
        out_shape=[jax.ShapeDtypeStruct((n, TOP_K), I32), jax.ShapeDtypeStruct((n, TOP_K), F32),
                   jax.ShapeDtypeStruct((n, TOP_K), I32), jax.ShapeDtypeStruct((1, N_EXPERTS), F32)],
        scratch_shapes=[pltpu.VMEM((1, N_EXPERTS), F32)],
        compiler_params=_params(("arbitrary",)),
        name="router",
    )(x, whi, wlo, b2)


def _row_copy(src, src_row, dst, dst_row, sem):
    return pltpu.make_async_copy(src.at[pl.ds(src_row, 1)], dst.at[pl.ds(dst_row, 1)], sem)


def _dispatch_kernel(dest_ref, x_ref, xs_in, xs_out, sem):
    del xs_in
    tm = x_ref.shape[0]

    def issue(r, c):
        for k in range(TOP_K):
            _row_copy(x_ref, r, xs_out, dest_ref[r * TOP_K + k], sem).start()
        return c

    lax.fori_loop(0, tm, issue, 0)

    def drain(r, c):
        for k in range(TOP_K):
            _row_copy(x_ref, r, xs_out, dest_ref[r * TOP_K + k], sem).wait()
        return c

    lax.fori_loop(0, tm, drain, 0)


def _dispatch(x, dest_flat, n_rows):
    n = x.shape[0]
    tm = TOKEN_TILE
    return pl.pallas_call(
        _dispatch_kernel,
        grid=(n // tm,),
        in_specs=[pl.BlockSpec((tm * TOP_K,), lambda i: (i,), memory_space=pltpu.SMEM),
                  _row_spec(tm, D_MODEL), pl.BlockSpec(memory_space=pl.ANY)],
        out_specs=pl.BlockSpec(memory_space=pl.ANY),
        out_shape=jax.ShapeDtypeStruct((n_rows, D_MODEL), F32),
        scratch_shapes=[pltpu.SemaphoreType.DMA(())],
        input_output_aliases={2: 0},
        compiler_params=_params(("arbitrary",)),
        name="moe_dispatch",
    )(dest_flat, x, jnp.zeros((n_rows, D_MODEL), F32))


def _expert_kernel(be_ref, na_ref, x_ref, w1_ref, b1_ref, w2_ref, b2_ref, o_ref, w1b_scr, w2b_scr):
    i = pl.program_id(0)
    active = i < na_ref[0]
    changed = jnp.logical_or(i == 0, be_ref[i] != be_ref[jnp.maximum(i - 1, 0)])

    @pl.when(jnp.logical_and(active, changed))
    def _():
        w1b_scr[...] = w1_ref[0].astype(BF16)
        w2b_scr[...] = w2_ref[0].astype(BF16)

    @pl.when(active)
    def _():
        h = jnp.dot(x_ref[...].astype(BF16), w1b_scr[...], preferred_element_type=F32) + b1_ref[0]
        gt = jnp.minimum(h[:, :D_FF], SWIGLU_LIMIT)
        up = jnp.clip(h[:, D_FF:], -SWIGLU_LIMIT, SWIGLU_LIMIT)
        act = (up + 1.0) * gt * jax.nn.sigmoid(SWIGLU_ALPHA * gt)
        o_ref[...] = jnp.dot(act.astype(BF16), w2b_scr[...], preferred_element_type=F32) + b2_ref[0]

    @pl.when(jnp.logical_not(active))
    def _():
        o_ref[...] = jnp.zeros(o_ref.shape, F32)


def _experts(xs, block_e, n_active, w1, b1, w2, b2):
    n_rows = xs.shape[0]
    nblk = n_rows // MOE_ROWS
    w1 = w1.reshape(-1, D_MODEL, 2 * D_FF)
    w2 = w2.reshape(-1, D_FF, D_MODEL)
    b1r = b1.reshape(-1, 1, 2 * D_FF)
    b2r = b2.reshape(-1, 1, D_MODEL)
    last = lambda i, na: jnp.minimum(i, na[0] - 1)
    return pl.pallas_call(
        _expert_kernel,
        grid_spec=pltpu.PrefetchScalarGridSpec(
            num_scalar_prefetch=2,
            grid=(nblk,),
            in_specs=[pl.BlockSpec((MOE_ROWS, D_MODEL), lambda i, be, na: (last(i, na), 0)),
                      pl.BlockSpec((1, D_MODEL, 2 * D_FF), lambda i, be, na: (be[i], 0, 0)),
                      pl.BlockSpec((1, 1, 2 * D_FF), lambda i, be, na: (be[i], 0, 0)),
                      pl.BlockSpec((1, D_FF, D_MODEL), lambda i, be, na: (be[i], 0, 0)),
                      pl.BlockSpec((1, 1, D_MODEL), lambda i, be, na: (be[i], 0, 0))],
            out_specs=pl.BlockSpec((MOE_ROWS, D_MODEL), lambda i, be, na: (i, 0)),
            scratch_shapes=[pltpu.VMEM((D_MODEL, 2 * D_FF), BF16), pltpu.VMEM((D_FF, D_MODEL), BF16)],
        ),
        out_shape=jax.ShapeDtypeStruct((n_rows, D_MODEL), F32),
        compiler_params=_params(("arbitrary",)),
        name="moe_experts",
    )(block_e, n_active, xs, w1, b1r, w2, b2r)


def _combine_kernel(dest_ref, x_ref, gate_ref, g_ref, b_ref, yb_hbm, o_ref, buf, sem):
    tm = x_ref.shape[0]

    def issue(r, c):
        for k in range(TOP_K):
            _row_copy(yb_hbm, dest_ref[r * TOP_K + k], buf.at[k], r, sem).start()
        return c

    lax.fori_loop(0, tm, issue, 0)

    def drain(r, c):
        for k in range(TOP_K):
            _row_copy(yb_hbm, dest_ref[r * TOP_K + k], buf.at[k], r, sem).wait()
        return c

    lax.fori_loop(0, tm, drain, 0)
    gate = gate_ref[...]
    y = gate[:, 0:1] * buf[0]
    for k in range(1, TOP_K):
        y = y + gate[:, k:k + 1] * buf[k]
    o_ref[...] = _layer_norm(DN_ALPHA * x_ref[...] + y, g_ref[...], b_ref[...])


def _combine(x, yb, dest_flat, gate, g, b):
    n = x.shape[0]
    tm = TOKEN_TILE
    g2, b2 = g.reshape(1, D_MODEL), b.reshape(1, D_MODEL)
    return pl.pallas_call(
        _combine_kernel,
        grid=(n // tm,),
        in_specs=[pl.BlockSpec((tm * TOP_K,), lambda i: (i,), memory_space=pltpu.SMEM),
                  _row_spec(tm, D_MODEL), pl.BlockSpec((tm, TOP_K), lambda i: (i, 0)),
                  _full_spec(g2), _full_spec(b2), pl.BlockSpec(memory_space=pl.ANY)],
        out_specs=_row_spec(tm, D_MODEL),
        out_shape=jax.ShapeDtypeStruct((n, D_MODEL), F32),
        scratch_shapes=[pltpu.VMEM((TOP_K, tm, D_MODEL), F32), pltpu.SemaphoreType.DMA(())],
        compiler_params=_params(("arbitrary",)),
        name="moe_combine",
    )(dest_flat, x, gate, g2, b2, yb)


def _moe_layer(x, layer, w_router, b_router, w1, b1, w2, b2, g, b):
    n = x.shape[0]
    idx, gate, rank, counts = _router(x, w_router, b_router)
    nblk = (n * TOP_K) // MOE_ROWS + N_EXPERTS
    cnt = counts[0].astype(I32)
    padded = (cnt + MOE_ROWS - 1) // MOE_ROWS * MOE_ROWS
    pad_end = jnp.cumsum(padded)
    pad_start = pad_end - padded
    dest = (pad_start[idx] + rank).reshape(-1)
    n_active = (pad_end[-1] // MOE_ROWS).astype(I32)
    blk = jnp.arange(nblk, dtype=I32)
    blk = jnp.minimum(blk, n_active - 1)
    block_e = jnp.minimum(jnp.searchsorted(pad_end, blk * MOE_ROWS, side="right"), N_EXPERTS - 1).astype(I32)
    block_e = block_e + layer * N_EXPERTS
    xs = _dispatch(x, dest, nblk * MOE_ROWS)
    yb = _experts(xs, block_e, n_active.reshape(1), w1, b1, w2, b2)
    return _combine(x, yb, dest, gate, g, b)


POOL_HALO = 16


def _pool_kernel(x_ref, halo_ref, w_ref, sc_ref, g_ref, b_ref, o_ref, ext_scr, *, tiles_per_seq, n_prev):
    tm = x_ref.shape[0]
    tile = pl.program_id(0) % tiles_per_seq
    x = x_ref[...]
    halo = halo_ref[...]
    if n_prev == 0:
        halo = jnp.where(tile == 0, 0.0, halo)
    ext_scr[0:POOL_HALO] = halo
    ext_scr[POOL_HALO:POOL_HALO + tm] = x
    pos = n_prev + tile * tm + lax.broadcasted_iota(I32, (tm, 1), 0)
    parts = []
    for g, w in enumerate(POOL_WINDOWS):
        cols = slice(g * POOL_GROUP, (g + 1) * POOL_GROUP)
        xg = x[:, cols]
        s = xg
        for d in range(1, w):
            s = s + ext_scr[POOL_HALO - d:POOL_HALO - d + tm, cols]
        cnt = jnp.minimum(pos + 1, w).astype(F32)
        dg = s / cnt - xg
        parts.append(jnp.dot(dg.astype(BF16), w_ref[g], preferred_element_type=F32))
    mix = jnp.concatenate(parts, axis=1) * sc_ref[...]
    o_ref[...] = _layer_norm(DN_ALPHA * x + mix, g_ref[...], b_ref[...])


def _pool(x, halo_src, pool_w, pool_scale, g, b, tm, tiles_per_seq, n_prev, halo_map):
    n = x.shape[0]
    w16 = pool_w.astype(BF16)
    sc = pool_scale.reshape(1, D_MODEL)
    g2, b2 = g.reshape(1, D_MODEL), b.reshape(1, D_MODEL)
    return pl.pallas_call(
        functools.partial(_pool_kernel, tiles_per_seq=tiles_per_seq, n_prev=n_prev),
        grid=(n // tm,),
        in_specs=[_row_spec(tm, D_MODEL), pl.BlockSpec((POOL_HALO, D_MODEL), halo_map),
                  _full_spec(w16), _full_spec(sc), _full_spec(g2), _full_spec(b2)],
        out_specs=_row_spec(tm, D_MODEL),
        out_shape=jax.ShapeDtypeStruct((n, D_MODEL), F32),
        scratch_shapes=[pltpu.VMEM((POOL_HALO + tm, D_MODEL), F32)],
        compiler_params=_params(("parallel",)),
        name="pool",
    )(x, halo_src, w16, sc, g2, b2)


PROJ_TILE = 512
DSA_Q_BLOCK = 256
DSA_K_BLOCK = 512
POOL_TILE = 512


def kernel(x_prompt, x_sample, cache_k, cache_v, cache_idx_k, page_table, state_gla, state_pool,
           w_in, gla_fg_w2, gla_fg_b, gla_norm_g, idx_kn_g, idx_kn_b, w_out,
           pool_w, pool_scale, ln_mix_g, ln_mix_b, ln_ffn_g, ln_ffn_b,
           moe_router_w, moe_router_b, moe_w1, moe_b1, moe_w2, moe_b2):
    bp, sp, _ = x_prompt.shape
    bs, ts, _ = x_sample.shape
    n_p, n_s = bp * sp, bs * ts
    xp = x_prompt.reshape(n_p, D_MODEL)
    xs = x_sample.reshape(n_s, D_MODEL)
    proj_w = (w_in, gla_fg_w2, gla_fg_b, idx_kn_g, idx_kn_b)

    qa, ka, va, ga, lf, qb, kb, vb, qi, ki, wi, kb16, vt16, ki16 = _project(xp, *proj_w, tm=PROJ_TILE)
    oa_p, gla_p = _gla(qa, ka, va, lf, ga, jnp.zeros((bp, GLA_HEADS, GLA_DK, GLA_DV), F32), gla_norm_g, bp, sp)
    ob_p = _dsa_prompt(qb, qi, wi, ki16, kb16, vt16, bp, sp, DSA_Q_BLOCK, DSA_K_BLOCK)
    x1p = _merge(xp, oa_p, ob_p, w_out, ln_mix_g[0], ln_mix_b[0], PROJ_TILE)
    k_p = kb.reshape(bp, sp, DSA_HEADS, DSA_DH)
    v_p = vb.reshape(bp, sp, DSA_HEADS, DSA_DH)
    kidx_p = ki.reshape(bp, sp, IDX_DIM)

    qa, ka, va, ga, lf, qb, kb, vb, qi, ki, wi, _, _, _ = _project(xs, *proj_w, tm=n_s)
    oa_s, gla_s = _gla(qa, ka, va, lf, ga, state_gla, gla_norm_g, bs, ts)
    ob_s = _dsa_sample(qb, qi, wi, ki, kb, vb, cache_k, cache_v, cache_idx_k, page_table, bs, ts)
    x1s = _merge(xs, oa_s, ob_s, w_out, ln_mix_g[0], ln_mix_b[0], n_s)
    k_s = kb.reshape(bs, ts, DSA_HEADS, DSA_DH)
    v_s = vb.reshape(bs, ts, DSA_HEADS, DSA_DH)
    kidx_s = ki.reshape(bs, ts, IDX_DIM)

    x_all = _moe_layer(jnp.concatenate([x1p, x1s], axis=0), 0, moe_router_w[0], moe_router_b[0],
                       moe_w1, moe_b1, moe_w2, moe_b2, ln_ffn_g[0], ln_ffn_b[0])
    xp, xs = x_all[:n_p], x_all[n_p:]

    pool_p = xp.reshape(bp, sp, D_MODEL)[:, -POOL_STATE:]
    xs3 = xs.reshape(bs, ts, D_MODEL)
    pool_s = jnp.concatenate([state_pool, xs3], axis=1)[:, -POOL_STATE:]
    per_seq = sp // POOL_TILE
    halo_step = POOL_TILE // POOL_HALO
    x2p = _pool(xp, xp, pool_w, pool_scale, ln_mix_g[1], ln_mix_b[1], POOL_TILE, per_seq, 0,
                lambda i: (jnp.maximum(i * halo_step - 1, 0), 0))
    halo_s = jnp.concatenate([jnp.zeros((bs, POOL_HALO - POOL_STATE, D_MODEL), F32), state_pool], axis=1)
    x2s = _pool(xs, halo_s.reshape(bs * POOL_HALO, D_MODEL), pool_w, pool_scale, ln_mix_g[1], ln_mix_b[1],
                ts, 1, POOL_STATE, lambda i: (i, 0))
    x_all = _moe_layer(jnp.concatenate([x2p, x2s], axis=0), 1, moe_router_w[1], moe_router_b[1],
                       moe_w1, moe_b1, moe_w2, moe_b2, ln_ffn_g[1], ln_ffn_b[1])
    y_p = x_all[:n_p].reshape(bp, sp, D_MODEL)
    y_s = x_all[n_p:].reshape(bs, ts, D_MODEL)
    return (y_p, y_s, k_p, v_p, kidx_p, gla_p, pool_p, k_s, v_s, kidx_s, gla_s, pool_s)
```

```python
import functools
import math

import jax
import jax.numpy as jnp
from jax import lax
from jax.experimental import pallas as pl
from jax.experimental.pallas import tpu as pltpu

F32 = jnp.float32
BF16 = jnp.bfloat16
I32 = jnp.int32

D_MODEL = 1024
DEPTH = 2
PAGE_SIZE = 128
GLA_HEADS = 4
GLA_DK = 64
GLA_DV = 128
GLA_GATE_RANK = 16
GLA_TAU = 16.0
GLA_CHUNK = 64
DSA_HEADS = 8
DSA_DH = 64
IDX_HEADS = 4
IDX_DIM = 64
DSA_TOPK = 256
IDX_W_SCALE = (IDX_HEADS ** -0.5) * (IDX_DIM ** -0.5)
POOL_WINDOWS = (2, 4, 8, 16)
POOL_GROUP = D_MODEL // 4
POOL_STATE = 16 - 1
N_EXPERTS = 32
TOP_K = 4
D_FF = D_MODEL
SWIGLU_ALPHA = 1.702
SWIGLU_LIMIT = 7.0
DN_ALPHA = (2 * DEPTH) ** 0.25
LN_EPS = 1e-5
GLA_QK = GLA_HEADS * GLA_DK
GLA_V = GLA_HEADS * GLA_DV
DSA_W = DSA_HEADS * DSA_DH
IDX_W = IDX_HEADS * IDX_DIM

LANES = 128
SUBLANES = 8
VMEM_LIMIT_BYTES = 56 * 1024 * 1024

NEG_BIG = -1e30


def _params(sem, vmem=VMEM_LIMIT_BYTES):
    return pltpu.CompilerParams(dimension_semantics=sem, vmem_limit_bytes=vmem)


def _layer_norm(x, g, b):
    mu = jnp.mean(x, axis=-1, keepdims=True)
    xc = x - mu
    var = jnp.mean(xc * xc, axis=-1, keepdims=True)
    return xc * lax.rsqrt(var + LN_EPS) * g + b


def _row_spec(tm, w):
    return pl.BlockSpec((tm, w), lambda i: (i, 0))


def _full_spec(a):
    nd = a.ndim
    return pl.BlockSpec(a.shape, lambda i: (0,) * nd)


_MAIN_W = (GLA_QK, GLA_QK, GLA_V, GLA_V, DSA_W, DSA_W, DSA_W, IDX_W)
_MAIN_OFF = tuple(sum(_MAIN_W[:i]) for i in range(len(_MAIN_W) + 1))


def _proj_kernel(x_ref, wm_ref, ws_ref, fgw_ref, fgb_ref, kng_ref, knb_ref,
                 qa_ref, ka_ref, va_ref, ga_ref, lf_ref, qb_ref, kb_ref, vb_ref,
                 qi_ref, ki_ref, wi_ref, kb16_ref, vt16_ref, ki16_ref):
    xb = x_ref[...].astype(BF16)

    def mm(n):
        return jnp.dot(xb, wm_ref[:, _MAIN_OFF[n]:_MAIN_OFF[n + 1]], preferred_element_type=F32)

    qa_ref[...] = mm(0) * (GLA_DK ** -0.5)
    ka_ref[...] = mm(1)
    va_ref[...] = mm(2)
    ga_ref[...] = mm(3)
    qb_ref[...] = mm(4)
    kb = mm(5)
    kb_ref[...] = kb
    kb16_ref[...] = kb.astype(BF16)
    vb = mm(6)
    vb_ref[...] = vb
    vt16_ref[...] = vb.T.astype(BF16)
    qi_ref[...] = mm(7)

    small = jnp.dot(xb, ws_ref[...], preferred_element_type=F32)
    ki = _layer_norm(small[:, :IDX_DIM], kng_ref[...], knb_ref[...])
    ki_ref[...] = ki
    ki16_ref[...] = ki.astype(BF16)
    fa = small[:, IDX_DIM:IDX_DIM + GLA_GATE_RANK]
    z = jnp.dot(fa, fgw_ref[...], preferred_element_type=F32,
                precision=lax.Precision.HIGHEST) + fgb_ref[...]
    lf_ref[...] = (jnp.minimum(z, 0.0) - jnp.log(1.0 + jnp.exp(-jnp.abs(z)))) * (1.0 / GLA_TAU)
    wi_ref[...] = small * IDX_W_SCALE


def _project(x2d, w_in, gla_fg_w2, gla_fg_b, idx_kn_g, idx_kn_b, tm):
    n = x2d.shape[0]
    pts = [0]
    for s in (GLA_QK, GLA_QK, GLA_V, GLA_V, GLA_GATE_RANK, DSA_W, DSA_W, DSA_W, IDX_W, IDX_DIM, IDX_HEADS):
        pts.append(pts[-1] + s)
    seg = lambda i: w_in[:, pts[i]:pts[i + 1]]
    wm = jnp.concatenate([seg(0), seg(1), seg(2), seg(3), seg(5), seg(6), seg(7), seg(8)], axis=1).astype(BF16)
    pad = LANES - IDX_DIM - GLA_GATE_RANK - IDX_HEADS
    ws = jnp.concatenate([seg(9), seg(4), seg(10), jnp.zeros((D_MODEL, pad), w_in.dtype)], axis=1).astype(BF16)
    fgb = gla_fg_b.reshape(1, GLA_QK)
    kng = idx_kn_g.reshape(1, IDX_DIM)
    knb = idx_kn_b.reshape(1, IDX_DIM)
    widths = (GLA_QK, GLA_QK, GLA_V, GLA_V, GLA_QK, DSA_W, DSA_W, DSA_W, IDX_W, IDX_DIM, LANES)
    out_shape = [jax.ShapeDtypeStruct((n, w), F32) for w in widths]
    out_shape += [jax.ShapeDtypeStruct((n, DSA_W), BF16), jax.ShapeDtypeStruct((DSA_W, n), BF16),
                  jax.ShapeDtypeStruct((n, IDX_DIM), BF16)]
    out_specs = [_row_spec(tm, w) for w in widths]
    out_specs += [_row_spec(tm, DSA_W), pl.BlockSpec((DSA_W, tm), lambda i: (0, i)), _row_spec(tm, IDX_DIM)]
    return pl.pallas_call(
        _proj_kernel,
        grid=(n // tm,),
        in_specs=[_row_spec(tm, D_MODEL), _full_spec(wm), _full_spec(ws), _full_spec(gla_fg_w2),
                  _full_spec(fgb), _full_spec(kng), _full_spec(knb)],
        out_specs=out_specs,
        out_shape=out_shape,
        compiler_params=_params(("parallel",)),
        name="proj",
    )(x2d, wm, ws, gla_fg_w2, fgb, kng, knb)


def _gla_kernel(q_ref, k_ref, v_ref, lf_ref, ga_ref, s0_ref, ng_ref, o_ref, sout_ref, s_scr, *, chunk):
    c = pl.program_id(1)
    nc = pl.num_programs(1)

    @pl.when(c == 0)
    def _():
        s_scr[...] = s0_ref[0]

    g = lf_ref[...]
    ri = lax.broadcasted_iota(I32, (chunk, chunk), 0)
    ci = lax.broadcasted_iota(I32, (chunk, chunk), 1)
    causal = ci <= ri
    tri = jnp.where(causal, 1.0, 0.0).astype(F32)
    b = jnp.dot(tri, g, preferred_element_type=F32, precision=lax.Precision.HIGHEST)
    mid = chunk // 2
    b_mid = b[mid:mid + 1, :]
    b_last = b[chunk - 1:chunk, :]
    q = q_ref[...]
    k = k_ref[...]
    q_in = q * jnp.exp(b)
    q_rel = q * jnp.exp(b - b_mid)
    k_rel = k * jnp.exp(b_mid - b)
    k_out = k * jnp.exp(b_last - b)
    dec_last = jnp.exp(b_last)
    v = v_ref[...]
    ga = ga_ref[...]
    ng = ng_ref[...]
    eye = jnp.where(lax.broadcasted_iota(I32, (GLA_DK, GLA_DK), 0)
                    == lax.broadcasted_iota(I32, (GLA_DK, GLA_DK), 1), 1.0, 0.0).astype(F32)
    for h in range(GLA_HEADS):
        ks = slice(h * GLA_DK, (h + 1) * GLA_DK)
        vs = slice(h * GLA_DV, (h + 1) * GLA_DV)
        s_h = s_scr[h]
        vh = v[:, vs]
        inter = jnp.dot(q_in[:, ks], s_h, preferred_element_type=F32)
        att = lax.dot_general(q_rel[:, ks], k_rel[:, ks], (((1,), (1,)), ((), ())),
                              preferred_element_type=F32)
        att = jnp.where(causal, att, 0.0)
        o = inter + jnp.dot(att, vh, preferred_element_type=F32)
        kv = lax.dot_general(k_out[:, ks], vh, (((0,), (0,)), ((), ())), preferred_element_type=F32)
        s_scr[h] = jnp.dot(eye * dec_last[:, ks], s_h, preferred_element_type=F32,
                           precision=lax.Precision.HIGHEST) + kv
        ms = jnp.mean(o * o, axis=-1, keepdims=True)
        gh = ga[:, vs]
        o_ref[:, vs] = o * lax.rsqrt(ms + LN_EPS) * ng * (gh * jax.nn.sigmoid(gh))

    @pl.when(c == nc - 1)
    def _():
        sout_ref[0] = s_scr[...]


def _gla(qa, ka, va, lf, ga, s0, gla_norm_g, batch, seq):
    chunk = math.gcd(seq, GLA_CHUNK)
    nc = seq // chunk
    ng = gla_norm_g.reshape(1, GLA_DV)
    spec = lambda w: pl.BlockSpec((chunk, w), lambda b, c: (b * nc + c, 0))
    sspec = pl.BlockSpec((1, GLA_HEADS, GLA_DK, GLA_DV), lambda b, c: (b, 0, 0, 0))
    return pl.pallas_call(
        functools.partial(_gla_kernel, chunk=chunk),
        grid=(batch, nc),
        in_specs=[spec(GLA_QK), spec(GLA_QK), spec(GLA_V), spec(GLA_QK), spec(GLA_V), sspec,
                  pl.BlockSpec((1, GLA_DV), lambda b, c: (0, 0))],
        out_specs=[spec(GLA_V), sspec],
        out_shape=[jax.ShapeDtypeStruct((batch * seq, GLA_V), F32),
                   jax.ShapeDtypeStruct((batch, GLA_HEADS, GLA_DK, GLA_DV), F32)],
        scratch_shapes=[pltpu.VMEM((GLA_HEADS, GLA_DK, GLA_DV), F32)],
        compiler_params=_params(("parallel", "arbitrary")),
        name="gla",
    )(qa, ka, va, lf, ga, s0, ng)


def _merge_kernel(x_ref, oa_ref, ob_ref, w_ref, g_ref, b_ref, o_ref):
    mix = jnp.dot(oa_ref[...].astype(BF16), w_ref[:GLA_V, :], preferred_element_type=F32)
    mix = mix + jnp.dot(ob_ref[...].astype(BF16), w_ref[GLA_V:, :], preferred_element_type=F32)
    o_ref[...] = _layer_norm(DN_ALPHA * x_ref[...] + mix, g_ref[...], b_ref[...])


def _merge(x2d, oa, ob, w_out, g, b, tm):
    n = x2d.shape[0]
    w16 = w_out.astype(BF16)
    g2, b2 = g.reshape(1, D_MODEL), b.reshape(1, D_MODEL)
    return pl.pallas_call(
        _merge_kernel,
        grid=(n // tm,),
        in_specs=[_row_spec(tm, D_MODEL), _row_spec(tm, GLA_V), _row_spec(tm, DSA_W),
                  _full_spec(w16), _full_spec(g2), _full_spec(b2)],
        out_specs=_row_spec(tm, D_MODEL),
        out_shape=jax.ShapeDtypeStruct((n, D_MODEL), F32),
        compiler_params=_params(("parallel",)),
        name="merge",
    )(x2d, oa, ob, w16, g2, b2)


_KEY_NEG_INF = -2139095041
_KEY_POS_INF = 2139095040
_WI_LANE = IDX_DIM + GLA_GATE_RANK


def _unkey(kk):
    return lax.bitcast_convert_type(jnp.where(kk < 0, kk ^ 0x7FFFFFFF, kk), F32)


def _key(v):
    i = lax.bitcast_convert_type(v, I32)
    return jnp.where(i < 0, i ^ 0x7FFFFFFF, i)


_REDUCE_CHAINS = 8


def _reduce_rows(x, reduce_fn):
    r, q = x.shape
    groups = r // SUBLANES
    chains = _REDUCE_CHAINS if groups % _REDUCE_CHAINS == 0 else 1
    y = x.reshape(chains, groups // chains, SUBLANES, q)
    return reduce_fn(reduce_fn(y, axis=1), axis=0)


def _fold_keys(m, key_axis):
    if key_axis == 1:
        out = m[:, :LANES]
        for t in range(1, m.shape[1] // LANES):
            out = out + m[:, t * LANES:(t + 1) * LANES]
        return out
    return _reduce_rows(m, jnp.sum)


_VALUE_BISECTIONS = 12
_KEY_BITS = 32


def _kth_largest_threshold(read_block, n_blocks, rows, k, active, key_axis=1, bounds=None):
    kf = float(k)
    qshape = (rows, 1) if key_axis == 1 else (1, rows)
    part = (rows, LANES) if key_axis == 1 else (SUBLANES, rows)
    value_steps = _VALUE_BISECTIONS if bounds is not None else 0
    max_steps = value_steps + _KEY_BITS + 2

    def count(preds):
        def body(c, accs):
            s = read_block(c)
            return tuple(a + _fold_keys(jnp.where(p(s), 1.0, 0.0), key_axis) for a, p in zip(accs, preds))
        accs = lax.fori_loop(0, n_blocks, body, tuple(jnp.zeros(part, F32) for _ in preds))
        return [jnp.sum(a, axis=key_axis, keepdims=True) for a in accs]

    def cond(st):
        return st[4] > 0

    def body(st):
        it, lo, hi, cnt_lo, _ = st
        mid = (lo >> 1) + (hi >> 1) + (lo & hi & 1)
        if value_steps:
            vmid = _key(0.5 * _unkey(lo) + 0.5 * _unkey(hi))
            vmid = jnp.minimum(jnp.maximum(vmid, lo + 1), jnp.maximum(hi - 1, lo + 1))
            mid = jnp.where(it < value_steps, vmid, mid)
        cand = _unkey(mid)
        cnt, = count([lambda s: s >= cand])
        ge = cnt >= kf
        lo = jnp.where(ge, mid, lo)
        cnt_lo = jnp.where(ge, cnt, cnt_lo)
        hi = jnp.where(ge, hi, mid)
        done = (cnt_lo == kf) | (hi == lo + 1) | jnp.logical_not(active)
        go = jnp.max(jnp.where(done, 0, 1)).astype(I32)
        go = jnp.where(it < max_steps, go, 0)
        return it + 1, lo, hi, cnt_lo, go

    if bounds is None:
        lo0 = jnp.full(qshape, _KEY_NEG_INF, I32)
        hi0 = jnp.full(qshape, _KEY_POS_INF, I32)
        cnt0 = jnp.full(qshape, -1.0, F32)
        open0 = active
    else:
        n_gt0, n_ge0 = count([lambda s: s > 0.0, lambda s: s >= 0.0])
        positive = n_gt0 >= kf
        non_negative = n_ge0 >= kf
        key_zero = _key(jnp.zeros(qshape, F32))
        lo0 = jnp.where(non_negative, key_zero, _key(bounds[0]))
        cnt0 = jnp.where(non_negative, n_ge0, -1.0)
        hi0 = jnp.where(positive, _key(bounds[1]) + 1, jnp.where(non_negative, key_zero + 1, key_zero))
        lo0 = jnp.where(active, lo0, _KEY_NEG_INF)
        hi0 = jnp.where(active, hi0, _KEY_POS_INF)
        open0 = jnp.logical_and(active, jnp.logical_not((cnt0 == kf) | (hi0 == lo0 + 1)))
    go0 = jnp.max(jnp.where(open0, 1, 0)).astype(I32)
    _, lo, _, _, _ = lax.while_loop(cond, body, (jnp.int32(0), lo0, hi0, cnt0, go0))
    thr = jnp.where(active, _unkey(lo), -jnp.inf)
    n_gt, n_ge = count([lambda s: s > thr, lambda s: s >= thr])
    need = jnp.where(active, kf - n_gt, 0.0)
    ties = jnp.logical_and(active, (n_ge - n_gt) > need)
    tie_flag = jnp.max(jnp.where(ties, 1, 0)).astype(I32)
    return thr, need, tie_flag


def _selection_bias(s, thr, need, tie_flag, tie_cnt_ref, bias_ref, key_axis=1):
    w = s.shape[key_axis]

    @pl.when(tie_flag == 0)
    def _():
        sel = jnp.logical_and(s >= thr, s > -jnp.inf)
        bias_ref[...] = jnp.where(sel, 0.0, NEG_BIG)

    @pl.when(tie_flag != 0)
    def _():
        eq = s == thr
        eqf = jnp.where(eq, 1.0, 0.0)
        r_i = lax.broadcasted_iota(I32, (w, w), 0)
        c_i = lax.broadcasted_iota(I32, (w, w), 1)
        if key_axis == 1:
            earlier = jnp.where(r_i < c_i, 1.0, 0.0).astype(BF16)
            rank = jnp.dot(eqf.astype(BF16), earlier, preferred_element_type=F32)
        else:
            earlier = jnp.where(c_i < r_i, 1.0, 0.0).astype(BF16)
            rank = jnp.dot(earlier, eqf.astype(BF16), preferred_element_type=F32)
        rank = rank + tie_cnt_ref[...]
        sel = jnp.logical_or(s > thr, jnp.logical_and(eq, rank < need))
        bias_ref[...] = jnp.where(sel, 0.0, NEG_BIG)
        tie_cnt_ref[...] = tie_cnt_ref[...] + jnp.sum(eqf, axis=key_axis, keepdims=True)


def _dsa_prompt_kernel(qb_ref, qi_ref, wi_ref, ki_ref, k_hbm, vt_hbm, o_ref,
                       k_scr, vt_scr, sc_scr, qm_scr, m_scr, l_scr, acc_scr, bias_scr, tie_scr, sem,
                       *, q_blk, k_blk, seq, n_sel):
    b = pl.program_id(0)
    i = pl.program_id(1)

    @pl.when(i == 0)
    def _():
        ck = pltpu.make_async_copy(k_hbm.at[pl.ds(b * seq, seq)], k_scr, sem.at[0])
        cv = pltpu.make_async_copy(vt_hbm.at[:, pl.ds(b * seq, seq)], vt_scr, sem.at[1])
        ck.start()
        cv.start()
        ck.wait()
        cv.wait()

    q0 = i * q_blk
    n_blocks = (q0 + q_blk + k_blk - 1) // k_blk
    q_pos = q0 + lax.broadcasted_iota(I32, (1, q_blk), 1)

    qit = qi_ref[...].T.astype(BF16)
    qit_h = [qit[h * IDX_DIM:(h + 1) * IDX_DIM] for h in range(IDX_HEADS)]
    wit = wi_ref[...].T
    w_h = [wit[_WI_LANE + h:_WI_LANE + h + 1] for h in range(IDX_HEADS)]

    def score_body(c, carry):
        off = pl.multiple_of(c * k_blk, k_blk)
        kc = ki_ref[pl.ds(off, k_blk), :]
        s = w_h[0] * jnp.maximum(jnp.dot(kc, qit_h[0], preferred_element_type=F32), 0.0)
        for h in range(1, IDX_HEADS):
            s = s + w_h[h] * jnp.maximum(jnp.dot(kc, qit_h[h], preferred_element_type=F32), 0.0)
        k_pos = off + lax.broadcasted_iota(I32, (k_blk, 1), 0)
        adm = k_pos <= q_pos
        sc_scr[pl.ds(off, k_blk), :] = jnp.where(adm, s, -jnp.inf)
        hi_part, lo_part = carry
        hi_part = jnp.maximum(hi_part, _reduce_rows(jnp.where(adm, s, -jnp.inf), jnp.max))
        lo_part = jnp.minimum(lo_part, _reduce_rows(jnp.where(adm, s, jnp.inf), jnp.min))
        return hi_part, lo_part

    hi_part, lo_part = lax.fori_loop(
        0, n_blocks, score_body,
        (jnp.full((SUBLANES, q_blk), -jnp.inf, F32), jnp.full((SUBLANES, q_blk), jnp.inf, F32)))
    s_max = jnp.max(hi_part, axis=0, keepdims=True)
    s_min = jnp.min(lo_part, axis=0, keepdims=True)

    def read_block(c):
        return sc_scr[pl.ds(pl.multiple_of(c * k_blk, k_blk), k_blk), :]

    active = (q_pos + 1) > n_sel
    thr, need, tie_flag = _kth_largest_threshold(read_block, n_blocks, q_blk, n_sel, active, key_axis=0,
                                                 bounds=(s_min, s_max))

    qt = (qb_ref[...] * (DSA_DH ** -0.5)).T
    row_lo = lax.broadcasted_iota(I32, (LANES, 1), 0) < DSA_DH
    for p in range(DSA_HEADS // 2):
        qp = qt[p * LANES:(p + 1) * LANES]
        qm_scr[2 * p] = jnp.where(row_lo, qp, 0.0).astype(BF16)
        qm_scr[2 * p + 1] = jnp.where(row_lo, 0.0, qp).astype(BF16)
    m_scr[...] = jnp.full(m_scr.shape, NEG_BIG, F32)
    l_scr[...] = jnp.zeros(l_scr.shape, F32)
    acc_scr[...] = jnp.zeros(acc_scr.shape, F32)
    tie_scr[...] = jnp.zeros(tie_scr.shape, F32)

    def att_body(j, carry):
        off = pl.multiple_of(j * k_blk, k_blk)
        _selection_bias(sc_scr[pl.ds(off, k_blk), :], thr, need, tie_flag, tie_scr, bias_scr, key_axis=0)
        bias = bias_scr[...]

        def qk(h):
            p = h // 2
            kp = k_scr[pl.ds(off, k_blk), p * LANES:(p + 1) * LANES]
            return jnp.dot(kp, qm_scr[h], preferred_element_type=F32)

        prs, alphas = [], []
        for h in range(DSA_HEADS):
            logit = qk(h) + bias
            m_old = m_scr[h]
            m_new = jnp.maximum(m_old, jnp.max(_reduce_rows(logit, jnp.max), axis=0, keepdims=True))
            alpha = jnp.exp(m_old - m_new)
            pe = jnp.exp(logit - m_new)
            l_scr[h] = alpha * l_scr[h] + jnp.sum(_reduce_rows(pe, jnp.sum), axis=0, keepdims=True)
            m_scr[h] = m_new
            prs.append(pe.astype(BF16))
            alphas.append(alpha)
        for h in range(DSA_HEADS):
            rows = slice(h * DSA_DH, (h + 1) * DSA_DH)
            vth = vt_scr[rows, pl.ds(off, k_blk)]
            acc_scr[rows, :] = alphas[h] * acc_scr[rows, :] + jnp.dot(vth, prs[h], preferred_element_type=F32)
        return carry

    lax.fori_loop(0, n_blocks, att_body, 0)

    for h in range(DSA_HEADS):
        rows = slice(h * DSA_DH, (h + 1) * DSA_DH)
        acc_scr[rows, :] = acc_scr[rows, :] / l_scr[h]
    o_ref[...] = acc_scr[...].T


def _dsa_prompt(qb, qi, wi, ki16, k16, vt16, batch, seq, q_blk, k_blk):
    n_sel = min(DSA_TOPK, seq // 4)
    nq = seq // q_blk
    qspec = lambda w: pl.BlockSpec((q_blk, w), lambda b, i: (b * nq + i, 0))
    return pl.pallas_call(
        functools.partial(_dsa_prompt_kernel, q_blk=q_blk, k_blk=k_blk, seq=seq, n_sel=n_sel),
        grid=(batch, nq),
        in_specs=[qspec(DSA_W), qspec(IDX_W), qspec(LANES),
                  pl.BlockSpec((seq, IDX_DIM), lambda b, i: (b, 0)),
                  pl.BlockSpec(memory_space=pl.ANY), pl.BlockSpec(memory_space=pl.ANY)],
        out_specs=qspec(DSA_W),
        out_shape=jax.ShapeDtypeStruct((batch * seq, DSA_W), F32),
        scratch_shapes=[pltpu.VMEM((seq, DSA_W), BF16), pltpu.VMEM((DSA_W, seq), BF16),
                        pltpu.VMEM((seq, q_blk), F32),
                        pltpu.VMEM((DSA_HEADS, LANES, q_blk), BF16),
                        pltpu.VMEM((DSA_HEADS, 1, q_blk), F32), pltpu.VMEM((DSA_HEADS, 1, q_blk), F32),
                        pltpu.VMEM((DSA_W, q_blk), F32), pltpu.VMEM((k_blk, q_blk), F32),
                        pltpu.VMEM((1, q_blk), F32), pltpu.SemaphoreType.DMA((2,))],
        compiler_params=_params(("arbitrary", "arbitrary")),
        name="dsa_prompt",
    )(qb, qi, wi, ki16, k16, vt16)


PAGES_PER_STEP = 8
SCORE_PAGES_PER_STEP = 32
_STEP_KEYS = PAGES_PER_STEP * PAGE_SIZE
_NT = (((1,), (1,)), ((), ()))


def _page_specs(page_shape, n_pages, per_step, clamp_last):
    specs = []
    zeros = (0,) * len(page_shape)
    for s in range(per_step):
        def imap(b, c, pt, s=s):
            page = jnp.minimum(c * per_step + s, n_pages - per_step + s) if clamp_last else c * per_step + s
            return (pt[b, page],) + zeros
        specs.append(pl.BlockSpec((1,) + page_shape, imap))
    return specs


def _stack_heads(qi):
    return jnp.concatenate([qi[:, h * IDX_DIM:(h + 1) * IDX_DIM] for h in range(IDX_HEADS)], axis=0)


def _idx_score(s4, wi, t):
    s = wi[:, _WI_LANE:_WI_LANE + 1] * jnp.maximum(s4[:t], 0.0)
    for h in range(1, IDX_HEADS):
        s = s + wi[:, _WI_LANE + h:_WI_LANE + h + 1] * jnp.maximum(s4[h * t:(h + 1) * t], 0.0)
    return s


def _dsa_sample_score_kernel(pt_ref, qi_ref, wi_ref, kin_ref, *rest, t, past, n_sel):
    pages = rest[:SCORE_PAGES_PER_STEP]
    sc_ref, thr_ref, need_ref, flag_ref = rest[SCORE_PAGES_PER_STEP:]
    c = pl.program_id(1)
    nch = pl.num_programs(1)
    step_keys = SCORE_PAGES_PER_STEP * PAGE_SIZE
    qi4 = _stack_heads(qi_ref[...]).astype(BF16)
    wi = wi_ref[...]
    keys_t = jnp.concatenate([p[0] for p in pages], axis=1).astype(BF16)
    off = pl.multiple_of(c * step_keys, step_keys)
    sc_ref[0, :, pl.ds(off, step_keys)] = _idx_score(
        jnp.dot(qi4, keys_t, preferred_element_type=F32), wi, t)

    @pl.when(c == nch - 1)
    def _():
        knew = jnp.concatenate([kin_ref[...], jnp.zeros((LANES - t, IDX_DIM), F32)], axis=0).astype(BF16)
        s = _idx_score(lax.dot_general(qi4, knew, _NT, preferred_element_type=F32), wi, t)
        col = lax.broadcasted_iota(I32, (t, LANES), 1)
        row = lax.broadcasted_iota(I32, (t, LANES), 0)
        sc_ref[0, :, past:past + LANES] = jnp.where(col <= row, s, -jnp.inf)
        active = (past + 1 + lax.broadcasted_iota(I32, (t, 1), 0)) > n_sel
        thr, need, flag = _kth_largest_threshold(lambda _: sc_ref[0], 1, t, n_sel, active)
        thr_ref[0] = thr
        need_ref[0] = need
        flag_ref[0] = jnp.zeros((t, 1), F32) + flag.astype(F32)


def _dsa_sample_attn_kernel(pt_ref, qb_ref, sc_ref, thr_ref, need_ref, flag_ref, kn_ref, vn_ref, *rest,
                            t, past):
    kpages = rest[:PAGES_PER_STEP]
    vpages = rest[PAGES_PER_STEP:2 * PAGES_PER_STEP]
    o_ref, m_scr, l_scr, acc_scr, bias_scr, biasn_scr, tie_scr = rest[2 * PAGES_PER_STEP:]
    c = pl.program_id(1)
    nch = pl.num_programs(1) - 1

    @pl.when(c == 0)
    def _():
        m_scr[...] = jnp.full(m_scr.shape, NEG_BIG, F32)
        l_scr[...] = jnp.zeros(l_scr.shape, F32)
        acc_scr[...] = jnp.zeros(acc_scr.shape, F32)
        tie_scr[...] = jnp.zeros(tie_scr.shape, F32)

    rows = DSA_HEADS * t
    blockmask = (lax.broadcasted_iota(I32, (rows, DSA_W), 0) // t
                 == lax.broadcasted_iota(I32, (rows, DSA_W), 1) // DSA_DH)
    qs = qb_ref[...] * (DSA_DH ** -0.5)
    qbd = jnp.where(blockmask, jnp.concatenate([qs] * DSA_HEADS, axis=0), 0.0).astype(BF16)
    thr = thr_ref[0]
    need = need_ref[0]
    flag = jnp.max(flag_ref[0]).astype(I32)

    def update(bias, k_op, v_op, keys_on_lanes):
        if keys_on_lanes:
            logit = jnp.dot(qbd, k_op, preferred_element_type=F32)
        else:
            logit = lax.dot_general(qbd, k_op, _NT, preferred_element_type=F32)
        logit = logit + jnp.concatenate([bias] * DSA_HEADS, axis=0)
        m_old = m_scr[...]
        m_new = jnp.maximum(m_old, jnp.max(logit, axis=-1, keepdims=True))
        alpha = jnp.exp(m_old - m_new)
        pr = jnp.exp(logit - m_new).astype(BF16)
        l_scr[...] = alpha * l_scr[...] + jnp.sum(pr.astype(F32), axis=-1, keepdims=True)
        m_scr[...] = m_new
        if keys_on_lanes:
            pv = lax.dot_general(pr, v_op, _NT, preferred_element_type=F32)
        else:
            pv = jnp.dot(pr, v_op, preferred_element_type=F32)
        acc_scr[...] = alpha * acc_scr[...] + pv

    @pl.when(c < nch)
    def _():
        off = pl.multiple_of(c * _STEP_KEYS, _STEP_KEYS)
        _selection_bias(sc_ref[0, :, pl.ds(off, _STEP_KEYS)], thr, need, flag, tie_scr, bias_scr)
        kc = jnp.concatenate([p[0].reshape(DSA_W, PAGE_SIZE) for p in kpages], axis=1).astype(BF16)
        vc = jnp.concatenate([p[0].reshape(DSA_W, PAGE_SIZE) for p in vpages], axis=1).astype(BF16)
        update(bias_scr[...], kc, vc, True)

    @pl.when(c == nch)
    def _():
        _selection_bias(sc_ref[0, :, past:past + LANES], thr, need, flag, tie_scr, biasn_scr)
        zpad = jnp.zeros((LANES - t, DSA_W), F32)
        kc = jnp.concatenate([kn_ref[...], zpad], axis=0).astype(BF16)
        vc = jnp.concatenate([vn_ref[...], zpad], axis=0).astype(BF16)
        update(biasn_scr[...], kc, vc, False)
        outn = jnp.where(blockmask, acc_scr[...] / l_scr[...], 0.0)
        out = outn[:t]
        for h in range(1, DSA_HEADS):
            out = out + outn[h * t:(h + 1) * t]
        o_ref[...] = out


def _dsa_sample(qb, qi, wi, ki_new, k_new, v_new, cache_k, cache_v, cache_idx_k, page_table, batch, t):
    n_pages = page_table.shape[1]
    past = n_pages * PAGE_SIZE
    n_sel = min(DSA_TOPK, (past + t) // 4)
    nch = n_pages // PAGES_PER_STEP
    nch_score = n_pages // SCORE_PAGES_PER_STEP
    lp = past + LANES
    kv_page = (DSA_HEADS, DSA_DH, PAGE_SIZE)
    ck = jnp.transpose(cache_k, (0, 2, 3, 1))
    cv = jnp.transpose(cache_v, (0, 2, 3, 1))
    cik = jnp.transpose(cache_idx_k, (0, 2, 1))
    rspec = lambda w: pl.BlockSpec((t, w), lambda b, c, pt: (b, 0))
    bspec = lambda w: pl.BlockSpec((1, t, w), lambda b, c, pt: (b, 0, 0))
    scores, thr, need, flag = pl.pallas_call(
        functools.partial(_dsa_sample_score_kernel, t=t, past=past, n_sel=n_sel),
        grid_spec=pltpu.PrefetchScalarGridSpec(
            num_scalar_prefetch=1,
            grid=(batch, nch_score),
            in_specs=[rspec(IDX_W), rspec(LANES), rspec(IDX_DIM)]
            + _page_specs((IDX_DIM, PAGE_SIZE), n_pages, SCORE_PAGES_PER_STEP, False),
            out_specs=[bspec(lp), bspec(1), bspec(1), bspec(1)],
        ),
        out_shape=[jax.ShapeDtypeStruct((batch, t, lp), F32)] + [jax.ShapeDtypeStruct((batch, t, 1), F32)] * 3,
        compiler_params=_params(("arbitrary", "arbitrary")),
        name="dsa_sample_score",
    )(page_table, qi, wi, ki_new, *([cik] * SCORE_PAGES_PER_STEP))
    return pl.pallas_call(
        functools.partial(_dsa_sample_attn_kernel, t=t, past=past),
        grid_spec=pltpu.PrefetchScalarGridSpec(
            num_scalar_prefetch=1,
            grid=(batch, nch + 1),
            in_specs=[rspec(DSA_W), bspec(lp), bspec(1), bspec(1), bspec(1), rspec(DSA_W), rspec(DSA_W)]
            + _page_specs(kv_page, n_pages, PAGES_PER_STEP, True)
            + _page_specs(kv_page, n_pages, PAGES_PER_STEP, True),
            out_specs=rspec(DSA_W),
            scratch_shapes=[pltpu.VMEM((DSA_HEADS * t, 1), F32), pltpu.VMEM((DSA_HEADS * t, 1), F32),
                            pltpu.VMEM((DSA_HEADS * t, DSA_W), F32), pltpu.VMEM((t, _STEP_KEYS), F32),
                            pltpu.VMEM((t, LANES), F32), pltpu.VMEM((t, 1), F32)],
        ),
        out_shape=jax.ShapeDtypeStruct((batch * t, DSA_W), F32),
        compiler_params=_params(("arbitrary", "arbitrary")),
        name="dsa_sample_attn",
    )(page_table, qb, scores, thr, need, flag, k_new, v_new,
      *([ck] * PAGES_PER_STEP), *([cv] * PAGES_PER_STEP))


MOE_ROWS = 256
ROUTE_TILE = 512
TOKEN_TILE = 256


def _split_bf16(a):
    hi = a.astype(BF16)
    lo = (a - hi.astype(F32)).astype(BF16)
    return hi, lo


def _router_kernel(x_ref, whi_ref, wlo_ref, b_ref, idx_ref, gate_ref, rank_ref, cnt_ref, carry_scr):
    i = pl.program_id(0)
    tm = x_ref.shape[0]

    @pl.when(i == 0)
    def _():
        carry_scr[...] = jnp.zeros(carry_scr.shape, F32)

    xhi, xlo = _split_bf16(x_ref[...])
    whi = whi_ref[...]
    logits = (jnp.dot(xhi, whi, preferred_element_type=F32)
              + jnp.dot(xlo, whi, preferred_element_type=F32)
              + jnp.dot(xhi, wlo_ref[...], preferred_element_type=F32)) + b_ref[...]
    lane = lax.broadcasted_iota(I32, (tm, N_EXPERTS), 1)
    slot = lax.broadcasted_iota(I32, (tm, TOP_K), 1)
    vals, idxs = [], []
    cur = logits
    for _ in range(TOP_K):
        m = jnp.max(cur, axis=-1, keepdims=True)
        ix = jnp.min(jnp.where(cur == m, lane, N_EXPERTS), axis=-1, keepdims=True)
        vals.append(m)
        idxs.append(ix)
        cur = jnp.where(lane == ix, -jnp.inf, cur)
    es = [jnp.exp(v - vals[0]) for v in vals]
    denom = es[0] + es[1] + es[2] + es[3]
    onehot = jnp.zeros((tm, N_EXPERTS), F32)
    for ix in idxs:
        onehot = onehot + jnp.where(lane == ix, 1.0, 0.0)
    earlier = (lax.broadcasted_iota(I32, (tm, tm), 1) < lax.broadcasted_iota(I32, (tm, tm), 0))
    excl = jnp.dot(jnp.where(earlier, 1.0, 0.0).astype(BF16), onehot.astype(BF16),
                   preferred_element_type=F32) + carry_scr[...]
    idx_out = jnp.zeros((tm, TOP_K), I32)
    gate_out = jnp.zeros((tm, TOP_K), F32)
    rank_out = jnp.zeros((tm, TOP_K), F32)
    for k in range(TOP_K):
        rk = jnp.sum(jnp.where(lane == idxs[k], excl, 0.0), axis=-1, keepdims=True)
        idx_out = jnp.where(slot == k, idxs[k], idx_out)
        gate_out = jnp.where(slot == k, es[k] / denom, gate_out)
        rank_out = jnp.where(slot == k, rk, rank_out)
    idx_ref[...] = idx_out
    gate_ref[...] = gate_out
    rank_ref[...] = rank_out.astype(I32)
    total = carry_scr[...] + jnp.sum(onehot, axis=0, keepdims=True)
    carry_scr[...] = total
    cnt_ref[...] = total


def _router(x, w_router, b_router):
    n = x.shape[0]
    tm = ROUTE_TILE if n % ROUTE_TILE == 0 else TOKEN_TILE
    whi, wlo = _split_bf16(w_router)
    b2 = b_router.reshape(1, N_EXPERTS)
    kspec = pl.BlockSpec((tm, TOP_K), lambda i: (i, 0))
    return pl.pallas_call(
        _router_kernel,
        grid=(n // tm,),
        in_specs=[_row_spec(tm, D_MODEL), _full_spec(whi), _full_spec(wlo), _full_spec(b2)],
        out_specs=[kspec, kspec, kspec, pl.BlockSpec((1, N_EXPERTS), lambda i: (0, 0))],
        out_shape=[jax.ShapeDtypeStruct((n, TOP_K), I32), jax.ShapeDtypeStruct((n, TOP_K), F32),
                   jax.ShapeDtypeStruct((n, TOP_K), I32), jax.ShapeDtypeStruct((1, N_EXPERTS), F32)],
        scratch_shapes=[pltpu.VMEM((1, N_EXPERTS), F32)],
        compiler_params=_params(("arbitrary",)),
        name="router",
    )(x, whi, wlo, b2)


def _row_copy(src, src_row, dst, dst_row, sem):
    return pltpu.make_async_copy(src.at[pl.ds(src_row, 1)], dst.at[pl.ds(dst_row, 1)], sem)


def _dispatch_kernel(dest_ref, x_ref, xs_in, xs_out, sem):
    del xs_in
    tm = x_ref.shape[0]

    def issue(r, c):
        for k in range(TOP_K):
            _row_copy(x_ref, r, xs_out, dest_ref[r * TOP_K + k], sem).start()
        return c

    lax.fori_loop(0, tm, issue, 0)
    for _ in range(TOP_K):
        pltpu.make_async_copy(x_ref, x_ref, sem).wait()


def _dispatch(x, dest_flat, n_rows):
    n = x.shape[0]
    tm = TOKEN_TILE
    return pl.pallas_call(
        _dispatch_kernel,
        grid=(n // tm,),
        in_specs=[pl.BlockSpec((tm * TOP_K,), lambda i: (i,), memory_space=pltpu.SMEM),
                  _row_spec(tm, D_MODEL), pl.BlockSpec(memory_space=pl.ANY)],
        out_specs=pl.BlockSpec(memory_space=pl.ANY),
        out_shape=jax.ShapeDtypeStruct((n_rows, D_MODEL), F32),
        scratch_shapes=[pltpu.SemaphoreType.DMA(())],
        input_output_aliases={2: 0},
        compiler_params=_params(("arbitrary",)),
        name="moe_dispatch",
    )(dest_flat, x, jnp.zeros((n_rows, D_MODEL), F32))


def _expert_kernel(be_ref, na_ref, x_ref, w1_ref, b1_ref, w2_ref, b2_ref, o_ref, w1b_scr, w2b_scr):
    i = pl.program_id(0)
    active = i < na_ref[0]
    changed = jnp.logical_or(i == 0, be_ref[i] != be_ref[jnp.maximum(i - 1, 0)])

    @pl.when(jnp.logical_and(active, changed))
    def _():
        w1b_scr[...] = w1_ref[0].astype(BF16)
        w2b_scr[...] = w2_ref[0].astype(BF16)

    @pl.when(active)
    def _():
        h = jnp.dot(x_ref[...].astype(BF16), w1b_scr[...], preferred_element_type=F32) + b1_ref[0]
        gt = jnp.minimum(h[:, :D_FF], SWIGLU_LIMIT)
        up = jnp.clip(h[:, D_FF:], -SWIGLU_LIMIT, SWIGLU_LIMIT)
        act = (up + 1.0) * gt * jax.nn.sigmoid(SWIGLU_ALPHA * gt)
        o_ref[...] = jnp.dot(act.astype(BF16), w2b_scr[...], preferred_element_type=F32) + b2_ref[0]

    @pl.when(jnp.logical_not(active))
    def _():
        o_ref[...] = jnp.zeros(o_ref.shape, F32)


def _experts(xs, block_e, n_active, w1, b1, w2, b2):
    n_rows = xs.shape[0]
    nblk = n_rows // MOE_ROWS
    w1 = w1.reshape(-1, D_MODEL, 2 * D_FF)
    w2 = w2.reshape(-1, D_FF, D_MODEL)
    b1r = b1.reshape(-1, 1, 2 * D_FF)
    b2r = b2.reshape(-1, 1, D_MODEL)
    last = lambda i, na: jnp.minimum(i, na[0] - 1)
    return pl.pallas_call(
        _expert_kernel,
        grid_spec=pltpu.PrefetchScalarGridSpec(
            num_scalar_prefetch=2,
            grid=(nblk,),
            in_specs=[pl.BlockSpec((MOE_ROWS, D_MODEL), lambda i, be, na: (last(i, na), 0)),
                      pl.BlockSpec((1, D_MODEL, 2 * D_FF), lambda i, be, na: (be[i], 0, 0)),
                      pl.BlockSpec((1, 1, 2 * D_FF), lambda i, be, na: (be[i], 0, 0)),
                      pl.BlockSpec((1, D_FF, D_MODEL), lambda i, be, na: (be[i], 0, 0)),
                      pl.BlockSpec((1, 1, D_MODEL), lambda i, be, na: (be[i], 0, 0))],
            out_specs=pl.BlockSpec((MOE_ROWS, D_MODEL), lambda i, be, na: (i, 0)),
            scratch_shapes=[pltpu.VMEM((D_MODEL, 2 * D_FF), BF16), pltpu.VMEM((D_FF, D_MODEL), BF16)],
        ),
        out_shape=jax.ShapeDtypeStruct((n_rows, D_MODEL), F32),
        compiler_params=_params(("arbitrary",)),
        name="moe_experts",
    )(block_e, n_active, xs, w1, b1r, w2, b2r)


def _combine_kernel(dest_ref, x_ref, gate_ref, g_ref, b_ref, yb_hbm, o_ref, buf, sem):
    tm = x_ref.shape[0]

    def issue(r, c):
        for k in range(TOP_K):
            _row_copy(yb_hbm, dest_ref[r * TOP_K + k], buf.at[k], r, sem).start()
        return c

    lax.fori_loop(0, tm, issue, 0)
    for k in range(TOP_K):
        pltpu.make_async_copy(buf.at[k], buf.at[k], sem).wait()
    gate = gate_ref[...]
    y = gate[:, 0:1] * buf[0]
    for k in range(1, TOP_K):
        y = y + gate[:, k:k + 1] * buf[k]
    o_ref[...] = _layer_norm(DN_ALPHA * x_ref[...] + y, g_ref[...], b_ref[...])


def _combine(x, yb, dest_flat, gate, g, b):
    n = x.shape[0]
    tm = TOKEN_TILE
    g2, b2 = g.reshape(1, D_MODEL), b.reshape(1, D_MODEL)
    return pl.pallas_call(
        _combine_kernel,
        grid=(n // tm,),
        in_specs=[pl.BlockSpec((tm * TOP_K,), lambda i: (i,), memory_space=pltpu.SMEM),
                  _row_spec(tm, D_MODEL), pl.BlockSpec((tm, TOP_K), lambda i: (i, 0)),
                  _full_spec(g2), _full_spec(b2), pl.BlockSpec(memory_space=pl.ANY)],
        out_specs=_row_spec(tm, D_MODEL),
        out_shape=jax.ShapeDtypeStruct((n, D_MODEL), F32),
        scratch_shapes=[pltpu.VMEM((TOP_K, tm, D_MODEL), F32), pltpu.SemaphoreType.DMA(())],
        compiler_params=_params(("arbitrary",)),
        name="moe_combine",
    )(dest_flat, x, gate, g2, b2, yb)


def _moe_layer(x, layer, w_router, b_router, w1, b1, w2, b2, g, b):
    n = x.shape[0]
    idx, gate, rank, counts = _router(x, w_router, b_router)
    nblk = (n * TOP_K) // MOE_ROWS + N_EXPERTS
    cnt = counts[0].astype(I32)
    padded = (cnt + MOE_ROWS - 1) // MOE_ROWS * MOE_ROWS
    pad_end = jnp.cumsum(padded)
    pad_start = pad_end - padded
    dest = (pad_start[idx] + rank).reshape(-1)
    n_active = (pad_end[-1] // MOE_ROWS).astype(I32)
    blk = jnp.arange(nblk, dtype=I32)
    blk = jnp.minimum(blk, n_active - 1)
    block_e = jnp.sum((pad_end[None, :] <= (blk * MOE_ROWS)[:, None]).astype(I32), axis=1)
    block_e = jnp.minimum(block_e, N_EXPERTS - 1) + layer * N_EXPERTS
    xs = _dispatch(x, dest, nblk * MOE_ROWS)
    yb = _experts(xs, block_e, n_active.reshape(1), w1, b1, w2, b2)
    return _combine(x, yb, dest, gate, g, b)


POOL_HALO = 16


def _pool_kernel(x_ref, halo_ref, w_ref, sc_ref, g_ref, b_ref, o_ref, ext_scr, *, tiles_per_seq, n_prev):
    tm = x_ref.shape[0]
    tile = pl.program_id(0) % tiles_per_seq
    x = x_ref[...]
    halo = halo_ref[...]
    if n_prev == 0:
        halo = jnp.where(tile == 0, 0.0, halo)
    ext_scr[0:POOL_HALO] = halo
    ext_scr[POOL_HALO:POOL_HALO + tm] = x
    pos = n_prev + tile * tm + lax.broadcasted_iota(I32, (tm, 1), 0)
    parts = []
    for g, w in enumerate(POOL_WINDOWS):
        cols = slice(g * POOL_GROUP, (g + 1) * POOL_GROUP)
        xg = x[:, cols]
        s = xg
        for d in range(1, w):
            s = s + ext_scr[POOL_HALO - d:POOL_HALO - d + tm, cols]
        cnt = jnp.minimum(pos + 1, w).astype(F32)
        dg = s / cnt - xg
        parts.append(jnp.dot(dg.astype(BF16), w_ref[g], preferred_element_type=F32))
    mix = jnp.concatenate(parts, axis=1) * sc_ref[...]
    o_ref[...] = _layer_norm(DN_ALPHA * x + mix, g_ref[...], b_ref[...])


def _pool(x, halo_src, pool_w, pool_scale, g, b, tm, tiles_per_seq, n_prev, halo_map):
    n = x.shape[0]
    w16 = pool_w.astype(BF16)
    sc = pool_scale.reshape(1, D_MODEL)
    g2, b2 = g.reshape(1, D_MODEL), b.reshape(1, D_MODEL)
    return pl.pallas_call(
        functools.partial(_pool_kernel, tiles_per_seq=tiles_per_seq, n_prev=n_prev),
        grid=(n // tm,),
        in_specs=[_row_spec(tm, D_MODEL), pl.BlockSpec((POOL_HALO, D_MODEL), halo_map),
                  _full_spec(w16), _full_spec(sc), _full_spec(g2), _full_spec(b2)],
        out_specs=_row_spec(tm, D_MODEL),
        out_shape=jax.ShapeDtypeStruct((n, D_MODEL), F32),
        scratch_shapes=[pltpu.VMEM((POOL_HALO + tm, D_MODEL), F32)],
        compiler_params=_params(("parallel",)),
        name="pool",
    )(x, halo_src, w16, sc, g2, b2)


PROJ_TILE = 512
DSA_Q_BLOCK = 256
DSA_K_BLOCK = 512
POOL_TILE = 512


def kernel(x_prompt, x_sample, cache_k, cache_v, cache_idx_k, page_table, state_gla, state_pool,
           w_in, gla_fg_w2, gla_fg_b, gla_norm_g, idx_kn_g, idx_kn_b, w_out,
           pool_w, pool_scale, ln_mix_g, ln_mix_b, ln_ffn_g, ln_ffn_b,
           moe_router_w, moe_router_b, moe_w1, moe_b1, moe_w2, moe_b2):
    bp, sp, _ = x_prompt.shape
    bs, ts, _ = x_sample.shape
    n_p, n_s = bp * sp, bs * ts
    xp = x_prompt.reshape(n_p, D_MODEL)
    xs = x_sample.reshape(n_s, D_MODEL)
    proj_w = (w_in, gla_fg_w2, gla_fg_b, idx_kn_g, idx_kn_b)

    qa, ka, va, ga, lf, qb, kb, vb, qi, ki, wi, kb16, vt16, ki16 = _project(xp, *proj_w, tm=PROJ_TILE)
    oa_p, gla_p = _gla(qa, ka, va, lf, ga, jnp.zeros((bp, GLA_HEADS, GLA_DK, GLA_DV), F32), gla_norm_g, bp, sp)
    ob_p = _dsa_prompt(qb, qi, wi, ki16, kb16, vt16, bp, sp, DSA_Q_BLOCK, DSA_K_BLOCK)
    x1p = _merge(xp, oa_p, ob_p, w_out, ln_mix_g[0], ln_mix_b[0], PROJ_TILE)
    k_p = kb.reshape(bp, sp, DSA_HEADS, DSA_DH)
    v_p = vb.reshape(bp, sp, DSA_HEADS, DSA_DH)
    kidx_p = ki.reshape(bp, sp, IDX_DIM)

    qa, ka, va, ga, lf, qb, kb, vb, qi, ki, wi, _, _, _ = _project(xs, *proj_w, tm=n_s)
    oa_s, gla_s = _gla(qa, ka, va, lf, ga, state_gla, gla_norm_g, bs, ts)
    ob_s = _dsa_sample(qb, qi, wi, ki, kb, vb, cache_k, cache_v, cache_idx_k, page_table, bs, ts)
    x1s = _merge(xs, oa_s, ob_s, w_out, ln_mix_g[0], ln_mix_b[0], n_s)
    k_s = kb.reshape(bs, ts, DSA_HEADS, DSA_DH)
    v_s = vb.reshape(bs, ts, DSA_HEADS, DSA_DH)
    kidx_s = ki.reshape(bs, ts, IDX_DIM)

    x_all = _moe_layer(jnp.concatenate([x1p, x1s], axis=0), 0, moe_router_w[0], moe_router_b[0],
                       moe_w1, moe_b1, moe_w2, moe_b2, ln_ffn_g[0], ln_ffn_b[0])
    xp, xs = x_all[:n_p], x_all[n_p:]

    pool_p = xp.reshape(bp, sp, D_MODEL)[:, -POOL_STATE:]
    xs3 = xs.reshape(bs, ts, D_MODEL)
    pool_s = jnp.concatenate([state_pool, xs3], axis=1)[:, -POOL_STATE:]
    per_seq = sp // POOL_TILE
    halo_step = POOL_TILE // POOL_HALO
    x2p = _pool(xp, xp, pool_w, pool_scale, ln_mix_g[1], ln_mix_b[1], POOL_TILE, per_seq, 0,
                lambda i: (jnp.maximum(i * halo_step - 1, 0), 0))
    halo_s = jnp.concatenate([jnp.zeros((bs, POOL_HALO - POOL_STATE, D_MODEL), F32), state_pool], axis=1)
    x2s = _pool(xs, halo_s.reshape(bs * POOL_HALO, D_MODEL), pool_w, pool_scale, ln_mix_g[1], ln_mix_b[1],
                ts, 1, POOL_STATE, lambda i: (i, 0))
    x_all = _moe_layer(jnp.concatenate([x2p, x2s], axis=0), 1, moe_router_w[1], moe_router_b[1],
                       moe_w1, moe_b1, moe_w2, moe_b2, ln_ffn_g[1], ln_ffn_b[1])
    y_p = x_all[:n_p].reshape(bp, sp, D_MODEL)
    y_s = x_all[n_p:].reshape(bs, ts, D_MODEL)
    return (y_p, y_s, k_p, v_p, kidx_p, gla_p, pool_p, k_s, v_s, kidx_s, gla_s, pool_s)
```

```python
import functools
import math

import jax
import jax.numpy as jnp
from jax import lax
from jax.experimental import pallas as pl
from jax.experimental.pallas import tpu as pltpu

F32 = jnp.float32
BF16 = jnp.bfloat16
I32 = jnp.int32

D_MODEL = 1024
DEPTH = 2
PAGE_SIZE = 128
GLA_HEADS = 4
GLA_DK = 64
GLA_DV = 128
GLA_GATE_RANK = 16
GLA_TAU = 16.0
GLA_CHUNK = 64
DSA_HEADS = 8
DSA_DH = 64
IDX_HEADS = 4
IDX_DIM = 64
DSA_TOPK = 256
IDX_W_SCALE = (IDX_HEADS ** -0.5) * (IDX_DIM ** -0.5)
POOL_WINDOWS = (2, 4, 8, 16)
POOL_GROUP = D_MODEL // 4
POOL_STATE = 16 - 1
N_EXPERTS = 32
TOP_K = 4
D_FF = D_MODEL
SWIGLU_ALPHA = 1.702
SWIGLU_LIMIT = 7.0
DN_ALPHA = (2 * DEPTH) ** 0.25
LN_EPS = 1e-5
GLA_QK = GLA_HEADS * GLA_DK
GLA_V = GLA_HEADS * GLA_DV
DSA_W = DSA_HEADS * DSA_DH
IDX_W = IDX_HEADS * IDX_DIM

LANES = 128
SUBLANES = 8
VMEM_LIMIT_BYTES = 56 * 1024 * 1024

NEG_BIG = -1e30


def _params(sem, vmem=VMEM_LIMIT_BYTES):
    return pltpu.CompilerParams(dimension_semantics=sem, vmem_limit_bytes=vmem)


def _layer_norm(x, g, b):
    mu = jnp.mean(x, axis=-1, keepdims=True)
    xc = x - mu
    var = jnp.mean(xc * xc, axis=-1, keepdims=True)
    return xc * lax.rsqrt(var + LN_EPS) * g + b


def _row_spec(tm, w):
    return pl.BlockSpec((tm, w), lambda i: (i, 0))


def _full_spec(a):
    nd = a.ndim
    return pl.BlockSpec(a.shape, lambda i: (0,) * nd)


_MAIN_W = (GLA_QK, GLA_QK, GLA_V, GLA_V, DSA_W, DSA_W, DSA_W, IDX_W)
_MAIN_OFF = tuple(sum(_MAIN_W[:i]) for i in range(len(_MAIN_W) + 1))


def _proj_kernel(x_ref, wm_ref, ws_ref, fgw_ref, fgb_ref, kng_ref, knb_ref,
                 qa_ref, ka_ref, va_ref, ga_ref, lf_ref, qb_ref, kb_ref, vb_ref,
                 qi_ref, ki_ref, wi_ref, kb16_ref, vt16_ref, ki16_ref):
    xb = x_ref[...].astype(BF16)

    def mm(n):
        return jnp.dot(xb, wm_ref[:, _MAIN_OFF[n]:_MAIN_OFF[n + 1]], preferred_element_type=F32)

    qa_ref[...] = mm(0) * (GLA_DK ** -0.5)
    ka_ref[...] = mm(1)
    va_ref[...] = mm(2)
    ga_ref[...] = mm(3)
    qb_ref[...] = mm(4)
    kb = mm(5)
    kb_ref[...] = kb
    kb16_ref[...] = kb.astype(BF16)
    vb = mm(6)
    vb_ref[...] = vb
    vt16_ref[...] = vb.T.astype(BF16)
    qi_ref[...] = mm(7)

    small = jnp.dot(xb, ws_ref[...], preferred_element_type=F32)
    ki = _layer_norm(small[:, :IDX_DIM], kng_ref[...], knb_ref[...])
    ki_ref[...] = ki
    ki16_ref[...] = ki.astype(BF16)
    fa = small[:, IDX_DIM:IDX_DIM + GLA_GATE_RANK]
    z = jnp.dot(fa, fgw_ref[...], preferred_element_type=F32,
                precision=lax.Precision.HIGHEST) + fgb_ref[...]
    lf_ref[...] = (jnp.minimum(z, 0.0) - jnp.log(1.0 + jnp.exp(-jnp.abs(z)))) * (1.0 / GLA_TAU)
    wi_ref[...] = small * IDX_W_SCALE


def _project(x2d, w_in, gla_fg_w2, gla_fg_b, idx_kn_g, idx_kn_b, tm):
    n = x2d.shape[0]
    pts = [0]
    for s in (GLA_QK, GLA_QK, GLA_V, GLA_V, GLA_GATE_RANK, DSA_W, DSA_W, DSA_W, IDX_W, IDX_DIM, IDX_HEADS):
        pts.append(pts[-1] + s)
    seg = lambda i: w_in[:, pts[i]:pts[i + 1]]
    wm = jnp.concatenate([seg(0), seg(1), seg(2), seg(3), seg(5), seg(6), seg(7), seg(8)], axis=1).astype(BF16)
    pad = LANES - IDX_DIM - GLA_GATE_RANK - IDX_HEADS
    ws = jnp.concatenate([seg(9), seg(4), seg(10), jnp.zeros((D_MODEL, pad), w_in.dtype)], axis=1).astype(BF16)
    fgb = gla_fg_b.reshape(1, GLA_QK)
    kng = idx_kn_g.reshape(1, IDX_DIM)
    knb = idx_kn_b.reshape(1, IDX_DIM)
    widths = (GLA_QK, GLA_QK, GLA_V, GLA_V, GLA_QK, DSA_W, DSA_W, DSA_W, IDX_W, IDX_DIM, LANES)
    out_shape = [jax.ShapeDtypeStruct((n, w), F32) for w in widths]
    out_shape += [jax.ShapeDtypeStruct((n, DSA_W), BF16), jax.ShapeDtypeStruct((DSA_W, n), BF16),
                  jax.ShapeDtypeStruct((n, IDX_DIM), BF16)]
    out_specs = [_row_spec(tm, w) for w in widths]
    out_specs += [_row_spec(tm, DSA_W), pl.BlockSpec((DSA_W, tm), lambda i: (0, i)), _row_spec(tm, IDX_DIM)]
    return pl.pallas_call(
        _proj_kernel,
        grid=(n // tm,),
        in_specs=[_row_spec(tm, D_MODEL), _full_spec(wm), _full_spec(ws), _full_spec(gla_fg_w2),
                  _full_spec(fgb), _full_spec(kng), _full_spec(knb)],
        out_specs=out_specs,
        out_shape=out_shape,
        compiler_params=_params(("parallel",)),
        name="proj",
    )(x2d, wm, ws, gla_fg_w2, fgb, kng, knb)


def _gla_kernel(q_ref, k_ref, v_ref, lf_ref, ga_ref, s0_ref, ng_ref, o_ref, sout_ref, s_scr, *, chunk):
    c = pl.program_id(1)
    nc = pl.num_programs(1)

    @pl.when(c == 0)
    def _():
        s_scr[...] = s0_ref[0]

    g = lf_ref[...]
    ri = lax.broadcasted_iota(I32, (chunk, chunk), 0)
    ci = lax.broadcasted_iota(I32, (chunk, chunk), 1)
    causal = ci <= ri
    tri = jnp.where(causal, 1.0, 0.0).astype(F32)
    b = jnp.dot(tri, g, preferred_element_type=F32, precision=lax.Precision.HIGHEST)
    mid = chunk // 2
    b_mid = b[mid:mid + 1, :]
    b_last = b[chunk - 1:chunk, :]
    q = q_ref[...]
    k = k_ref[...]
    q_in = q * jnp.exp(b)
    q_rel = q * jnp.exp(b - b_mid)
    k_rel = k * jnp.exp(b_mid - b)
    k_out = k * jnp.exp(b_last - b)
    dec_last = jnp.exp(b_last)
    v = v_ref[...]
    ga = ga_ref[...]
    ng = ng_ref[...]
    eye = jnp.where(lax.broadcasted_iota(I32, (GLA_DK, GLA_DK), 0)
                    == lax.broadcasted_iota(I32, (GLA_DK, GLA_DK), 1), 1.0, 0.0).astype(F32)
    for h in range(GLA_HEADS):
        ks = slice(h * GLA_DK, (h + 1) * GLA_DK)
        vs = slice(h * GLA_DV, (h + 1) * GLA_DV)
        s_h = s_scr[h]
        vh = v[:, vs]
        inter = jnp.dot(q_in[:, ks], s_h, preferred_element_type=F32)
        att = lax.dot_general(q_rel[:, ks], k_rel[:, ks], (((1,), (1,)), ((), ())),
                              preferred_element_type=F32)
        att = jnp.where(causal, att, 0.0)
        o = inter + jnp.dot(att, vh, preferred_element_type=F32)
        kv = lax.dot_general(k_out[:, ks], vh, (((0,), (0,)), ((), ())), preferred_element_type=F32)
        s_scr[h] = jnp.dot(eye * dec_last[:, ks], s_h, preferred_element_type=F32,
                           precision=lax.Precision.HIGHEST) + kv
        ms = jnp.mean(o * o, axis=-1, keepdims=True)
        gh = ga[:, vs]
        o_ref[:, vs] = o * lax.rsqrt(ms + LN_EPS) * ng * (gh * jax.nn.sigmoid(gh))

    @pl.when(c == nc - 1)
    def _():
        sout_ref[0] = s_scr[...]


def _gla(qa, ka, va, lf, ga, s0, gla_norm_g, batch, seq):
    chunk = math.gcd(seq, GLA_CHUNK)
    nc = seq // chunk
    ng = gla_norm_g.reshape(1, GLA_DV)
    spec = lambda w: pl.BlockSpec((chunk, w), lambda b, c: (b * nc + c, 0))
    sspec = pl.BlockSpec((1, GLA_HEADS, GLA_DK, GLA_DV), lambda b, c: (b, 0, 0, 0))
    return pl.pallas_call(
        functools.partial(_gla_kernel, chunk=chunk),
        grid=(batch, nc),
        in_specs=[spec(GLA_QK), spec(GLA_QK), spec(GLA_V), spec(GLA_QK), spec(GLA_V), sspec,
                  pl.BlockSpec((1, GLA_DV), lambda b, c: (0, 0))],
        out_specs=[spec(GLA_V), sspec],
        out_shape=[jax.ShapeDtypeStruct((batch * seq, GLA_V), F32),
                   jax.ShapeDtypeStruct((batch, GLA_HEADS, GLA_DK, GLA_DV), F32)],
        scratch_shapes=[pltpu.VMEM((GLA_HEADS, GLA_DK, GLA_DV), F32)],
        compiler_params=_params(("parallel", "arbitrary")),
        name="gla",
    )(qa, ka, va, lf, ga, s0, ng)


def _merge_kernel(x_ref, oa_ref, ob_ref, w_ref, g_ref, b_ref, o_ref):
    mix = jnp.dot(oa_ref[...].astype(BF16), w_ref[:GLA_V, :], preferred_element_type=F32)
    mix = mix + jnp.dot(ob_ref[...].astype(BF16), w_ref[GLA_V:, :], preferred_element_type=F32)
    o_ref[...] = _layer_norm(DN_ALPHA * x_ref[...] + mix, g_ref[...], b_ref[...])


def _merge(x2d, oa, ob, w_out, g, b, tm):
    n = x2d.shape[0]
    w16 = w_out.astype(BF16)
    g2, b2 = g.reshape(1, D_MODEL), b.reshape(1, D_MODEL)
    return pl.pallas_call(
        _merge_kernel,
        grid=(n // tm,),
        in_specs=[_row_spec(tm, D_MODEL), _row_spec(tm, GLA_V), _row_spec(tm, DSA_W),
                  _full_spec(w16), _full_spec(g2), _full_spec(b2)],
        out_specs=_row_spec(tm, D_MODEL),
        out_shape=jax.ShapeDtypeStruct((n, D_MODEL), F32),
        compiler_params=_params(("parallel",)),
        name="merge",
    )(x2d, oa, ob, w16, g2, b2)


_KEY_NEG_INF = -2139095041
_KEY_POS_INF = 2139095040
_WI_LANE = IDX_DIM + GLA_GATE_RANK


def _unkey(kk):
    return lax.bitcast_convert_type(jnp.where(kk < 0, kk ^ 0x7FFFFFFF, kk), F32)


def _key(v):
    i = lax.bitcast_convert_type(v, I32)
    return jnp.where(i < 0, i ^ 0x7FFFFFFF, i)


_REDUCE_CHAINS = 8


def _reduce_rows(x, reduce_fn):
    r, q = x.shape
    groups = r // SUBLANES
    chains = _REDUCE_CHAINS if groups % _REDUCE_CHAINS == 0 else 1
    y = x.reshape(chains, groups // chains, SUBLANES, q)
    return reduce_fn(reduce_fn(y, axis=1), axis=0)


def _fold_keys(m, key_axis):
    if key_axis == 1:
        out = m[:, :LANES]
        for t in range(1, m.shape[1] // LANES):
            out = out + m[:, t * LANES:(t + 1) * LANES]
        return out
    return _reduce_rows(m, jnp.sum)


_VALUE_BISECTIONS = 12
_KEY_BITS = 32


def _kth_largest_threshold(read_block, n_blocks, rows, k, active, key_axis=1, bounds=None):
    kf = float(k)
    qshape = (rows, 1) if key_axis == 1 else (1, rows)
    part = (rows, LANES) if key_axis == 1 else (SUBLANES, rows)
    value_steps = _VALUE_BISECTIONS if bounds is not None else 0
    max_steps = value_steps + _KEY_BITS + 2

    def count(preds):
        def body(c, accs):
            s = read_block(c)
            return tuple(a + _fold_keys(jnp.where(p(s), 1.0, 0.0), key_axis) for a, p in zip(accs, preds))
        accs = lax.fori_loop(0, n_blocks, body, tuple(jnp.zeros(part, F32) for _ in preds))
        return [jnp.sum(a, axis=key_axis, keepdims=True) for a in accs]

    def cond(st):
        return st[5] > 0

    def body(st):
        it, lo, hi, cnt_lo, open_, _ = st
        mid = (lo >> 1) + (hi >> 1) + (lo & hi & 1)
        if value_steps:
            vmid = _key(0.5 * _unkey(lo) + 0.5 * _unkey(hi))
            vmid = jnp.minimum(jnp.maximum(vmid, lo + 1), jnp.maximum(hi - 1, lo + 1))
            mid = jnp.where(jnp.logical_and(it < value_steps, positive), vmid, mid)
        cand = _unkey(mid)
        cnt, = count([lambda s: s >= cand])
        ge = jnp.logical_and(open_ > 0, cnt >= kf)
        lt = jnp.logical_and(open_ > 0, cnt < kf)
        lo = jnp.where(ge, mid, lo)
        cnt_lo = jnp.where(ge, cnt, cnt_lo)
        hi = jnp.where(lt, mid, hi)
        settled = (cnt_lo == kf) | (hi == lo + 1)
        open_ = jnp.where(settled, 0, open_)
        go = jnp.where(it < max_steps, jnp.max(open_), 0)
        return it + 1, lo, hi, cnt_lo, open_, go

    if bounds is None:
        lo0 = jnp.full(qshape, _KEY_NEG_INF, I32)
        hi0 = jnp.full(qshape, _KEY_POS_INF, I32)
        cnt0 = jnp.full(qshape, -1.0, F32)
        open0 = active
    else:
        n_gt0, n_ge0 = count([lambda s: s > 0.0, lambda s: s >= 0.0])
        positive = n_gt0 >= kf
        non_negative = n_ge0 >= kf
        key_zero = _key(jnp.zeros(qshape, F32))
        lo0 = jnp.where(non_negative, key_zero, _key(bounds[0]))
        cnt0 = jnp.where(non_negative, n_ge0, -1.0)
        hi0 = jnp.where(positive, _key(bounds[1]) + 1, jnp.where(non_negative, key_zero + 1, key_zero))
        lo0 = jnp.where(active, lo0, _KEY_NEG_INF)
        hi0 = jnp.where(active, hi0, _KEY_POS_INF)
        open0 = jnp.logical_and(active, jnp.logical_not((cnt0 == kf) | (hi0 == lo0 + 1)))
    open0 = jnp.where(open0, 1, 0).astype(I32)
    _, lo, _, _, _, _ = lax.while_loop(cond, body, (jnp.int32(0), lo0, hi0, cnt0, open0, jnp.max(open0)))
    thr = jnp.where(active, _unkey(lo), -jnp.inf)
    n_gt, n_ge = count([lambda s: s > thr, lambda s: s >= thr])
    need = jnp.where(active, kf - n_gt, 0.0)
    ties = jnp.where(jnp.logical_and(active, (n_ge - n_gt) > need), 1, 0).astype(I32)
    return thr, need, ties


def _earlier_matrix(w, key_axis):
    r_i = lax.broadcasted_iota(I32, (w, w), 0)
    c_i = lax.broadcasted_iota(I32, (w, w), 1)
    return jnp.where(r_i < c_i if key_axis == 1 else c_i < r_i, 1.0, 0.0).astype(BF16)


def _selection_bias(s, thr, need, tie_flag, tie_cnt_ref, bias_ref, earlier_ref, key_axis=1):
    w = s.shape[key_axis]

    @pl.when(tie_flag == 0)
    def _():
        sel = jnp.logical_and(s >= thr, s > -jnp.inf)
        bias_ref[...] = jnp.where(sel, 0.0, NEG_BIG)

    @pl.when(tie_flag != 0)
    def _():
        eq = s == thr
        eqf = jnp.where(eq, 1.0, 0.0)
        earlier = earlier_ref[:w, :w]
        if key_axis == 1:
            rank = jnp.dot(eqf.astype(BF16), earlier, preferred_element_type=F32)
        else:
            rank = jnp.dot(earlier, eqf.astype(BF16), preferred_element_type=F32)
        rank = rank + tie_cnt_ref[...]
        sel = jnp.logical_or(s > thr, jnp.logical_and(eq, rank < need))
        bias_ref[...] = jnp.where(sel, 0.0, NEG_BIG)
        tie_cnt_ref[...] = tie_cnt_ref[...] + jnp.sum(eqf, axis=key_axis, keepdims=True)


def _dsa_prompt_kernel(qb_ref, qi_ref, wi_ref, ki_ref, k_hbm, vt_hbm, o_ref,
                       k_scr, vt_scr, sc_scr, qm_scr, m_scr, l_scr, acc_scr, bias_scr, tie_scr, earlier_scr, sem,
                       *, q_blk, k_blk, seq, n_sel):
    b = pl.program_id(0)
    i = pl.program_id(1)

    @pl.when(i == 0)
    def _():
        ck = pltpu.make_async_copy(k_hbm.at[pl.ds(b * seq, seq)], k_scr, sem.at[0])
        cv = pltpu.make_async_copy(vt_hbm.at[:, pl.ds(b * seq, seq)], vt_scr, sem.at[1])
        ck.start()
        cv.start()
        ck.wait()
        cv.wait()

    q0 = i * q_blk
    n_blocks = (q0 + q_blk + k_blk - 1) // k_blk
    q_pos = q0 + lax.broadcasted_iota(I32, (1, q_blk), 1)

    qit = qi_ref[...].T.astype(BF16)
    qit_h = [qit[h * IDX_DIM:(h + 1) * IDX_DIM] for h in range(IDX_HEADS)]
    wit = wi_ref[...].T
    w_h = [wit[_WI_LANE + h:_WI_LANE + h + 1] for h in range(IDX_HEADS)]

    def score_body(c, carry):
        off = pl.multiple_of(c * k_blk, k_blk)
        kc = ki_ref[pl.ds(off, k_blk), :]
        s = w_h[0] * jnp.maximum(jnp.dot(kc, qit_h[0], preferred_element_type=F32), 0.0)
        for h in range(1, IDX_HEADS):
            s = s + w_h[h] * jnp.maximum(jnp.dot(kc, qit_h[h], preferred_element_type=F32), 0.0)
        k_pos = off + lax.broadcasted_iota(I32, (k_blk, 1), 0)
        adm = k_pos <= q_pos
        sc_scr[pl.ds(off, k_blk), :] = jnp.where(adm, s, -jnp.inf)
        hi_part, lo_part = carry
        hi_part = jnp.maximum(hi_part, _reduce_rows(jnp.where(adm, s, -jnp.inf), jnp.max))
        lo_part = jnp.minimum(lo_part, _reduce_rows(jnp.where(adm, s, jnp.inf), jnp.min))
        return hi_part, lo_part

    hi_part, lo_part = lax.fori_loop(
        0, n_blocks, score_body,
        (jnp.full((SUBLANES, q_blk), -jnp.inf, F32), jnp.full((SUBLANES, q_blk), jnp.inf, F32)))
    s_max = jnp.max(hi_part, axis=0, keepdims=True)
    s_min = jnp.min(lo_part, axis=0, keepdims=True)

    def read_block(c):
        return sc_scr[pl.ds(pl.multiple_of(c * k_blk, k_blk), k_blk), :]

    active = (q_pos + 1) > n_sel
    thr, need, ties = _kth_largest_threshold(read_block, n_blocks, q_blk, n_sel, active, key_axis=0,
                                             bounds=(s_min, s_max))
    tie_flag = jnp.max(ties)

    qt = (qb_ref[...] * (DSA_DH ** -0.5)).T
    row_lo = lax.broadcasted_iota(I32, (LANES, 1), 0) < DSA_DH
    for p in range(DSA_HEADS // 2):
        qp = qt[p * LANES:(p + 1) * LANES]
        qm_scr[2 * p] = jnp.where(row_lo, qp, 0.0).astype(BF16)
        qm_scr[2 * p + 1] = jnp.where(row_lo, 0.0, qp).astype(BF16)
    m_scr[...] = jnp.full(m_scr.shape, NEG_BIG, F32)
    l_scr[...] = jnp.zeros(l_scr.shape, F32)
    acc_scr[...] = jnp.zeros(acc_scr.shape, F32)
    tie_scr[...] = jnp.zeros(tie_scr.shape, F32)

    @pl.when(tie_flag != 0)
    def _():
        earlier_scr[...] = _earlier_matrix(k_blk, 0)

    def att_body(j, carry):
        off = pl.multiple_of(j * k_blk, k_blk)
        _selection_bias(sc_scr[pl.ds(off, k_blk), :], thr, need, tie_flag, tie_scr, bias_scr, earlier_scr,
                        key_axis=0)
        bias = bias_scr[...]

        def qk(h):
            p = h // 2
            kp = k_scr[pl.ds(off, k_blk), p * LANES:(p + 1) * LANES]
            return jnp.dot(kp, qm_scr[h], preferred_element_type=F32)

        prs, alphas = [], []
        for h in range(DSA_HEADS):
            logit = qk(h) + bias
            m_old = m_scr[h]
            m_new = jnp.maximum(m_old, jnp.max(_reduce_rows(logit, jnp.max), axis=0, keepdims=True))
            alpha = jnp.exp(m_old - m_new)
            pe = jnp.exp(logit - m_new)
            l_scr[h] = alpha * l_scr[h] + jnp.sum(_reduce_rows(pe, jnp.sum), axis=0, keepdims=True)
            m_scr[h] = m_new
            prs.append(pe.astype(BF16))
            alphas.append(alpha)
        for h in range(DSA_HEADS):
            rows = slice(h * DSA_DH, (h + 1) * DSA_DH)
            vth = vt_scr[rows, pl.ds(off, k_blk)]
            acc_scr[rows, :] = alphas[h] * acc_scr[rows, :] + jnp.dot(vth, prs[h], preferred_element_type=F32)
        return carry

    lax.fori_loop(0, n_blocks, att_body, 0)

    for h in range(DSA_HEADS):
        rows = slice(h * DSA_DH, (h + 1) * DSA_DH)
        acc_scr[rows, :] = acc_scr[rows, :] / l_scr[h]
    o_ref[...] = acc_scr[...].T


def _dsa_prompt(qb, qi, wi, ki16, k16, vt16, batch, seq, q_blk, k_blk):
    n_sel = min(DSA_TOPK, seq // 4)
    nq = seq // q_blk
    qspec = lambda w: pl.BlockSpec((q_blk, w), lambda b, i: (b * nq + i, 0))
    return pl.pallas_call(
        functools.partial(_dsa_prompt_kernel, q_blk=q_blk, k_blk=k_blk, seq=seq, n_sel=n_sel),
        grid=(batch, nq),
        in_specs=[qspec(DSA_W), qspec(IDX_W), qspec(LANES),
                  pl.BlockSpec((seq, IDX_DIM), lambda b, i: (b, 0)),
                  pl.BlockSpec(memory_space=pl.ANY), pl.BlockSpec(memory_space=pl.ANY)],
        out_specs=qspec(DSA_W),
        out_shape=jax.ShapeDtypeStruct((batch * seq, DSA_W), F32),
        scratch_shapes=[pltpu.VMEM((seq, DSA_W), BF16), pltpu.VMEM((DSA_W, seq), BF16),
                        pltpu.VMEM((seq, q_blk), F32),
                        pltpu.VMEM((DSA_HEADS, LANES, q_blk), BF16),
                        pltpu.VMEM((DSA_HEADS, 1, q_blk), F32), pltpu.VMEM((DSA_HEADS, 1, q_blk), F32),
                        pltpu.VMEM((DSA_W, q_blk), F32), pltpu.VMEM((k_blk, q_blk), F32),
                        pltpu.VMEM((1, q_blk), F32), pltpu.VMEM((k_blk, k_blk), BF16),
                        pltpu.SemaphoreType.DMA((2,))],
        compiler_params=_params(("arbitrary", "arbitrary")),
        name="dsa_prompt",
    )(qb, qi, wi, ki16, k16, vt16)


PAGES_PER_STEP = 8
SCORE_PAGES_PER_STEP = 32
_STEP_KEYS = PAGES_PER_STEP * PAGE_SIZE
_NT = (((1,), (1,)), ((), ()))


def _page_specs(page_shape, n_pages, per_step, clamp_last):
    specs = []
    zeros = (0,) * len(page_shape)
    for s in range(per_step):
        def imap(b, c, pt, s=s):
            page = jnp.minimum(c * per_step + s, n_pages - per_step + s) if clamp_last else c * per_step + s
            return (pt[b, page],) + zeros
        specs.append(pl.BlockSpec((1,) + page_shape, imap))
    return specs


def _stack_heads(qi):
    return jnp.concatenate([qi[:, h * IDX_DIM:(h + 1) * IDX_DIM] for h in range(IDX_HEADS)], axis=0)


def _idx_score(s4, wi, t):
    s = wi[:, _WI_LANE:_WI_LANE + 1] * jnp.maximum(s4[:t], 0.0)
    for h in range(1, IDX_HEADS):
        s = s + wi[:, _WI_LANE + h:_WI_LANE + h + 1] * jnp.maximum(s4[h * t:(h + 1) * t], 0.0)
    return s


def _dsa_sample_score_kernel(pt_ref, qi_ref, wi_ref, kin_ref, *rest, t, past):
    pages = rest[:SCORE_PAGES_PER_STEP]
    sc_ref, = rest[SCORE_PAGES_PER_STEP:]
    c = pl.program_id(1)
    nch = pl.num_programs(1)
    step_keys = SCORE_PAGES_PER_STEP * PAGE_SIZE
    qi4 = _stack_heads(qi_ref[...]).astype(BF16)
    wi = wi_ref[...]
    keys_t = jnp.concatenate([p[0] for p in pages], axis=1).astype(BF16)
    off = pl.multiple_of(c * step_keys, step_keys)
    sc_ref[0, :, pl.ds(off, step_keys)] = _idx_score(
        jnp.dot(qi4, keys_t, preferred_element_type=F32), wi, t)

    @pl.when(c == nch - 1)
    def _():
        knew = jnp.concatenate([kin_ref[...], jnp.zeros((LANES - t, IDX_DIM), F32)], axis=0).astype(BF16)
        s = _idx_score(lax.dot_general(qi4, knew, _NT, preferred_element_type=F32), wi, t)
        col = lax.broadcasted_iota(I32, (t, LANES), 1)
        row = lax.broadcasted_iota(I32, (t, LANES), 0)
        sc_ref[0, :, past:past + LANES] = jnp.where(col <= row, s, -jnp.inf)


THRESHOLD_GROUP = 8


def _dsa_sample_threshold_kernel(sc_ref, thr_ref, need_ref, flag_ref, *, t, past, n_sel):
    g = sc_ref.shape[0]
    rows = g * t

    def scores(_):
        return sc_ref[...].reshape(rows, sc_ref.shape[2])

    s = scores(0)
    s_max = jnp.max(s, axis=1, keepdims=True)
    s_min = jnp.min(jnp.where(s > -jnp.inf, s, jnp.inf), axis=1, keepdims=True)
    q_in_seq = lax.broadcasted_iota(I32, (g, t, 1), 1).reshape(rows, 1)
    active = (past + 1 + q_in_seq) > n_sel
    thr, need, ties = _kth_largest_threshold(scores, 1, rows, n_sel, active, bounds=(s_min, s_max))
    thr_ref[...] = thr.reshape(g, t, 1)
    need_ref[...] = need.reshape(g, t, 1)
    flag_ref[...] = ties.astype(F32).reshape(g, t, 1)


def _dsa_sample_attn_kernel(pt_ref, qb_ref, sc_ref, thr_ref, need_ref, flag_ref, kn_ref, vn_ref, *rest,
                            t, past):
    kpages = rest[:PAGES_PER_STEP]
    vpages = rest[PAGES_PER_STEP:2 * PAGES_PER_STEP]
    o_ref, m_scr, l_scr, acc_scr, bias_scr, biasn_scr, tie_scr, earlier_scr = rest[2 * PAGES_PER_STEP:]
    c = pl.program_id(1)
    nch = pl.num_programs(1) - 1
    flag = jnp.max(flag_ref[0]).astype(I32)

    @pl.when(jnp.logical_and(c == 0, flag != 0))
    def _():
        earlier_scr[...] = _earlier_matrix(_STEP_KEYS, 1)

    @pl.when(c == 0)
    def _():
        m_scr[...] = jnp.full(m_scr.shape, NEG_BIG, F32)
        l_scr[...] = jnp.zeros(l_scr.shape, F32)
        acc_scr[...] = jnp.zeros(acc_scr.shape, F32)
        tie_scr[...] = jnp.zeros(tie_scr.shape, F32)

    rows = DSA_HEADS * t
    blockmask = (lax.broadcasted_iota(I32, (rows, DSA_W), 0) // t
                 == lax.broadcasted_iota(I32, (rows, DSA_W), 1) // DSA_DH)
    qs = qb_ref[...] * (DSA_DH ** -0.5)
    qbd = jnp.where(blockmask, jnp.concatenate([qs] * DSA_HEADS, axis=0), 0.0).astype(BF16)
    thr = thr_ref[0]
    need = need_ref[0]

    def update(bias, k_op, v_op, keys_on_lanes):
        if keys_on_lanes:
            logit = jnp.dot(qbd, k_op, preferred_element_type=F32)
        else:
            logit = lax.dot_general(qbd, k_op, _NT, preferred_element_type=F32)
        logit = logit + jnp.concatenate([bias] * DSA_HEADS, axis=0)
        m_old = m_scr[...]
        m_new = jnp.maximum(m_old, jnp.max(logit, axis=-1, keepdims=True))
        alpha = jnp.exp(m_old - m_new)
        pr = jnp.exp(logit - m_new).astype(BF16)
        l_scr[...] = alpha * l_scr[...] + jnp.sum(pr.astype(F32), axis=-1, keepdims=True)
        m_scr[...] = m_new
        if keys_on_lanes:
            pv = lax.dot_general(pr, v_op, _NT, preferred_element_type=F32)
        else:
            pv = jnp.dot(pr, v_op, preferred_element_type=F32)
        acc_scr[...] = alpha * acc_scr[...] + pv

    @pl.when(c < nch)
    def _():
        off = pl.multiple_of(c * _STEP_KEYS, _STEP_KEYS)
        _selection_bias(sc_ref[0, :, pl.ds(off, _STEP_KEYS)], thr, need, flag, tie_scr, bias_scr, earlier_scr)
        kc = jnp.concatenate([p[0].reshape(DSA_W, PAGE_SIZE) for p in kpages], axis=1).astype(BF16)
        vc = jnp.concatenate([p[0].reshape(DSA_W, PAGE_SIZE) for p in vpages], axis=1).astype(BF16)
        update(bias_scr[...], kc, vc, True)

    @pl.when(c == nch)
    def _():
        _selection_bias(sc_ref[0, :, past:past + LANES], thr, need, flag, tie_scr, biasn_scr, earlier_scr)
        zpad = jnp.zeros((LANES - t, DSA_W), F32)
        kc = jnp.concatenate([kn_ref[...], zpad], axis=0).astype(BF16)
        vc = jnp.concatenate([vn_ref[...], zpad], axis=0).astype(BF16)
        update(biasn_scr[...], kc, vc, False)
        outn = jnp.where(blockmask, acc_scr[...] / l_scr[...], 0.0)
        out = outn[:t]
        for h in range(1, DSA_HEADS):
            out = out + outn[h * t:(h + 1) * t]
        o_ref[...] = out


def _dsa_sample(qb, qi, wi, ki_new, k_new, v_new, cache_k, cache_v, cache_idx_k, page_table, batch, t):
    n_pages = page_table.shape[1]
    past = n_pages * PAGE_SIZE
    n_sel = min(DSA_TOPK, (past + t) // 4)
    nch = n_pages // PAGES_PER_STEP
    nch_score = n_pages // SCORE_PAGES_PER_STEP
    lp = past + LANES
    kv_page = (DSA_HEADS, DSA_DH, PAGE_SIZE)
    ck = jnp.transpose(cache_k, (0, 2, 3, 1))
    cv = jnp.transpose(cache_v, (0, 2, 3, 1))
    cik = jnp.transpose(cache_idx_k, (0, 2, 1))
    rspec = lambda w: pl.BlockSpec((t, w), lambda b, c, pt: (b, 0))
    bspec = lambda w: pl.BlockSpec((1, t, w), lambda b, c, pt: (b, 0, 0))
    scores = pl.pallas_call(
        functools.partial(_dsa_sample_score_kernel, t=t, past=past),
        grid_spec=pltpu.PrefetchScalarGridSpec(
            num_scalar_prefetch=1,
            grid=(batch, nch_score),
            in_specs=[rspec(IDX_W), rspec(LANES), rspec(IDX_DIM)]
            + _page_specs((IDX_DIM, PAGE_SIZE), n_pages, SCORE_PAGES_PER_STEP, False),
            out_specs=bspec(lp),
        ),
        out_shape=jax.ShapeDtypeStruct((batch, t, lp), F32),
        compiler_params=_params(("arbitrary", "arbitrary")),
        name="dsa_sample_score",
    )(page_table, qi, wi, ki_new, *([cik] * SCORE_PAGES_PER_STEP))
    grp = THRESHOLD_GROUP if batch % THRESHOLD_GROUP == 0 else 1
    gspec = lambda w: pl.BlockSpec((grp, t, w), lambda i: (i, 0, 0))
    thr, need, flag = pl.pallas_call(
        functools.partial(_dsa_sample_threshold_kernel, t=t, past=past, n_sel=n_sel),
        grid=(batch // grp,),
        in_specs=[gspec(lp)],
        out_specs=[gspec(1)] * 3,
        out_shape=[jax.ShapeDtypeStruct((batch, t, 1), F32)] * 3,
        compiler_params=_params(("parallel",)),
        name="dsa_sample_threshold",
    )(scores)
    return pl.pallas_call(
        functools.partial(_dsa_sample_attn_kernel, t=t, past=past),
        grid_spec=pltpu.PrefetchScalarGridSpec(
            num_scalar_prefetch=1,
            grid=(batch, nch + 1),
            in_specs=[rspec(DSA_W), bspec(lp), bspec(1), bspec(1), bspec(1), rspec(DSA_W), rspec(DSA_W)]
            + _page_specs(kv_page, n_pages, PAGES_PER_STEP, True)
            + _page_specs(kv_page, n_pages, PAGES_PER_STEP, True),
            out_specs=rspec(DSA_W),
            scratch_shapes=[pltpu.VMEM((DSA_HEADS * t, 1), F32), pltpu.VMEM((DSA_HEADS * t, 1), F32),
                            pltpu.VMEM((DSA_HEADS * t, DSA_W), F32), pltpu.VMEM((t, _STEP_KEYS), F32),
                            pltpu.VMEM((t, LANES), F32), pltpu.VMEM((t, 1), F32),
                            pltpu.VMEM((_STEP_KEYS, _STEP_KEYS), BF16)],
        ),
        out_shape=jax.ShapeDtypeStruct((batch * t, DSA_W), F32),
        compiler_params=_params(("arbitrary", "arbitrary")),
        name="dsa_sample_attn",
    )(page_table, qb, scores, thr, need, flag, k_new, v_new,
      *([ck] * PAGES_PER_STEP), *([cv] * PAGES_PER_STEP))


MOE_ROWS = 256
ROUTE_TILE = 512
TOKEN_TILE = 256


def _split_bf16(a):
    hi = a.astype(BF16)
    lo = (a - hi.astype(F32)).astype(BF16)
    return hi, lo


def _router_kernel(x_ref, whi_ref, wlo_ref, b_ref, idx_ref, gate_ref, rank_ref, cnt_ref, carry_scr):
    i = pl.program_id(0)
    tm = x_ref.shape[0]

    @pl.when(i == 0)
    def _():
        carry_scr[...] = jnp.zeros(carry_scr.shape, F32)

    xhi, xlo = _split_bf16(x_ref[...])
    whi = whi_ref[...]
    logits = (jnp.dot(xhi, whi, preferred_element_type=F32)
              + jnp.dot(xlo, whi, preferred_element_type=F32)
              + jnp.dot(xhi, wlo_ref[...], preferred_element_type=F32)) + b_ref[...]
    lane = lax.broadcasted_iota(I32, (tm, N_EXPERTS), 1)
    slot = lax.broadcasted_iota(I32, (tm, TOP_K), 1)
    vals, idxs = [], []
    cur = logits
    for _ in range(TOP_K):
        m = jnp.max(cur, axis=-1, keepdims=True)
        ix = jnp.min(jnp.where(cur == m, lane, N_EXPERTS), axis=-1, keepdims=True)
        vals.append(m)
        idxs.append(ix)
        cur = jnp.where(lane == ix, -jnp.inf, cur)
    es = [jnp.exp(v - vals[0]) for v in vals]
    denom = es[0] + es[1] + es[2] + es[3]
    onehot = jnp.zeros((tm, N_EXPERTS), F32)
    for ix in idxs:
        onehot = onehot + jnp.where(lane == ix, 1.0, 0.0)
    earlier = (lax.broadcasted_iota(I32, (tm, tm), 1) < lax.broadcasted_iota(I32, (tm, tm), 0))
    excl = jnp.dot(jnp.where(earlier, 1.0, 0.0).astype(BF16), onehot.astype(BF16),
                   preferred_element_type=F32) + carry_scr[...]
    idx_out = jnp.zeros((tm, TOP_K), I32)
    gate_out = jnp.zeros((tm, TOP_K), F32)
    rank_out = jnp.zeros((tm, TOP_K), F32)
    for k in range(TOP_K):
        rk = jnp.sum(jnp.where(lane == idxs[k], excl, 0.0), axis=-1, keepdims=True)
        idx_out = jnp.where(slot == k, idxs[k], idx_out)
        gate_out = jnp.where(slot == k, es[k] / denom, gate_out)
        rank_out = jnp.where(slot == k, rk, rank_out)
    idx_ref[...] = idx_out
    gate_ref[...] = gate_out
    rank_ref[...] = rank_out.astype(I32)
    total = carry_scr[...] + jnp.sum(onehot, axis=0, keepdims=True)
    carry_scr[...] = total
    cnt_ref[...] = total


def _router(x, w_router, b_router):
    n = x.shape[0]
    tm = ROUTE_TILE if n % ROUTE_TILE == 0 else TOKEN_TILE
    whi, wlo = _split_bf16(w_router)
    b2 = b_router.reshape(1, N_EXPERTS)
    kspec = pl.BlockSpec((tm, TOP_K), lambda i: (i, 0))
    return pl.pallas_call(
        _router_kernel,
        grid=(n // tm,),
        in_specs=[_row_spec(tm, D_MODEL), _full_spec(whi), _full_spec(wlo), _full_spec(b2)],
        out_specs=[kspec, kspec, kspec, pl.BlockSpec((1, N_EXPERTS), lambda i: (0, 0))],
        out_shape=[jax.ShapeDtypeStruct((n, TOP_K), I32), jax.ShapeDtypeStruct((n, TOP_K), F32),
                   jax.ShapeDtypeStruct((n, TOP_K), I32), jax.ShapeDtypeStruct((1, N_EXPERTS), F32)],
        scratch_shapes=[pltpu.VMEM((1, N_EXPERTS), F32)],
        compiler_params=_params(("arbitrary",)),
        name="router",
    )(x, whi, wlo, b2)


def _row_copy(src, src_row, dst, dst_row, sem):
    return pltpu.make_async_copy(src.at[pl.ds(src_row, 1)], dst.at[pl.ds(dst_row, 1)], sem)


def _dispatch_kernel(dest_ref, x_ref, xs_in, xs_out, sem):
    del xs_in
    tm = x_ref.shape[0]

    def issue(r, c):
        for k in range(TOP_K):
            _row_copy(x_ref, r, xs_out, dest_ref[r * TOP_K + k], sem).start()
        return c

    lax.fori_loop(0, tm, issue, 0)
    for _ in range(TOP_K):
        pltpu.make_async_copy(x_ref, x_ref, sem).wait()


def _dispatch(x, dest_flat, n_rows):
    n = x.shape[0]
    tm = TOKEN_TILE
    return pl.pallas_call(
        _dispatch_kernel,
        grid=(n // tm,),
        in_specs=[pl.BlockSpec((tm * TOP_K,), lambda i: (i,), memory_space=pltpu.SMEM),
                  _row_spec(tm, D_MODEL), pl.BlockSpec(memory_space=pl.ANY)],
        out_specs=pl.BlockSpec(memory_space=pl.ANY),
        out_shape=jax.ShapeDtypeStruct((n_rows, D_MODEL), F32),
        scratch_shapes=[pltpu.SemaphoreType.DMA(())],
        input_output_aliases={2: 0},
        compiler_params=_params(("arbitrary",)),
        name="moe_dispatch",
    )(dest_flat, x, jnp.zeros((n_rows, D_MODEL), F32))


def _expert_kernel(be_ref, na_ref, x_ref, w1_ref, b1_ref, w2_ref, b2_ref, o_ref, w1b_scr, w2b_scr):
    i = pl.program_id(0)
    active = i < na_ref[0]
    changed = jnp.logical_or(i == 0, be_ref[i] != be_ref[jnp.maximum(i - 1, 0)])

    @pl.when(jnp.logical_and(active, changed))
    def _():
        w1b_scr[...] = w1_ref[0].astype(BF16)
        w2b_scr[...] = w2_ref[0].astype(BF16)

    @pl.when(active)
    def _():
        h = jnp.dot(x_ref[...].astype(BF16), w1b_scr[...], preferred_element_type=F32) + b1_ref[0]
        gt = jnp.minimum(h[:, :D_FF], SWIGLU_LIMIT)
        up = jnp.clip(h[:, D_FF:], -SWIGLU_LIMIT, SWIGLU_LIMIT)
        act = (up + 1.0) * gt * jax.nn.sigmoid(SWIGLU_ALPHA * gt)
        o_ref[...] = jnp.dot(act.astype(BF16), w2b_scr[...], preferred_element_type=F32) + b2_ref[0]

    @pl.when(jnp.logical_not(active))
    def _():
        o_ref[...] = jnp.zeros(o_ref.shape, F32)


def _experts(xs, block_e, n_active, w1, b1, w2, b2):
    n_rows = xs.shape[0]
    nblk = n_rows // MOE_ROWS
    w1 = w1.reshape(-1, D_MODEL, 2 * D_FF)
    w2 = w2.reshape(-1, D_FF, D_MODEL)
    b1r = b1.reshape(-1, 1, 2 * D_FF)
    b2r = b2.reshape(-1, 1, D_MODEL)
    last = lambda i, na: jnp.minimum(i, na[0] - 1)
    return pl.pallas_call(
        _expert_kernel,
        grid_spec=pltpu.PrefetchScalarGridSpec(
            num_scalar_prefetch=2,
            grid=(nblk,),
            in_specs=[pl.BlockSpec((MOE_ROWS, D_MODEL), lambda i, be, na: (last(i, na), 0)),
                      pl.BlockSpec((1, D_MODEL, 2 * D_FF), lambda i, be, na: (be[i], 0, 0)),
                      pl.BlockSpec((1, 1, 2 * D_FF), lambda i, be, na: (be[i], 0, 0)),
                      pl.BlockSpec((1, D_FF, D_MODEL), lambda i, be, na: (be[i], 0, 0)),
                      pl.BlockSpec((1, 1, D_MODEL), lambda i, be, na: (be[i], 0, 0))],
            out_specs=pl.BlockSpec((MOE_ROWS, D_MODEL), lambda i, be, na: (i, 0)),
            scratch_shapes=[pltpu.VMEM((D_MODEL, 2 * D_FF), BF16), pltpu.VMEM((D_FF, D_MODEL), BF16)],
        ),
        out_shape=jax.ShapeDtypeStruct((n_rows, D_MODEL), F32),
        compiler_params=_params(("arbitrary",)),
        name="moe_experts",
    )(block_e, n_active, xs, w1, b1r, w2, b2r)


def _combine_kernel(dest_ref, x_ref, gate_ref, g_ref, b_ref, yb_hbm, o_ref, buf, sem):
    tm = x_ref.shape[0]

    def issue(r, c):
        for k in range(TOP_K):
            _row_copy(yb_hbm, dest_ref[r * TOP_K + k], buf.at[k], r, sem).start()
        return c

    lax.fori_loop(0, tm, issue, 0)
    for k in range(TOP_K):
        pltpu.make_async_copy(buf.at[k], buf.at[k], sem).wait()
    gate = gate_ref[...]
    y = gate[:, 0:1] * buf[0]
    for k in range(1, TOP_K):
        y = y + gate[:, k:k + 1] * buf[k]
    o_ref[...] = _layer_norm(DN_ALPHA * x_ref[...] + y, g_ref[...], b_ref[...])


def _combine(x, yb, dest_flat, gate, g, b):
    n = x.shape[0]
    tm = TOKEN_TILE
    g2, b2 = g.reshape(1, D_MODEL), b.reshape(1, D_MODEL)
    return pl.pallas_call(
        _combine_kernel,
        grid=(n // tm,),
        in_specs=[pl.BlockSpec((tm * TOP_K,), lambda i: (i,), memory_space=pltpu.SMEM),
                  _row_spec(tm, D_MODEL), pl.BlockSpec((tm, TOP_K), lambda i: (i, 0)),
                  _full_spec(g2), _full_spec(b2), pl.BlockSpec(memory_space=pl.ANY)],
        out_specs=_row_spec(tm, D_MODEL),
        out_shape=jax.ShapeDtypeStruct((n, D_MODEL), F32),
        scratch_shapes=[pltpu.VMEM((TOP_K, tm, D_MODEL), F32), pltpu.SemaphoreType.DMA(())],
        compiler_params=_params(("arbitrary",)),
        name="moe_combine",
    )(dest_flat, x, gate, g2, b2, yb)


def _moe_layer(x, layer, w_router, b_router, w1, b1, w2, b2, g, b):
    n = x.shape[0]
    idx, gate, rank, counts = _router(x, w_router, b_router)
    nblk = (n * TOP_K) // MOE_ROWS + N_EXPERTS
    cnt = counts[0].astype(I32)
    padded = (cnt + MOE_ROWS - 1) // MOE_ROWS * MOE_ROWS
    pad_end = jnp.cumsum(padded)
    pad_start = pad_end - padded
    dest = (pad_start[idx] + rank).reshape(-1)
    n_active = (pad_end[-1] // MOE_ROWS).astype(I32)
    blk = jnp.arange(nblk, dtype=I32)
    blk = jnp.minimum(blk, n_active - 1)
    block_e = jnp.sum((pad_end[None, :] <= (blk * MOE_ROWS)[:, None]).astype(I32), axis=1)
    block_e = jnp.minimum(block_e, N_EXPERTS - 1) + layer * N_EXPERTS
    xs = _dispatch(x, dest, nblk * MOE_ROWS)
    yb = _experts(xs, block_e, n_active.reshape(1), w1, b1, w2, b2)
    return _combine(x, yb, dest, gate, g, b)


POOL_HALO = 16


def _pool_kernel(x_ref, halo_ref, w_ref, sc_ref, g_ref, b_ref, o_ref, ext_scr, *, tiles_per_seq, n_prev):
    tm = x_ref.shape[0]
    tile = pl.program_id(0) % tiles_per_seq
    x = x_ref[...]
    halo = halo_ref[...]
    if n_prev == 0:
        halo = jnp.where(tile == 0, 0.0, halo)
    ext_scr[0:POOL_HALO] = halo
    ext_scr[POOL_HALO:POOL_HALO + tm] = x
    pos = n_prev + tile * tm + lax.broadcasted_iota(I32, (tm, 1), 0)
    parts = []
    for g, w in enumerate(POOL_WINDOWS):
        cols = slice(g * POOL_GROUP, (g + 1) * POOL_GROUP)
        xg = x[:, cols]
        s = xg
        for d in range(1, w):
            s = s + ext_scr[POOL_HALO - d:POOL_HALO - d + tm, cols]
        cnt = jnp.minimum(pos + 1, w).astype(F32)
        dg = s / cnt - xg
        parts.append(jnp.dot(dg.astype(BF16), w_ref[g], preferred_element_type=F32))
    mix = jnp.concatenate(parts, axis=1) * sc_ref[...]
    o_ref[...] = _layer_norm(DN_ALPHA * x + mix, g_ref[...], b_ref[...])


def _pool(x, halo_src, pool_w, pool_scale, g, b, tm, tiles_per_seq, n_prev, halo_map):
    n = x.shape[0]
    w16 = pool_w.astype(BF16)
    sc = pool_scale.reshape(1, D_MODEL)
    g2, b2 = g.reshape(1, D_MODEL), b.reshape(1, D_MODEL)
    return pl.pallas_call(
        functools.partial(_pool_kernel, tiles_per_seq=tiles_per_seq, n_prev=n_prev),
        grid=(n // tm,),
        in_specs=[_row_spec(tm, D_MODEL), pl.BlockSpec((POOL_HALO, D_MODEL), halo_map),
                  _full_spec(w16), _full_spec(sc), _full_spec(g2), _full_spec(b2)],
        out_specs=_row_spec(tm, D_MODEL),
        out_shape=jax.ShapeDtypeStruct((n, D_MODEL), F32),
        scratch_shapes=[pltpu.VMEM((POOL_HALO + tm, D_MODEL), F32)],
        compiler_params=_params(("parallel",)),
        name="pool",
    )(x, halo_src, w16, sc, g2, b2)


PROJ_TILE = 512
DSA_Q_BLOCK = 256
DSA_K_BLOCK = 512
POOL_TILE = 512


def kernel(x_prompt, x_sample, cache_k, cache_v, cache_idx_k, page_table, state_gla, state_pool,
           w_in, gla_fg_w2, gla_fg_b, gla_norm_g, idx_kn_g, idx_kn_b, w_out,
           pool_w, pool_scale, ln_mix_g, ln_mix_b, ln_ffn_g, ln_ffn_b,
           moe_router_w, moe_router_b, moe_w1, moe_b1, moe_w2, moe_b2):
    bp, sp, _ = x_prompt.shape
    bs, ts, _ = x_sample.shape
    n_p, n_s = bp * sp, bs * ts
    xp = x_prompt.reshape(n_p, D_MODEL)
    xs = x_sample.reshape(n_s, D_MODEL)
    proj_w = (w_in, gla_fg_w2, gla_fg_b, idx_kn_g, idx_kn_b)

    qa, ka, va, ga, lf, qb, kb, vb, qi, ki, wi, kb16, vt16, ki16 = _project(xp, *proj_w, tm=PROJ_TILE)
    oa_p, gla_p = _gla(qa, ka, va, lf, ga, jnp.zeros((bp, GLA_HEADS, GLA_DK, GLA_DV), F32), gla_norm_g, bp, sp)
    ob_p = _dsa_prompt(qb, qi, wi, ki16, kb16, vt16, bp, sp, DSA_Q_BLOCK, DSA_K_BLOCK)
    x1p = _merge(xp, oa_p, ob_p, w_out, ln_mix_g[0], ln_mix_b[0], PROJ_TILE)
    k_p = kb.reshape(bp, sp, DSA_HEADS, DSA_DH)
    v_p = vb.reshape(bp, sp, DSA_HEADS, DSA_DH)
    kidx_p = ki.reshape(bp, sp, IDX_DIM)

    qa, ka, va, ga, lf, qb, kb, vb, qi, ki, wi, _, _, _ = _project(xs, *proj_w, tm=n_s)
    oa_s, gla_s = _gla(qa, ka, va, lf, ga, state_gla, gla_norm_g, bs, ts)
    ob_s = _dsa_sample(qb, qi, wi, ki, kb, vb, cache_k, cache_v, cache_idx_k, page_table, bs, ts)
    x1s = _merge(xs, oa_s, ob_s, w_out, ln_mix_g[0], ln_mix_b[0], n_s)
    k_s = kb.reshape(bs, ts, DSA_HEADS, DSA_DH)
    v_s = vb.reshape(bs, ts, DSA_HEADS, DSA_DH)
    kidx_s = ki.reshape(bs, ts, IDX_DIM)

    x_all = _moe_layer(jnp.concatenate([x1p, x1s], axis=0), 0, moe_router_w[0], moe_router_b[0],
                       moe_w1, moe_b1, moe_w2, moe_b2, ln_ffn_g[0], ln_ffn_b[0])
    xp, xs = x_all[:n_p], x_all[n_p:]

    pool_p = xp.reshape(bp, sp, D_MODEL)[:, -POOL_STATE:]
    xs3 = xs.reshape(bs, ts, D_MODEL)
    pool_s = jnp.concatenate([state_pool, xs3], axis=1)[:, -POOL_STATE:]
    per_seq = sp // POOL_TILE
    halo_step = POOL_TILE // POOL_HALO
    x2p = _pool(xp, xp, pool_w, pool_scale, ln_mix_g[1], ln_mix_b[1], POOL_TILE, per_seq, 0,
                lambda i: (jnp.maximum(i * halo_step - 1, 0), 0))
    halo_s = jnp.concatenate([jnp.zeros((bs, POOL_HALO - POOL_STATE, D_MODEL), F32), state_pool], axis=1)
    x2s = _pool(xs, halo_s.reshape(bs * POOL_HALO, D_MODEL), pool_w, pool_scale, ln_mix_g[1], ln_mix_b[1],
                ts, 1, POOL_STATE, lambda i: (i, 0))
    x_all = _moe_layer(jnp.concatenate([x2p, x2s], axis=0), 1, moe_router_w[1], moe_router_b[1],
                       moe_w1, moe_b1, moe_w2, moe_b2, ln_ffn_g[1], ln_ffn_b[1])
    y_p = x_all[:n_p].reshape(bp, sp, D_MODEL)
    y_s = x_all[n_p:].reshape(bs, ts, D_MODEL)
    return (y_p, y_s, k_p, v_p, kidx_p, gla_p, pool_p, k_s, v_s, kidx_s, gla_s, pool_s)
```

```python
import functools
import math

import jax
import jax.numpy as jnp
from jax import lax
from jax.experimental import pallas as pl
from jax.experimental.pallas import tpu as pltpu

F32 = jnp.float32
BF16 = jnp.bfloat16
I32 = jnp.int32

D_MODEL = 1024
DEPTH = 2
PAGE_SIZE = 128
GLA_HEADS = 4
GLA_DK = 64
GLA_DV = 128
GLA_GATE_RANK = 16
GLA_TAU = 16.0
GLA_CHUNK = 64
GLA_TILE = 2 * GLA_CHUNK
DSA_HEADS = 8
DSA_DH = 64
IDX_HEADS = 4
IDX_DIM = 64
DSA_TOPK = 256
IDX_W_SCALE = (IDX_HEADS ** -0.5) * (IDX_DIM ** -0.5)
POOL_WINDOWS = (2, 4, 8, 16)
POOL_GROUP = D_MODEL // 4
POOL_STATE = 16 - 1
N_EXPERTS = 32
TOP_K = 4
D_FF = D_MODEL
SWIGLU_ALPHA = 1.702
SWIGLU_LIMIT = 7.0
DN_ALPHA = (2 * DEPTH) ** 0.25
LN_EPS = 1e-5
GLA_QK = GLA_HEADS * GLA_DK
GLA_V = GLA_HEADS * GLA_DV
DSA_W = DSA_HEADS * DSA_DH
IDX_W = IDX_HEADS * IDX_DIM

LANES = 128
SUBLANES = 8
VMEM_LIMIT_BYTES = 56 * 1024 * 1024

NEG_BIG = -1e30


def _params(sem, vmem=VMEM_LIMIT_BYTES):
    return pltpu.CompilerParams(dimension_semantics=sem, vmem_limit_bytes=vmem)


def _layer_norm(x, g, b):
    mu = jnp.mean(x, axis=-1, keepdims=True)
    xc = x - mu
    var = jnp.mean(xc * xc, axis=-1, keepdims=True)
    return xc * lax.rsqrt(var + LN_EPS) * g + b


def _row_spec(tm, w):
    return pl.BlockSpec((tm, w), lambda i: (i, 0))


def _full_spec(a):
    nd = a.ndim
    return pl.BlockSpec(a.shape, lambda i: (0,) * nd)


_MAIN_W = (GLA_QK, GLA_QK, GLA_V, GLA_V, DSA_W, DSA_W, DSA_W, IDX_W)
_MAIN_OFF = tuple(sum(_MAIN_W[:i]) for i in range(len(_MAIN_W) + 1))


def _proj_kernel(x_ref, wm_ref, ws_ref, fgw_ref, fgb_ref, kng_ref, knb_ref,
                 qa_ref, ka_ref, va_ref, ga_ref, lf_ref, qb_ref, kb_ref, vb_ref,
                 qi_ref, ki_ref, wi_ref, kb16_ref, vt16_ref, ki16_ref):
    xb = x_ref[...].astype(BF16)

    def mm(n):
        return jnp.dot(xb, wm_ref[:, _MAIN_OFF[n]:_MAIN_OFF[n + 1]], preferred_element_type=F32)

    qa_ref[...] = mm(0) * (GLA_DK ** -0.5)
    ka_ref[...] = mm(1)
    va_ref[...] = mm(2)
    ga_ref[...] = mm(3)
    qb_ref[...] = mm(4)
    kb = mm(5)
    kb_ref[...] = kb
    kb16_ref[...] = kb.astype(BF16)
    vb = mm(6)
    vb_ref[...] = vb
    vt16_ref[...] = vb.T.astype(BF16)
    qi_ref[...] = mm(7)

    small = jnp.dot(xb, ws_ref[...], preferred_element_type=F32)
    ki = _layer_norm(small[:, :IDX_DIM], kng_ref[...], knb_ref[...])
    ki_ref[...] = ki
    ki16_ref[...] = ki.astype(BF16)
    fa = small[:, IDX_DIM:IDX_DIM + GLA_GATE_RANK]
    z = jnp.dot(fa, fgw_ref[...], preferred_element_type=F32,
                precision=lax.Precision.HIGHEST) + fgb_ref[...]
    lf_ref[...] = (jnp.minimum(z, 0.0) - jnp.log(1.0 + jnp.exp(-jnp.abs(z)))) * (1.0 / GLA_TAU)
    wi_ref[...] = small * IDX_W_SCALE


def _project(x2d, w_in, gla_fg_w2, gla_fg_b, idx_kn_g, idx_kn_b, tm):
    n = x2d.shape[0]
    pts = [0]
    for s in (GLA_QK, GLA_QK, GLA_V, GLA_V, GLA_GATE_RANK, DSA_W, DSA_W, DSA_W, IDX_W, IDX_DIM, IDX_HEADS):
        pts.append(pts[-1] + s)
    seg = lambda i: w_in[:, pts[i]:pts[i + 1]]
    wm = jnp.concatenate([seg(0), seg(1), seg(2), seg(3), seg(5), seg(6), seg(7), seg(8)], axis=1).astype(BF16)
    pad = LANES - IDX_DIM - GLA_GATE_RANK - IDX_HEADS
    ws = jnp.concatenate([seg(9), seg(4), seg(10), jnp.zeros((D_MODEL, pad), w_in.dtype)], axis=1).astype(BF16)
    fgb = gla_fg_b.reshape(1, GLA_QK)
    kng = idx_kn_g.reshape(1, IDX_DIM)
    knb = idx_kn_b.reshape(1, IDX_DIM)
    widths = (GLA_QK, GLA_QK, GLA_V, GLA_V, GLA_QK, DSA_W, DSA_W, DSA_W, IDX_W, IDX_DIM, LANES)
    out_shape = [jax.ShapeDtypeStruct((n, w), F32) for w in widths]
    out_shape += [jax.ShapeDtypeStruct((n, DSA_W), BF16), jax.ShapeDtypeStruct((DSA_W, n), BF16),
                  jax.ShapeDtypeStruct((n, IDX_DIM), BF16)]
    out_specs = [_row_spec(tm, w) for w in widths]
    out_specs += [_row_spec(tm, DSA_W), pl.BlockSpec((DSA_W, tm), lambda i: (0, i)), _row_spec(tm, IDX_DIM)]
    return pl.pallas_call(
        _proj_kernel,
        grid=(n // tm,),
        in_specs=[_row_spec(tm, D_MODEL), _full_spec(wm), _full_spec(ws), _full_spec(gla_fg_w2),
                  _full_spec(fgb), _full_spec(kng), _full_spec(knb)],
        out_specs=out_specs,
        out_shape=out_shape,
        compiler_params=_params(("parallel",)),
        name="proj",
    )(x2d, wm, ws, gla_fg_w2, fgb, kng, knb)


def _gla_kernel(q_ref, k_ref, v_ref, lf_ref, ga_ref, s0_ref, ng_ref, o_ref, sout_ref, s_scr, *, chunk):
    c = pl.program_id(1)
    nc = pl.num_programs(1)

    @pl.when(c == 0)
    def _():
        s_scr[...] = s0_ref[0]

    g = lf_ref[...]
    ri = lax.broadcasted_iota(I32, (chunk, chunk), 0)
    ci = lax.broadcasted_iota(I32, (chunk, chunk), 1)
    causal = ci <= ri
    tri = jnp.where(causal, 1.0, 0.0).astype(F32)
    b = jnp.dot(tri, g, preferred_element_type=F32, precision=lax.Precision.HIGHEST)
    mid = chunk // 2
    b_mid = b[mid:mid + 1, :]
    b_last = b[chunk - 1:chunk, :]
    q = q_ref[...]
    k = k_ref[...]
    q_in = q * jnp.exp(b)
    q_rel = q * jnp.exp(b - b_mid)
    k_rel = k * jnp.exp(b_mid - b)
    k_out = k * jnp.exp(b_last - b)
    dec_last = jnp.exp(b_last)
    v = v_ref[...]
    ga = ga_ref[...]
    ng = ng_ref[...]
    eye = jnp.where(lax.broadcasted_iota(I32, (GLA_DK, GLA_DK), 0)
                    == lax.broadcasted_iota(I32, (GLA_DK, GLA_DK), 1), 1.0, 0.0).astype(F32)
    for h in range(GLA_HEADS):
        ks = slice(h * GLA_DK, (h + 1) * GLA_DK)
        vs = slice(h * GLA_DV, (h + 1) * GLA_DV)
        s_h = s_scr[h]
        vh = v[:, vs]
        inter = jnp.dot(q_in[:, ks], s_h, preferred_element_type=F32)
        att = lax.dot_general(q_rel[:, ks], k_rel[:, ks], (((1,), (1,)), ((), ())),
                              preferred_element_type=F32)
        att = jnp.where(causal, att, 0.0)
        o = inter + jnp.dot(att, vh, preferred_element_type=F32)
        kv = lax.dot_general(k_out[:, ks], vh, (((0,), (0,)), ((), ())), preferred_element_type=F32)
        s_scr[h] = jnp.dot(eye * dec_last[:, ks], s_h, preferred_element_type=F32,
                           precision=lax.Precision.HIGHEST) + kv
        ms = jnp.mean(o * o, axis=-1, keepdims=True)
        gh = ga[:, vs]
        o_ref[:, vs] = o * lax.rsqrt(ms + LN_EPS) * ng * (gh * jax.nn.sigmoid(gh))

    @pl.when(c == nc - 1)
    def _():
        sout_ref[0] = s_scr[...]


def _gla(qa, ka, va, lf, ga, s0, gla_norm_g, batch, seq):
    chunk = math.gcd(seq, GLA_TILE)
    nc = seq // chunk
    ng = gla_norm_g.reshape(1, GLA_DV)
    spec = lambda w: pl.BlockSpec((chunk, w), lambda b, c: (b * nc + c, 0))
    sspec = pl.BlockSpec((1, GLA_HEADS, GLA_DK, GLA_DV), lambda b, c: (b, 0, 0, 0))
    return pl.pallas_call(
        functools.partial(_gla_kernel, chunk=chunk),
        grid=(batch, nc),
        in_specs=[spec(GLA_QK), spec(GLA_QK), spec(GLA_V), spec(GLA_QK), spec(GLA_V), sspec,
                  pl.BlockSpec((1, GLA_DV), lambda b, c: (0, 0))],
        out_specs=[spec(GLA_V), sspec],
        out_shape=[jax.ShapeDtypeStruct((batch * seq, GLA_V), F32),
                   jax.ShapeDtypeStruct((batch, GLA_HEADS, GLA_DK, GLA_DV), F32)],
        scratch_shapes=[pltpu.VMEM((GLA_HEADS, GLA_DK, GLA_DV), F32)],
        compiler_params=_params(("parallel", "arbitrary")),
        name="gla",
    )(qa, ka, va, lf, ga, s0, ng)


def _merge_kernel(x_ref, oa_ref, ob_ref, w_ref, g_ref, b_ref, o_ref):
    mix = jnp.dot(oa_ref[...].astype(BF16), w_ref[:GLA_V, :], preferred_element_type=F32)
    mix = mix + jnp.dot(ob_ref[...].astype(BF16), w_ref[GLA_V:, :], preferred_element_type=F32)
    o_ref[...] = _layer_norm(DN_ALPHA * x_ref[...] + mix, g_ref[...], b_ref[...])


def _merge(x2d, oa, ob, w_out, g, b, tm):
    n = x2d.shape[0]
    w16 = w_out.astype(BF16)
    g2, b2 = g.reshape(1, D_MODEL), b.reshape(1, D_MODEL)
    return pl.pallas_call(
        _merge_kernel,
        grid=(n // tm,),
        in_specs=[_row_spec(tm, D_MODEL), _row_spec(tm, GLA_V), _row_spec(tm, DSA_W),
                  _full_spec(w16), _full_spec(g2), _full_spec(b2)],
        out_specs=_row_spec(tm, D_MODEL),
        out_shape=jax.ShapeDtypeStruct((n, D_MODEL), F32),
        compiler_params=_params(("parallel",)),
        name="merge",
    )(x2d, oa, ob, w16, g2, b2)


_KEY_NEG_INF = -2139095041
_KEY_POS_INF = 2139095040
_WI_LANE = IDX_DIM + GLA_GATE_RANK


def _unkey(kk):
    return lax.bitcast_convert_type(jnp.where(kk < 0, kk ^ 0x7FFFFFFF, kk), F32)


def _key(v):
    i = lax.bitcast_convert_type(v, I32)
    return jnp.where(i < 0, i ^ 0x7FFFFFFF, i)


_REDUCE_CHAINS = 8


def _reduce_rows(x, reduce_fn):
    r, q = x.shape
    groups = r // SUBLANES
    chains = _REDUCE_CHAINS if groups % _REDUCE_CHAINS == 0 else 1
    y = x.reshape(chains, groups // chains, SUBLANES, q)
    return reduce_fn(reduce_fn(y, axis=1), axis=0)


def _fold_keys(m, key_axis):
    if key_axis == 1:
        out = m[:, :LANES]
        for t in range(1, m.shape[1] // LANES):
            out = out + m[:, t * LANES:(t + 1) * LANES]
        return out
    return _reduce_rows(m, jnp.sum)


_VALUE_BISECTIONS = 12
_KEY_BITS = 32


def _kth_largest_threshold(read_block, n_blocks, rows, k, active, key_axis=1, bounds=None):
    kf = float(k)
    qshape = (rows, 1) if key_axis == 1 else (1, rows)
    part = (rows, LANES) if key_axis == 1 else (SUBLANES, rows)
    value_steps = _VALUE_BISECTIONS if bounds is not None else 0
    max_steps = value_steps + _KEY_BITS + 2

    def count(preds):
        def body(c, accs):
            s = read_block(c)
            return tuple(a + _fold_keys(jnp.where(p(s), 1.0, 0.0), key_axis) for a, p in zip(accs, preds))
        accs = lax.fori_loop(0, n_blocks, body, tuple(jnp.zeros(part, F32) for _ in preds))
        return [jnp.sum(a, axis=key_axis, keepdims=True) for a in accs]

    def cond(st):
        return st[5] > 0

    def body(st):
        it, lo, hi, cnt_lo, open_, _ = st
        mid = (lo >> 1) + (hi >> 1) + (lo & hi & 1)
        if value_steps:
            vmid = _key(0.5 * _unkey(lo) + 0.5 * _unkey(hi))
            vmid = jnp.minimum(jnp.maximum(vmid, lo + 1), jnp.maximum(hi - 1, lo + 1))
            mid = jnp.where(jnp.logical_and(it < value_steps, positive), vmid, mid)
        cand = _unkey(mid)
        cnt, = count([lambda s: s >= cand])
        ge = jnp.logical_and(open_ > 0, cnt >= kf)
        lt = jnp.logical_and(open_ > 0, cnt < kf)
        lo = jnp.where(ge, mid, lo)
        cnt_lo = jnp.where(ge, cnt, cnt_lo)
        hi = jnp.where(lt, mid, hi)
        settled = (cnt_lo == kf) | (hi == lo + 1)
        open_ = jnp.where(settled, 0, open_)
        go = jnp.where(it < max_steps, jnp.max(open_), 0)
        return it + 1, lo, hi, cnt_lo, open_, go

    if bounds is None:
        lo0 = jnp.full(qshape, _KEY_NEG_INF, I32)
        hi0 = jnp.full(qshape, _KEY_POS_INF, I32)
        cnt0 = jnp.full(qshape, -1.0, F32)
        open0 = active
    else:
        n_gt0, n_ge0 = count([lambda s: s > 0.0, lambda s: s >= 0.0])
        positive = n_gt0 >= kf
        non_negative = n_ge0 >= kf
        key_zero = _key(jnp.zeros(qshape, F32))
        lo0 = jnp.where(non_negative, key_zero, _key(bounds[0]))
        cnt0 = jnp.where(non_negative, n_ge0, -1.0)
        hi0 = jnp.where(positive, _key(bounds[1]) + 1, jnp.where(non_negative, key_zero + 1, key_zero))
        lo0 = jnp.where(active, lo0, _KEY_NEG_INF)
        hi0 = jnp.where(active, hi0, _KEY_POS_INF)
        open0 = jnp.logical_and(active, jnp.logical_not((cnt0 == kf) | (hi0 == lo0 + 1)))
    open0 = jnp.where(open0, 1, 0).astype(I32)
    _, lo, _, _, _, _ = lax.while_loop(cond, body, (jnp.int32(0), lo0, hi0, cnt0, open0, jnp.max(open0)))
    thr = jnp.where(active, _unkey(lo), -jnp.inf)
    n_gt, n_ge = count([lambda s: s > thr, lambda s: s >= thr])
    need = jnp.where(active, kf - n_gt, 0.0)
    ties = jnp.where(jnp.logical_and(active, (n_ge - n_gt) > need), 1, 0).astype(I32)
    return thr, need, ties


def _earlier_matrix(w, key_axis):
    r_i = lax.broadcasted_iota(I32, (w, w), 0)
    c_i = lax.broadcasted_iota(I32, (w, w), 1)
    return jnp.where(r_i < c_i if key_axis == 1 else c_i < r_i, 1.0, 0.0).astype(BF16)


def _selection_bias(s, thr, need, tie_flag, tie_cnt_ref, bias_ref, earlier_ref, key_axis=1):
    w = s.shape[key_axis]

    @pl.when(tie_flag == 0)
    def _():
        sel = jnp.logical_and(s >= thr, s > -jnp.inf)
        bias_ref[...] = jnp.where(sel, 0.0, NEG_BIG)

    @pl.when(tie_flag != 0)
    def _():
        eq = s == thr
        eqf = jnp.where(eq, 1.0, 0.0)
        earlier = earlier_ref[:w, :w]
        if key_axis == 1:
            rank = jnp.dot(eqf.astype(BF16), earlier, preferred_element_type=F32)
        else:
            rank = jnp.dot(earlier, eqf.astype(BF16), preferred_element_type=F32)
        rank = rank + tie_cnt_ref[...]
        sel = jnp.logical_or(s > thr, jnp.logical_and(eq, rank < need))
        bias_ref[...] = jnp.where(sel, 0.0, NEG_BIG)
        tie_cnt_ref[...] = tie_cnt_ref[...] + jnp.sum(eqf, axis=key_axis, keepdims=True)


def _dsa_prompt_kernel(qb_ref, qi_ref, wi_ref, ki_ref, k_hbm, vt_hbm, o_ref,
                       k_scr, vt_scr, sc_scr, qm_scr, m_scr, l_scr, acc_scr, bias_scr, tie_scr, earlier_scr, sem,
                       *, q_blk, k_blk, seq, n_sel):
    b = pl.program_id(0)
    i = pl.program_id(1)

    @pl.when(i == 0)
    def _():
        ck = pltpu.make_async_copy(k_hbm.at[pl.ds(b * seq, seq)], k_scr, sem.at[0])
        cv = pltpu.make_async_copy(vt_hbm.at[:, pl.ds(b * seq, seq)], vt_scr, sem.at[1])
        ck.start()
        cv.start()
        ck.wait()
        cv.wait()

    q0 = i * q_blk
    n_blocks = (q0 + q_blk + k_blk - 1) // k_blk
    q_pos = q0 + lax.broadcasted_iota(I32, (1, q_blk), 1)

    qit = qi_ref[...].T.astype(BF16)
    qit_h = [qit[h * IDX_DIM:(h + 1) * IDX_DIM] for h in range(IDX_HEADS)]
    wit = wi_ref[...].T
    w_h = [wit[_WI_LANE + h:_WI_LANE + h + 1] for h in range(IDX_HEADS)]

    def score_body(c, carry):
        off = pl.multiple_of(c * k_blk, k_blk)
        kc = ki_ref[pl.ds(off, k_blk), :]
        s = w_h[0] * jnp.maximum(jnp.dot(kc, qit_h[0], preferred_element_type=F32), 0.0)
        for h in range(1, IDX_HEADS):
            s = s + w_h[h] * jnp.maximum(jnp.dot(kc, qit_h[h], preferred_element_type=F32), 0.0)
        k_pos = off + lax.broadcasted_iota(I32, (k_blk, 1), 0)
        adm = k_pos <= q_pos
        sc_scr[pl.ds(off, k_blk), :] = jnp.where(adm, s, -jnp.inf)
        hi_part, lo_part = carry
        hi_part = jnp.maximum(hi_part, _reduce_rows(jnp.where(adm, s, -jnp.inf), jnp.max))
        lo_part = jnp.minimum(lo_part, _reduce_rows(jnp.where(adm, s, jnp.inf), jnp.min))
        return hi_part, lo_part

    hi_part, lo_part = lax.fori_loop(
        0, n_blocks, score_body,
        (jnp.full((SUBLANES, q_blk), -jnp.inf, F32), jnp.full((SUBLANES, q_blk), jnp.inf, F32)))
    s_max = jnp.max(hi_part, axis=0, keepdims=True)
    s_min = jnp.min(lo_part, axis=0, keepdims=True)

    def read_block(c):
        return sc_scr[pl.ds(pl.multiple_of(c * k_blk, k_blk), k_blk), :]

    active = (q_pos + 1) > n_sel
    thr, need, ties = _kth_largest_threshold(read_block, n_blocks, q_blk, n_sel, active, key_axis=0,
                                             bounds=(s_min, s_max))
    tie_flag = jnp.max(ties)

    qt = (qb_ref[...] * (DSA_DH ** -0.5)).T
    row_lo = lax.broadcasted_iota(I32, (LANES, 1), 0) < DSA_DH
    for p in range(DSA_HEADS // 2):
        qp = qt[p * LANES:(p + 1) * LANES]
        qm_scr[2 * p] = jnp.where(row_lo, qp, 0.0).astype(BF16)
        qm_scr[2 * p + 1] = jnp.where(row_lo, 0.0, qp).astype(BF16)
    m_scr[...] = jnp.full(m_scr.shape, NEG_BIG, F32)
    l_scr[...] = jnp.zeros(l_scr.shape, F32)
    acc_scr[...] = jnp.zeros(acc_scr.shape, F32)
    tie_scr[...] = jnp.zeros(tie_scr.shape, F32)

    @pl.when(tie_flag != 0)
    def _():
        earlier_scr[...] = _earlier_matrix(k_blk, 0)

    def att_body(j, carry):
        off = pl.multiple_of(j * k_blk, k_blk)
        _selection_bias(sc_scr[pl.ds(off, k_blk), :], thr, need, tie_flag, tie_scr, bias_scr, earlier_scr,
                        key_axis=0)
        bias = bias_scr[...]

        def qk(h):
            p = h // 2
            kp = k_scr[pl.ds(off, k_blk), p * LANES:(p + 1) * LANES]
            return jnp.dot(kp, qm_scr[h], preferred_element_type=F32)

        prs, alphas = [], []
        for h in range(DSA_HEADS):
            logit = qk(h) + bias
            m_old = m_scr[h]
            m_new = jnp.maximum(m_old, jnp.max(_reduce_rows(logit, jnp.max), axis=0, keepdims=True))
            alpha = jnp.exp(m_old - m_new)
            pe = jnp.exp(logit - m_new)
            l_scr[h] = alpha * l_scr[h] + jnp.sum(_reduce_rows(pe, jnp.sum), axis=0, keepdims=True)
            m_scr[h] = m_new
            prs.append(pe.astype(BF16))
            alphas.append(alpha)
        for h in range(DSA_HEADS):
            rows = slice(h * DSA_DH, (h + 1) * DSA_DH)
            vth = vt_scr[rows, pl.ds(off, k_blk)]
            acc_scr[rows, :] = alphas[h] * acc_scr[rows, :] + jnp.dot(vth, prs[h], preferred_element_type=F32)
        return carry

    lax.fori_loop(0, n_blocks, att_body, 0)

    for h in range(DSA_HEADS):
        rows = slice(h * DSA_DH, (h + 1) * DSA_DH)
        acc_scr[rows, :] = acc_scr[rows, :] / l_scr[h]
    o_ref[...] = acc_scr[...].T


def _dsa_prompt(qb, qi, wi, ki16, k16, vt16, batch, seq, q_blk, k_blk):
    n_sel = min(DSA_TOPK, seq // 4)
    nq = seq // q_blk
    qspec = lambda w: pl.BlockSpec((q_blk, w), lambda b, i: (b * nq + i, 0))
    return pl.pallas_call(
        functools.partial(_dsa_prompt_kernel, q_blk=q_blk, k_blk=k_blk, seq=seq, n_sel=n_sel),
        grid=(batch, nq),
        in_specs=[qspec(DSA_W), qspec(IDX_W), qspec(LANES),
                  pl.BlockSpec((seq, IDX_DIM), lambda b, i: (b, 0)),
                  pl.BlockSpec(memory_space=pl.ANY), pl.BlockSpec(memory_space=pl.ANY)],
        out_specs=qspec(DSA_W),
        out_shape=jax.ShapeDtypeStruct((batch * seq, DSA_W), F32),
        scratch_shapes=[pltpu.VMEM((seq, DSA_W), BF16), pltpu.VMEM((DSA_W, seq), BF16),
                        pltpu.VMEM((seq, q_blk), F32),
                        pltpu.VMEM((DSA_HEADS, LANES, q_blk), BF16),
                        pltpu.VMEM((DSA_HEADS, 1, q_blk), F32), pltpu.VMEM((DSA_HEADS, 1, q_blk), F32),
                        pltpu.VMEM((DSA_W, q_blk), F32), pltpu.VMEM((k_blk, q_blk), F32),
                        pltpu.VMEM((1, q_blk), F32), pltpu.VMEM((k_blk, k_blk), BF16),
                        pltpu.SemaphoreType.DMA((2,))],
        compiler_params=_params(("arbitrary", "arbitrary")),
        name="dsa_prompt",
    )(qb, qi, wi, ki16, k16, vt16)


PAGES_PER_STEP = 8
ATTN_PARTS = 2
SCORE_PAGES_PER_STEP = 32
_STEP_KEYS = PAGES_PER_STEP * PAGE_SIZE
_NT = (((1,), (1,)), ((), ()))


def _page_specs(page_shape, n_pages, per_step, clamp_last):
    specs = []
    zeros = (0,) * len(page_shape)
    for s in range(per_step):
        def imap(b, c, pt, s=s):
            page = jnp.minimum(c * per_step + s, n_pages - per_step + s) if clamp_last else c * per_step + s
            return (pt[b, page],) + zeros
        specs.append(pl.BlockSpec((1,) + page_shape, imap))
    return specs


def _stack_heads(qi):
    return jnp.concatenate([qi[:, h * IDX_DIM:(h + 1) * IDX_DIM] for h in range(IDX_HEADS)], axis=0)


def _idx_score(s4, wi, t):
    s = wi[:, _WI_LANE:_WI_LANE + 1] * jnp.maximum(s4[:t], 0.0)
    for h in range(1, IDX_HEADS):
        s = s + wi[:, _WI_LANE + h:_WI_LANE + h + 1] * jnp.maximum(s4[h * t:(h + 1) * t], 0.0)
    return s


def _dsa_sample_score_kernel(pt_ref, qi_ref, wi_ref, kin_ref, *rest, t, past):
    pages = rest[:SCORE_PAGES_PER_STEP]
    sc_ref, = rest[SCORE_PAGES_PER_STEP:]
    c = pl.program_id(1)
    nch = pl.num_programs(1)
    step_keys = SCORE_PAGES_PER_STEP * PAGE_SIZE
    qi4 = _stack_heads(qi_ref[...]).astype(BF16)
    wi = wi_ref[...]
    keys_t = jnp.concatenate([p[0] for p in pages], axis=1).astype(BF16)
    off = pl.multiple_of(c * step_keys, step_keys)
    sc_ref[0, :, pl.ds(off, step_keys)] = _idx_score(
        jnp.dot(qi4, keys_t, preferred_element_type=F32), wi, t)

    @pl.when(c == nch - 1)
    def _():
        knew = jnp.concatenate([kin_ref[...], jnp.zeros((LANES - t, IDX_DIM), F32)], axis=0).astype(BF16)
        s = _idx_score(lax.dot_general(qi4, knew, _NT, preferred_element_type=F32), wi, t)
        col = lax.broadcasted_iota(I32, (t, LANES), 1)
        row = lax.broadcasted_iota(I32, (t, LANES), 0)
        sc_ref[0, :, past:past + LANES] = jnp.where(col <= row, s, -jnp.inf)


THRESHOLD_GROUP = 8


def _dsa_sample_threshold_kernel(sc_ref, thr_ref, need_ref, flag_ref, *, t, past, n_sel):
    g = sc_ref.shape[0]
    rows = g * t

    def scores(_):
        return sc_ref[...].reshape(rows, sc_ref.shape[2])

    s = scores(0)
    s_max = jnp.max(s, axis=1, keepdims=True)
    s_min = jnp.min(jnp.where(s > -jnp.inf, s, jnp.inf), axis=1, keepdims=True)
    q_in_seq = lax.broadcasted_iota(I32, (g, t, 1), 1).reshape(rows, 1)
    active = (past + 1 + q_in_seq) > n_sel
    thr, need, ties = _kth_largest_threshold(scores, 1, rows, n_sel, active, bounds=(s_min, s_max))
    thr_ref[...] = thr.reshape(g, t, 1)
    need_ref[...] = need.reshape(g, t, 1)
    flag_ref[...] = ties.astype(F32).reshape(g, t, 1)


def _dsa_sample_attn_kernel(pt_ref, qb_ref, sc_ref, thr_ref, need_ref, flag_ref, kn_ref, vn_ref, *rest,
                            t, past):
    kpages = rest[:PAGES_PER_STEP]
    vpages = rest[PAGES_PER_STEP:2 * PAGES_PER_STEP]
    o_ref, m_scr, l_scr, acc_scr, bias_scr, biasn_scr, tie_scr, earlier_scr = rest[2 * PAGES_PER_STEP:]
    c = pl.program_id(1)
    nch = pl.num_programs(1) - 1
    flag = jnp.max(flag_ref[0]).astype(I32)

    @pl.when(jnp.logical_and(c == 0, flag != 0))
    def _():
        earlier_scr[...] = _earlier_matrix(_STEP_KEYS, 1)

    @pl.when(c == 0)
    def _():
        m_scr[...] = jnp.full(m_scr.shape, NEG_BIG, F32)
        l_scr[...] = jnp.zeros(l_scr.shape, F32)
        acc_scr[...] = jnp.zeros(acc_scr.shape, F32)
        tie_scr[...] = jnp.zeros(tie_scr.shape, F32)

    rows = DSA_HEADS * t
    blockmask = (lax.broadcasted_iota(I32, (rows, DSA_W), 0) // t
                 == lax.broadcasted_iota(I32, (rows, DSA_W), 1) // DSA_DH)
    qs = qb_ref[...] * (DSA_DH ** -0.5)
    qbd = jnp.where(blockmask, jnp.concatenate([qs] * DSA_HEADS, axis=0), 0.0).astype(BF16)
    thr = thr_ref[0]
    need = need_ref[0]

    def probs(part, bias, k_op, keys_on_lanes):
        if keys_on_lanes:
            logit = jnp.dot(qbd, k_op, preferred_element_type=F32)
        else:
            logit = lax.dot_general(qbd, k_op, _NT, preferred_element_type=F32)
        logit = logit + jnp.concatenate([bias] * DSA_HEADS, axis=0)
        m_old = m_scr[part]
        m_new = jnp.maximum(m_old, jnp.max(logit, axis=-1, keepdims=True))
        alpha = jnp.exp(m_old - m_new)
        pr = jnp.exp(logit - m_new).astype(BF16)
        l_scr[part] = alpha * l_scr[part] + jnp.sum(pr.astype(F32), axis=-1, keepdims=True)
        m_scr[part] = m_new
        return pr, alpha

    def accumulate(part, pr, alpha, v_op, keys_on_lanes):
        if keys_on_lanes:
            pv = lax.dot_general(pr, v_op, _NT, preferred_element_type=F32)
        else:
            pv = jnp.dot(pr, v_op, preferred_element_type=F32)
        acc_scr[part] = alpha * acc_scr[part] + pv

    @pl.when(c < nch)
    def _():
        off = pl.multiple_of(c * _STEP_KEYS, _STEP_KEYS)
        _selection_bias(sc_ref[0, :, pl.ds(off, _STEP_KEYS)], thr, need, flag, tie_scr, bias_scr, earlier_scr)
        bias = bias_scr[...]
        per = PAGES_PER_STEP // ATTN_PARTS
        stats = []
        for part in range(ATTN_PARTS):
            kc = jnp.concatenate([p[0].reshape(DSA_W, PAGE_SIZE)
                                  for p in kpages[part * per:(part + 1) * per]], axis=1).astype(BF16)
            stats.append(probs(part, bias[:, part * per * PAGE_SIZE:(part + 1) * per * PAGE_SIZE], kc, True))
        for part in range(ATTN_PARTS):
            vc = jnp.concatenate([p[0].reshape(DSA_W, PAGE_SIZE)
                                  for p in vpages[part * per:(part + 1) * per]], axis=1).astype(BF16)
            accumulate(part, stats[part][0], stats[part][1], vc, True)

    @pl.when(c == nch)
    def _():
        _selection_bias(sc_ref[0, :, past:past + LANES], thr, need, flag, tie_scr, biasn_scr, earlier_scr)
        zpad = jnp.zeros((LANES - t, DSA_W), F32)
        kc = jnp.concatenate([kn_ref[...], zpad], axis=0).astype(BF16)
        vc = jnp.concatenate([vn_ref[...], zpad], axis=0).astype(BF16)
        pr, alpha = probs(0, biasn_scr[...], kc, False)
        accumulate(0, pr, alpha, vc, False)
        m_all = m_scr[0]
        for part in range(1, ATTN_PARTS):
            m_all = jnp.maximum(m_all, m_scr[part])
        l_all = jnp.zeros_like(m_all)
        acc_all = jnp.zeros(acc_scr.shape[1:], F32)
        for part in range(ATTN_PARTS):
            w_part = jnp.exp(m_scr[part] - m_all)
            l_all = l_all + w_part * l_scr[part]
            acc_all = acc_all + w_part * acc_scr[part]
        outn = jnp.where(blockmask, acc_all / l_all, 0.0)
        out = outn[:t]
        for h in range(1, DSA_HEADS):
            out = out + outn[h * t:(h + 1) * t]
        o_ref[...] = out


def _dsa_sample(qb, qi, wi, ki_new, k_new, v_new, cache_k, cache_v, cache_idx_k, page_table, batch, t):
    n_pages = page_table.shape[1]
    past = n_pages * PAGE_SIZE
    n_sel = min(DSA_TOPK, (past + t) // 4)
    nch = n_pages // PAGES_PER_STEP
    nch_score = n_pages // SCORE_PAGES_PER_STEP
    lp = past + LANES
    kv_page = (DSA_HEADS, DSA_DH, PAGE_SIZE)
    ck = jnp.transpose(cache_k, (0, 2, 3, 1))
    cv = jnp.transpose(cache_v, (0, 2, 3, 1))
    cik = jnp.transpose(cache_idx_k, (0, 2, 1))
    rspec = lambda w: pl.BlockSpec((t, w), lambda b, c, pt: (b, 0))
    bspec = lambda w: pl.BlockSpec((1, t, w), lambda b, c, pt: (b, 0, 0))
    scores = pl.pallas_call(
        functools.partial(_dsa_sample_score_kernel, t=t, past=past),
        grid_spec=pltpu.PrefetchScalarGridSpec(
            num_scalar_prefetch=1,
            grid=(batch, nch_score),
            in_specs=[rspec(IDX_W), rspec(LANES), rspec(IDX_DIM)]
            + _page_specs((IDX_DIM, PAGE_SIZE), n_pages, SCORE_PAGES_PER_STEP, False),
            out_specs=bspec(lp),
        ),
        out_shape=jax.ShapeDtypeStruct((batch, t, lp), F32),
        compiler_params=_params(("arbitrary", "arbitrary")),
        name="dsa_sample_score",
    )(page_table, qi, wi, ki_new, *([cik] * SCORE_PAGES_PER_STEP))
    grp = THRESHOLD_GROUP if batch % THRESHOLD_GROUP == 0 else 1
    gspec = lambda w: pl.BlockSpec((grp, t, w), lambda i: (i, 0, 0))
    thr, need, flag = pl.pallas_call(
        functools.partial(_dsa_sample_threshold_kernel, t=t, past=past, n_sel=n_sel),
        grid=(batch // grp,),
        in_specs=[gspec(lp)],
        out_specs=[gspec(1)] * 3,
        out_shape=[jax.ShapeDtypeStruct((batch, t, 1), F32)] * 3,
        compiler_params=_params(("parallel",)),
        name="dsa_sample_threshold",
    )(scores)
    return pl.pallas_call(
        functools.partial(_dsa_sample_attn_kernel, t=t, past=past),
        grid_spec=pltpu.PrefetchScalarGridSpec(
            num_scalar_prefetch=1,
            grid=(batch, nch + 1),
            in_specs=[rspec(DSA_W), bspec(lp), bspec(1), bspec(1), bspec(1), rspec(DSA_W), rspec(DSA_W)]
            + _page_specs(kv_page, n_pages, PAGES_PER_STEP, True)
            + _page_specs(kv_page, n_pages, PAGES_PER_STEP, True),
            out_specs=rspec(DSA_W),
            scratch_shapes=[pltpu.VMEM((ATTN_PARTS, DSA_HEADS * t, 1), F32),
                            pltpu.VMEM((ATTN_PARTS, DSA_HEADS * t, 1), F32),
                            pltpu.VMEM((ATTN_PARTS, DSA_HEADS * t, DSA_W), F32), pltpu.VMEM((t, _STEP_KEYS), F32),
                            pltpu.VMEM((t, LANES), F32), pltpu.VMEM((t, 1), F32),
                            pltpu.VMEM((_STEP_KEYS, _STEP_KEYS), BF16)],
        ),
        out_shape=jax.ShapeDtypeStruct((batch * t, DSA_W), F32),
        compiler_params=_params(("arbitrary", "arbitrary")),
        name="dsa_sample_attn",
    )(page_table, qb, scores, thr, need, flag, k_new, v_new,
      *([ck] * PAGES_PER_STEP), *([cv] * PAGES_PER_STEP))


MOE_ROWS = 256
ROUTE_TILE = 512
TOKEN_TILE = 256


def _split_bf16(a):
    hi = a.astype(BF16)
    lo = (a - hi.astype(F32)).astype(BF16)
    return hi, lo


def _router_kernel(x_ref, whi_ref, wlo_ref, b_ref, idx_ref, gate_ref, rank_ref, cnt_ref, carry_scr):
    i = pl.program_id(0)
    tm = x_ref.shape[0]

    @pl.when(i == 0)
    def _():
        carry_scr[...] = jnp.zeros(carry_scr.shape, F32)

    xhi, xlo = _split_bf16(x_ref[...])
    whi = whi_ref[...]
    logits = (jnp.dot(xhi, whi, preferred_element_type=F32)
              + jnp.dot(xlo, whi, preferred_element_type=F32)
              + jnp.dot(xhi, wlo_ref[...], preferred_element_type=F32)) + b_ref[...]
    lane = lax.broadcasted_iota(I32, (tm, N_EXPERTS), 1)
    slot = lax.broadcasted_iota(I32, (tm, TOP_K), 1)
    vals, idxs = [], []
    cur = logits
    for _ in range(TOP_K):
        m = jnp.max(cur, axis=-1, keepdims=True)
        ix = jnp.min(jnp.where(cur == m, lane, N_EXPERTS), axis=-1, keepdims=True)
        vals.append(m)
        idxs.append(ix)
        cur = jnp.where(lane == ix, -jnp.inf, cur)
    es = [jnp.exp(v - vals[0]) for v in vals]
    denom = es[0] + es[1] + es[2] + es[3]
    onehot = jnp.zeros((tm, N_EXPERTS), F32)
    for ix in idxs:
        onehot = onehot + jnp.where(lane == ix, 1.0, 0.0)
    earlier = (lax.broadcasted_iota(I32, (tm, tm), 1) < lax.broadcasted_iota(I32, (tm, tm), 0))
    excl = jnp.dot(jnp.where(earlier, 1.0, 0.0).astype(BF16), onehot.astype(BF16),
                   preferred_element_type=F32) + carry_scr[...]
    idx_out = jnp.zeros((tm, TOP_K), I32)
    gate_out = jnp.zeros((tm, TOP_K), F32)
    rank_out = jnp.zeros((tm, TOP_K), F32)
    for k in range(TOP_K):
        rk = jnp.sum(jnp.where(lane == idxs[k], excl, 0.0), axis=-1, keepdims=True)
        idx_out = jnp.where(slot == k, idxs[k], idx_out)
        gate_out = jnp.where(slot == k, es[k] / denom, gate_out)
        rank_out = jnp.where(slot == k, rk, rank_out)
    idx_ref[...] = idx_out
    gate_ref[...] = gate_out
    rank_ref[...] = rank_out.astype(I32)
    total = carry_scr[...] + jnp.sum(onehot, axis=0, keepdims=True)
    carry_scr[...] = total
    cnt_ref[...] = total


def _router(x, w_router, b_router):
    n = x.shape[0]
    tm = ROUTE_TILE if n % ROUTE_TILE == 0 else TOKEN_TILE
    whi, wlo = _split_bf16(w_router)
    b2 = b_router.reshape(1, N_EXPERTS)
    kspec = pl.BlockSpec((tm, TOP_K), lambda i: (i, 0))
    return pl.pallas_call(
        _router_kernel,
        grid=(n // tm,),
        in_specs=[_row_spec(tm, D_MODEL), _full_spec(whi), _full_spec(wlo), _full_spec(b2)],
        out_specs=[kspec, kspec, kspec, pl.BlockSpec((1, N_EXPERTS), lambda i: (0, 0))],
        out_shape=[jax.ShapeDtypeStruct((n, TOP_K), I32), jax.ShapeDtypeStruct((n, TOP_K), F32),
                   jax.ShapeDtypeStruct((n, TOP_K), I32), jax.ShapeDtypeStruct((1, N_EXPERTS), F32)],
        scratch_shapes=[pltpu.VMEM((1, N_EXPERTS), F32)],
        compiler_params=_params(("arbitrary",)),
        name="router",
    )(x, whi, wlo, b2)


def _row_copy(src, src_row, dst, dst_row, sem):
    return pltpu.make_async_copy(src.at[pl.ds(src_row, 1)], dst.at[pl.ds(dst_row, 1)], sem)


def _dispatch_kernel(dest_ref, x_ref, xs_in, xs_out, sem):
    del xs_in
    tm = x_ref.shape[0]

    def issue(r, c):
        for k in range(TOP_K):
            _row_copy(x_ref, r, xs_out, dest_ref[r * TOP_K + k], sem).start(priority=k % 2)
        return c

    lax.fori_loop(0, tm, issue, 0)
    for _ in range(TOP_K):
        pltpu.make_async_copy(x_ref, x_ref, sem).wait()


def _dispatch(x, dest_flat, n_rows):
    n = x.shape[0]
    tm = TOKEN_TILE
    return pl.pallas_call(
        _dispatch_kernel,
        grid=(n // tm,),
        in_specs=[pl.BlockSpec((tm * TOP_K,), lambda i: (i,), memory_space=pltpu.SMEM),
                  _row_spec(tm, D_MODEL), pl.BlockSpec(memory_space=pl.ANY)],
        out_specs=pl.BlockSpec(memory_space=pl.ANY),
        out_shape=jax.ShapeDtypeStruct((n_rows, D_MODEL), F32),
        scratch_shapes=[pltpu.SemaphoreType.DMA(())],
        input_output_aliases={2: 0},
        compiler_params=_params(("arbitrary",)),
        name="moe_dispatch",
    )(dest_flat, x, jnp.zeros((n_rows, D_MODEL), F32))


def _expert_kernel(be_ref, na_ref, x_ref, w1_ref, b1_ref, w2_ref, b2_ref, o_ref, w1b_scr, w2b_scr):
    i = pl.program_id(0)
    active = i < na_ref[0]
    changed = jnp.logical_or(i == 0, be_ref[i] != be_ref[jnp.maximum(i - 1, 0)])

    @pl.when(jnp.logical_and(active, changed))
    def _():
        w1b_scr[...] = w1_ref[0].astype(BF16)
        w2b_scr[...] = w2_ref[0].astype(BF16)

    @pl.when(active)
    def _():
        h = jnp.dot(x_ref[...].astype(BF16), w1b_scr[...], preferred_element_type=F32) + b1_ref[0]
        gt = jnp.minimum(h[:, :D_FF], SWIGLU_LIMIT)
        up = jnp.clip(h[:, D_FF:], -SWIGLU_LIMIT, SWIGLU_LIMIT)
        act = (up + 1.0) * gt * jax.nn.sigmoid(SWIGLU_ALPHA * gt)
        o_ref[...] = jnp.dot(act.astype(BF16), w2b_scr[...], preferred_element_type=F32) + b2_ref[0]

    @pl.when(jnp.logical_not(active))
    def _():
        o_ref[...] = jnp.zeros(o_ref.shape, F32)


def _experts(xs, block_e, n_active, w1, b1, w2, b2):
    n_rows = xs.shape[0]
    nblk = n_rows // MOE_ROWS
    w1 = w1.reshape(-1, D_MODEL, 2 * D_FF)
    w2 = w2.reshape(-1, D_FF, D_MODEL)
    b1r = b1.reshape(-1, 1, 2 * D_FF)
    b2r = b2.reshape(-1, 1, D_MODEL)
    last = lambda i, na: jnp.minimum(i, na[0] - 1)
    return pl.pallas_call(
        _expert_kernel,
        grid_spec=pltpu.PrefetchScalarGridSpec(
            num_scalar_prefetch=2,
            grid=(nblk,),
            in_specs=[pl.BlockSpec((MOE_ROWS, D_MODEL), lambda i, be, na: (last(i, na), 0)),
                      pl.BlockSpec((1, D_MODEL, 2 * D_FF), lambda i, be, na: (be[i], 0, 0)),
                      pl.BlockSpec((1, 1, 2 * D_FF), lambda i, be, na: (be[i], 0, 0)),
                      pl.BlockSpec((1, D_FF, D_MODEL), lambda i, be, na: (be[i], 0, 0)),
                      pl.BlockSpec((1, 1, D_MODEL), lambda i, be, na: (be[i], 0, 0))],
            out_specs=pl.BlockSpec((MOE_ROWS, D_MODEL), lambda i, be, na: (i, 0)),
            scratch_shapes=[pltpu.VMEM((D_MODEL, 2 * D_FF), BF16), pltpu.VMEM((D_FF, D_MODEL), BF16)],
        ),
        out_shape=jax.ShapeDtypeStruct((n_rows, D_MODEL), F32),
        compiler_params=_params(("arbitrary",)),
        name="moe_experts",
    )(block_e, n_active, xs, w1, b1r, w2, b2r)


def _combine_kernel(dest_ref, x_ref, gate_ref, g_ref, b_ref, yb_hbm, *rest, first_tiles):
    outs, (buf, sem) = rest[:-2], rest[-2:]
    tm = x_ref.shape[0]

    def issue(r, c):
        for k in range(TOP_K):
            _row_copy(yb_hbm, dest_ref[r * TOP_K + k], buf.at[k], r, sem).start(priority=k % 2)
        return c

    lax.fori_loop(0, tm, issue, 0)
    for k in range(TOP_K):
        pltpu.make_async_copy(buf.at[k], buf.at[k], sem).wait()
    gate = gate_ref[...]
    y = gate[:, 0:1] * buf[0]
    for k in range(1, TOP_K):
        y = y + gate[:, k:k + 1] * buf[k]
    res = _layer_norm(DN_ALPHA * x_ref[...] + y, g_ref[...], b_ref[...])
    if first_tiles is None:
        outs[0][...] = res
    else:
        i = pl.program_id(0)

        @pl.when(i < first_tiles)
        def _():
            outs[0][...] = res

        @pl.when(i >= first_tiles)
        def _():
            outs[1][...] = res


def _combine(x, yb, dest_flat, gate, g, b, split=None):
    n = x.shape[0]
    tm = TOKEN_TILE
    g2, b2 = g.reshape(1, D_MODEL), b.reshape(1, D_MODEL)
    if split is None:
        first_tiles = None
        out_specs = _row_spec(tm, D_MODEL)
        out_shape = jax.ShapeDtypeStruct((n, D_MODEL), F32)
    else:
        first_tiles = split // tm
        out_specs = [pl.BlockSpec((tm, D_MODEL), lambda i: (jnp.minimum(i, first_tiles - 1), 0)),
                     pl.BlockSpec((tm, D_MODEL), lambda i: (jnp.maximum(i - first_tiles, 0), 0))]
        out_shape = [jax.ShapeDtypeStruct((split, D_MODEL), F32), jax.ShapeDtypeStruct((n - split, D_MODEL), F32)]
    return pl.pallas_call(
        functools.partial(_combine_kernel, first_tiles=first_tiles),
        grid=(n // tm,),
        in_specs=[pl.BlockSpec((tm * TOP_K,), lambda i: (i,), memory_space=pltpu.SMEM),
                  _row_spec(tm, D_MODEL), pl.BlockSpec((tm, TOP_K), lambda i: (i, 0)),
                  _full_spec(g2), _full_spec(b2), pl.BlockSpec(memory_space=pl.ANY)],
        out_specs=out_specs,
        out_shape=out_shape,
        scratch_shapes=[pltpu.VMEM((TOP_K, tm, D_MODEL), F32), pltpu.SemaphoreType.DMA(())],
        compiler_params=_params(("arbitrary",)),
        name="moe_combine",
    )(dest_flat, x, gate, g2, b2, yb)


def _moe_layer(x, layer, w_router, b_router, w1, b1, w2, b2, g, b, split=None):
    n = x.shape[0]
    idx, gate, rank, counts = _router(x, w_router, b_router)
    nblk = (n * TOP_K) // MOE_ROWS + N_EXPERTS
    cnt = counts[0].astype(I32)
    padded = (cnt + MOE_ROWS - 1) // MOE_ROWS * MOE_ROWS
    pad_end = jnp.cumsum(padded)
    pad_start = pad_end - padded
    dest = (pad_start[idx] + rank).reshape(-1)
    n_active = (pad_end[-1] // MOE_ROWS).astype(I32)
    blk = jnp.arange(nblk, dtype=I32)
    blk = jnp.minimum(blk, n_active - 1)
    block_e = jnp.sum((pad_end[None, :] <= (blk * MOE_ROWS)[:, None]).astype(I32), axis=1)
    block_e = jnp.minimum(block_e, N_EXPERTS - 1) + layer * N_EXPERTS
    xs = _dispatch(x, dest, nblk * MOE_ROWS)
    yb = _experts(xs, block_e, n_active.reshape(1), w1, b1, w2, b2)
    return _combine(x, yb, dest, gate, g, b, split)


POOL_HALO = 16


def _pool_kernel(x_ref, halo_ref, w_ref, sc_ref, g_ref, b_ref, o_ref, ext_scr, *, tiles_per_seq, n_prev):
    tm = x_ref.shape[0]
    tile = pl.program_id(0) % tiles_per_seq
    x = x_ref[...]
    halo = halo_ref[...]
    if n_prev == 0:
        halo = jnp.where(tile == 0, 0.0, halo)
    ext_scr[0:POOL_HALO] = halo
    ext_scr[POOL_HALO:POOL_HALO + tm] = x
    pos = n_prev + tile * tm + lax.broadcasted_iota(I32, (tm, 1), 0)
    parts = []
    for g, w in enumerate(POOL_WINDOWS):
        cols = slice(g * POOL_GROUP, (g + 1) * POOL_GROUP)
        xg = x[:, cols]
        s = xg
        for d in range(1, w):
            s = s + ext_scr[POOL_HALO - d:POOL_HALO - d + tm, cols]
        cnt = jnp.minimum(pos + 1, w).astype(F32)
        dg = s / cnt - xg
        parts.append(jnp.dot(dg.astype(BF16), w_ref[g], preferred_element_type=F32))
    mix = jnp.concatenate(parts, axis=1) * sc_ref[...]
    o_ref[...] = _layer_norm(DN_ALPHA * x + mix, g_ref[...], b_ref[...])


def _pool(x, halo_src, pool_w, pool_scale, g, b, tm, tiles_per_seq, n_prev, halo_map, n, row_off=0):
    w16 = pool_w.astype(BF16)
    sc = pool_scale.reshape(1, D_MODEL)
    g2, b2 = g.reshape(1, D_MODEL), b.reshape(1, D_MODEL)
    blk_off = row_off // tm
    return pl.pallas_call(
        functools.partial(_pool_kernel, tiles_per_seq=tiles_per_seq, n_prev=n_prev),
        grid=(n // tm,),
        in_specs=[pl.BlockSpec((tm, D_MODEL), lambda i: (i + blk_off, 0)),
                  pl.BlockSpec((POOL_HALO, D_MODEL), halo_map),
                  _full_spec(w16), _full_spec(sc), _full_spec(g2), _full_spec(b2)],
        out_specs=_row_spec(tm, D_MODEL),
        out_shape=jax.ShapeDtypeStruct((n, D_MODEL), F32),
        scratch_shapes=[pltpu.VMEM((POOL_HALO + tm, D_MODEL), F32)],
        compiler_params=_params(("parallel",)),
        name="pool",
    )(x, halo_src, w16, sc, g2, b2)


PROJ_TILE = 512
DSA_Q_BLOCK = 256
DSA_K_BLOCK = 512
POOL_TILE = 512


def kernel(x_prompt, x_sample, cache_k, cache_v, cache_idx_k, page_table, state_gla, state_pool,
           w_in, gla_fg_w2, gla_fg_b, gla_norm_g, idx_kn_g, idx_kn_b, w_out,
           pool_w, pool_scale, ln_mix_g, ln_mix_b, ln_ffn_g, ln_ffn_b,
           moe_router_w, moe_router_b, moe_w1, moe_b1, moe_w2, moe_b2):
    bp, sp, _ = x_prompt.shape
    bs, ts, _ = x_sample.shape
    n_p, n_s = bp * sp, bs * ts
    xp = x_prompt.reshape(n_p, D_MODEL)
    xs = x_sample.reshape(n_s, D_MODEL)
    proj_w = (w_in, gla_fg_w2, gla_fg_b, idx_kn_g, idx_kn_b)

    qa, ka, va, ga, lf, qb, kb, vb, qi, ki, wi, kb16, vt16, ki16 = _project(xp, *proj_w, tm=PROJ_TILE)
    oa_p, gla_p = _gla(qa, ka, va, lf, ga, jnp.zeros((bp, GLA_HEADS, GLA_DK, GLA_DV), F32), gla_norm_g, bp, sp)
    ob_p = _dsa_prompt(qb, qi, wi, ki16, kb16, vt16, bp, sp, DSA_Q_BLOCK, DSA_K_BLOCK)
    x1p = _merge(xp, oa_p, ob_p, w_out, ln_mix_g[0], ln_mix_b[0], PROJ_TILE)
    k_p = kb.reshape(bp, sp, DSA_HEADS, DSA_DH)
    v_p = vb.reshape(bp, sp, DSA_HEADS, DSA_DH)
    kidx_p = ki.reshape(bp, sp, IDX_DIM)

    qa, ka, va, ga, lf, qb, kb, vb, qi, ki, wi, _, _, _ = _project(xs, *proj_w, tm=n_s)
    oa_s, gla_s = _gla(qa, ka, va, lf, ga, state_gla, gla_norm_g, bs, ts)
    ob_s = _dsa_sample(qb, qi, wi, ki, kb, vb, cache_k, cache_v, cache_idx_k, page_table, bs, ts)
    x1s = _merge(xs, oa_s, ob_s, w_out, ln_mix_g[0], ln_mix_b[0], n_s)
    k_s = kb.reshape(bs, ts, DSA_HEADS, DSA_DH)
    v_s = vb.reshape(bs, ts, DSA_HEADS, DSA_DH)
    kidx_s = ki.reshape(bs, ts, IDX_DIM)

    x_all = _moe_layer(jnp.concatenate([x1p, x1s], axis=0), 0, moe_router_w[0], moe_router_b[0],
                       moe_w1, moe_b1, moe_w2, moe_b2, ln_ffn_g[0], ln_ffn_b[0])

    pool_p = jnp.stack([x_all[(s + 1) * sp - POOL_STATE:(s + 1) * sp] for s in range(bp)])
    xs3 = x_all[n_p:].reshape(bs, ts, D_MODEL)
    pool_s = jnp.concatenate([state_pool, xs3], axis=1)[:, -POOL_STATE:]
    per_seq = sp // POOL_TILE
    halo_step = POOL_TILE // POOL_HALO
    x2p = _pool(x_all, x_all, pool_w, pool_scale, ln_mix_g[1], ln_mix_b[1], POOL_TILE, per_seq, 0,
                lambda i: (jnp.maximum(i * halo_step - 1, 0), 0), n_p)
    halo_s = jnp.concatenate([jnp.zeros((bs, POOL_HALO - POOL_STATE, D_MODEL), F32), state_pool], axis=1)
    x2s = _pool(x_all, halo_s.reshape(bs * POOL_HALO, D_MODEL), pool_w, pool_scale, ln_mix_g[1], ln_mix_b[1],
                ts, 1, POOL_STATE, lambda i: (i, 0), n_s, row_off=n_p)
    y_p, y_s = _moe_layer(jnp.concatenate([x2p, x2s], axis=0), 1, moe_router_w[1], moe_router_b[1],
                          moe_w1, moe_b1, moe_w2, moe_b2, ln_ffn_g[1], ln_ffn_b[1], split=n_p)
    y_p = y_p.reshape(bp, sp, D_MODEL)
    y_s = y_s.reshape(bs, ts, D_MODEL)
    return (y_p, y_s, k_p, v_p, kidx_p, gla_p, pool_p, k_s, v_s, kidx_s, gla_s, pool_s)
```

```python
import functools
import math

import jax
import jax.numpy as jnp
from jax import lax
from jax.experimental import pallas as pl
from jax.experimental.pallas import tpu as pltpu

F32 = jnp.float32
BF16 = jnp.bfloat16
I32 = jnp.int32

D_MODEL = 1024
DEPTH = 2
PAGE_SIZE = 128
GLA_HEADS = 4
GLA_DK = 64
GLA_DV = 128
GLA_GATE_RANK = 16
GLA_TAU = 16.0
GLA_CHUNK = 64
GLA_TILE = 2 * GLA_CHUNK
DSA_HEADS = 8
DSA_DH = 64
IDX_HEADS = 4
IDX_DIM = 64
DSA_TOPK = 256
IDX_W_SCALE = (IDX_HEADS ** -0.5) * (IDX_DIM ** -0.5)
POOL_WINDOWS = (2, 4, 8, 16)
POOL_GROUP = D_MODEL // 4
POOL_STATE = 16 - 1
N_EXPERTS = 32
TOP_K = 4
D_FF = D_MODEL
SWIGLU_ALPHA = 1.702
SWIGLU_LIMIT = 7.0
DN_ALPHA = (2 * DEPTH) ** 0.25
LN_EPS = 1e-5
GLA_QK = GLA_HEADS * GLA_DK
GLA_V = GLA_HEADS * GLA_DV
DSA_W = DSA_HEADS * DSA_DH
IDX_W = IDX_HEADS * IDX_DIM

LANES = 128
SUBLANES = 8
VMEM_LIMIT_BYTES = 56 * 1024 * 1024

NEG_BIG = -1e30


def _params(sem, vmem=VMEM_LIMIT_BYTES):
    return pltpu.CompilerParams(dimension_semantics=sem, vmem_limit_bytes=vmem)


def _layer_norm(x, g, b):
    mu = jnp.mean(x, axis=-1, keepdims=True)
    xc = x - mu
    var = jnp.mean(xc * xc, axis=-1, keepdims=True)
    return xc * lax.rsqrt(var + LN_EPS) * g + b


def _row_spec(tm, w):
    return pl.BlockSpec((tm, w), lambda i: (i, 0))


def _full_spec(a):
    nd = a.ndim
    return pl.BlockSpec(a.shape, lambda i: (0,) * nd)


_MAIN_W = (GLA_QK, GLA_QK, GLA_V, GLA_V, DSA_W, DSA_W, DSA_W, IDX_W)
_MAIN_OFF = tuple(sum(_MAIN_W[:i]) for i in range(len(_MAIN_W) + 1))


def _proj_kernel(x_ref, wm_ref, ws_ref, fgw_ref, fgb_ref, kng_ref, knb_ref,
                 qa_ref, ka_ref, va_ref, ga_ref, lf_ref, qb_ref, kb_ref, vb_ref,
                 qi_ref, ki_ref, wi_ref, kb16_ref, vt16_ref, ki16_ref, *channel_major):
    xb = x_ref[...].astype(BF16)

    def mm(n):
        return jnp.dot(xb, wm_ref[:, _MAIN_OFF[n]:_MAIN_OFF[n + 1]], preferred_element_type=F32)

    qa_ref[...] = mm(0) * (GLA_DK ** -0.5)
    ka_ref[...] = mm(1)
    va_ref[...] = mm(2)
    ga_ref[...] = mm(3)
    qb_ref[...] = mm(4)
    kb = mm(5)
    kb_ref[...] = kb
    kb16_ref[...] = kb.astype(BF16)
    vb = mm(6)
    vb_ref[...] = vb
    vbt = vb.T
    vt16_ref[...] = vbt.astype(BF16)
    qi_ref[...] = mm(7)

    small = jnp.dot(xb, ws_ref[...], preferred_element_type=F32)
    ki = _layer_norm(small[:, :IDX_DIM], kng_ref[...], knb_ref[...])
    ki_ref[...] = ki
    ki16_ref[...] = ki.astype(BF16)
    if channel_major:
        kt_ref, vt_ref, kit_ref = channel_major
        kt_ref[0] = kb.T
        vt_ref[0] = vbt
        kit_ref[0] = jnp.concatenate([ki, jnp.zeros_like(ki)], axis=1).T[:IDX_DIM]
    fa = small[:, IDX_DIM:IDX_DIM + GLA_GATE_RANK]
    z = jnp.dot(fa, fgw_ref[...], preferred_element_type=F32,
                precision=lax.Precision.HIGHEST) + fgb_ref[...]
    lf_ref[...] = (jnp.minimum(z, 0.0) - jnp.log(1.0 + jnp.exp(-jnp.abs(z)))) * (1.0 / GLA_TAU)
    wi_ref[...] = small * IDX_W_SCALE


def _project(x2d, w_in, gla_fg_w2, gla_fg_b, idx_kn_g, idx_kn_b, tm, seq=None):
    n = x2d.shape[0]
    pts = [0]
    for s in (GLA_QK, GLA_QK, GLA_V, GLA_V, GLA_GATE_RANK, DSA_W, DSA_W, DSA_W, IDX_W, IDX_DIM, IDX_HEADS):
        pts.append(pts[-1] + s)
    seg = lambda i: w_in[:, pts[i]:pts[i + 1]]
    wm = jnp.concatenate([seg(0), seg(1), seg(2), seg(3), seg(5), seg(6), seg(7), seg(8)], axis=1).astype(BF16)
    pad = LANES - IDX_DIM - GLA_GATE_RANK - IDX_HEADS
    ws = jnp.concatenate([seg(9), seg(4), seg(10), jnp.zeros((D_MODEL, pad), w_in.dtype)], axis=1).astype(BF16)
    fgb = gla_fg_b.reshape(1, GLA_QK)
    kng = idx_kn_g.reshape(1, IDX_DIM)
    knb = idx_kn_b.reshape(1, IDX_DIM)
    widths = (GLA_QK, GLA_QK, GLA_V, GLA_V, GLA_QK, DSA_W, DSA_W, DSA_W, IDX_W, IDX_DIM, LANES)
    out_shape = [jax.ShapeDtypeStruct((n, w), F32) for w in widths]
    out_shape += [jax.ShapeDtypeStruct((n, DSA_W), BF16), jax.ShapeDtypeStruct((DSA_W, n), BF16),
                  jax.ShapeDtypeStruct((n, IDX_DIM), BF16)]
    out_specs = [_row_spec(tm, w) for w in widths]
    out_specs += [_row_spec(tm, DSA_W), pl.BlockSpec((DSA_W, tm), lambda i: (0, i)), _row_spec(tm, IDX_DIM)]
    if seq is not None:
        tps = seq // tm
        for w in (DSA_W, DSA_W, IDX_DIM):
            out_shape.append(jax.ShapeDtypeStruct((n // seq, w, seq), F32))
            out_specs.append(pl.BlockSpec((1, w, tm), lambda i: (i // tps, 0, i % tps)))
    return pl.pallas_call(
        _proj_kernel,
        grid=(n // tm,),
        in_specs=[_row_spec(tm, D_MODEL), _full_spec(wm), _full_spec(ws), _full_spec(gla_fg_w2),
                  _full_spec(fgb), _full_spec(kng), _full_spec(knb)],
        out_specs=out_specs,
        out_shape=out_shape,
        compiler_params=_params(("parallel",)),
        name="proj",
    )(x2d, wm, ws, gla_fg_w2, fgb, kng, knb)


def _gla_kernel(q_ref, k_ref, v_ref, lf_ref, ga_ref, s0_ref, ng_ref, o_ref, sout_ref, s_scr, *, chunk):
    c = pl.program_id(1)
    nc = pl.num_programs(1)

    @pl.when(c == 0)
    def _():
        s_scr[...] = s0_ref[0]

    g = lf_ref[...]
    ri = lax.broadcasted_iota(I32, (chunk, chunk), 0)
    ci = lax.broadcasted_iota(I32, (chunk, chunk), 1)
    causal = ci <= ri
    tri = jnp.where(causal, 1.0, 0.0).astype(F32)
    b = jnp.dot(tri, g, preferred_element_type=F32, precision=lax.Precision.HIGHEST)
    mid = chunk // 2
    b_mid = b[mid:mid + 1, :]
    b_last = b[chunk - 1:chunk, :]
    q = q_ref[...]
    k = k_ref[...]
    q_in = q * jnp.exp(b)
    q_rel = q * jnp.exp(b - b_mid)
    k_rel = k * jnp.exp(b_mid - b)
    k_out = k * jnp.exp(b_last - b)
    dec_last = jnp.exp(b_last)
    v = v_ref[...]
    ga = ga_ref[...]
    ng = ng_ref[...]
    eye = jnp.where(lax.broadcasted_iota(I32, (GLA_DK, GLA_DK), 0)
                    == lax.broadcasted_iota(I32, (GLA_DK, GLA_DK), 1), 1.0, 0.0).astype(F32)
    for h in range(GLA_HEADS):
        ks = slice(h * GLA_DK, (h + 1) * GLA_DK)
        vs = slice(h * GLA_DV, (h + 1) * GLA_DV)
        s_h = s_scr[h]
        vh = v[:, vs]
        inter = jnp.dot(q_in[:, ks], s_h, preferred_element_type=F32)
        att = lax.dot_general(q_rel[:, ks], k_rel[:, ks], (((1,), (1,)), ((), ())),
                              preferred_element_type=F32)
        att = jnp.where(causal, att, 0.0)
        o = inter + jnp.dot(att, vh, preferred_element_type=F32)
        kv = lax.dot_general(k_out[:, ks], vh, (((0,), (0,)), ((), ())), preferred_element_type=F32)
        s_scr[h] = jnp.dot(eye * dec_last[:, ks], s_h, preferred_element_type=F32,
                           precision=lax.Precision.HIGHEST) + kv
        ms = jnp.mean(o * o, axis=-1, keepdims=True)
        gh = ga[:, vs]
        o_ref[:, vs] = o * lax.rsqrt(ms + LN_EPS) * ng * (gh * jax.nn.sigmoid(gh))

    @pl.when(c == nc - 1)
    def _():
        sout_ref[0] = s_scr[...]


def _gla(qa, ka, va, lf, ga, s0, gla_norm_g, batch, seq):
    chunk = math.gcd(seq, GLA_TILE)
    nc = seq // chunk
    ng = gla_norm_g.reshape(1, GLA_DV)
    spec = lambda w: pl.BlockSpec((chunk, w), lambda b, c: (b * nc + c, 0))
    sspec = pl.BlockSpec((1, GLA_HEADS, GLA_DK, GLA_DV), lambda b, c: (b, 0, 0, 0))
    return pl.pallas_call(
        functools.partial(_gla_kernel, chunk=chunk),
        grid=(batch, nc),
        in_specs=[spec(GLA_QK), spec(GLA_QK), spec(GLA_V), spec(GLA_QK), spec(GLA_V), sspec,
                  pl.BlockSpec((1, GLA_DV), lambda b, c: (0, 0))],
        out_specs=[spec(GLA_V), sspec],
        out_shape=[jax.ShapeDtypeStruct((batch * seq, GLA_V), F32),
                   jax.ShapeDtypeStruct((batch, GLA_HEADS, GLA_DK, GLA_DV), F32)],
        scratch_shapes=[pltpu.VMEM((GLA_HEADS, GLA_DK, GLA_DV), F32)],
        compiler_params=_params(("parallel", "arbitrary")),
        name="gla",
    )(qa, ka, va, lf, ga, s0, ng)


def _merge_kernel(x_ref, oa_ref, ob_ref, w_ref, g_ref, b_ref, o_ref):
    mix = jnp.dot(oa_ref[...].astype(BF16), w_ref[:GLA_V, :], preferred_element_type=F32)
    mix = mix + jnp.dot(ob_ref[...].astype(BF16), w_ref[GLA_V:, :], preferred_element_type=F32)
    o_ref[...] = _layer_norm(DN_ALPHA * x_ref[...] + mix, g_ref[...], b_ref[...])


def _merge(x2d, oa, ob, w_out, g, b, tm):
    n = x2d.shape[0]
    w16 = w_out.astype(BF16)
    g2, b2 = g.reshape(1, D_MODEL), b.reshape(1, D_MODEL)
    return pl.pallas_call(
        _merge_kernel,
        grid=(n // tm,),
        in_specs=[_row_spec(tm, D_MODEL), _row_spec(tm, GLA_V), _row_spec(tm, DSA_W),
                  _full_spec(w16), _full_spec(g2), _full_spec(b2)],
        out_specs=_row_spec(tm, D_MODEL),
        out_shape=jax.ShapeDtypeStruct((n, D_MODEL), F32),
        compiler_params=_params(("parallel",)),
        name="merge",
    )(x2d, oa, ob, w16, g2, b2)


_KEY_NEG_INF = -2139095041
_KEY_POS_INF = 2139095040
_WI_LANE = IDX_DIM + GLA_GATE_RANK


def _unkey(kk):
    return lax.bitcast_convert_type(jnp.where(kk < 0, kk ^ 0x7FFFFFFF, kk), F32)


def _key(v):
    i = lax.bitcast_convert_type(v, I32)
    return jnp.where(i < 0, i ^ 0x7FFFFFFF, i)


_REDUCE_CHAINS = 8


def _reduce_rows(x, reduce_fn):
    r, q = x.shape
    groups = r // SUBLANES
    chains = _REDUCE_CHAINS if groups % _REDUCE_CHAINS == 0 else 1
    y = x.reshape(chains, groups // chains, SUBLANES, q)
    return reduce_fn(reduce_fn(y, axis=1), axis=0)


def _fold_keys(m, key_axis):
    if key_axis == 1:
        out = m[:, :LANES]
        for t in range(1, m.shape[1] // LANES):
            out = out + m[:, t * LANES:(t + 1) * LANES]
        return out
    return _reduce_rows(m, jnp.sum)


_VALUE_BISECTIONS = 12
_KEY_BITS = 32


def _kth_largest_threshold(read_block, n_blocks, rows, k, active, key_axis=1, bounds=None):
    kf = float(k)
    qshape = (rows, 1) if key_axis == 1 else (1, rows)
    part = (rows, LANES) if key_axis == 1 else (SUBLANES, rows)
    value_steps = _VALUE_BISECTIONS if bounds is not None else 0
    max_steps = value_steps + _KEY_BITS + 2

    def count(preds):
        def body(c, accs):
            s = read_block(c)
            return tuple(a + _fold_keys(jnp.where(p(s), 1.0, 0.0), key_axis) for a, p in zip(accs, preds))
        accs = lax.fori_loop(0, n_blocks, body, tuple(jnp.zeros(part, F32) for _ in preds))
        return [jnp.sum(a, axis=key_axis, keepdims=True) for a in accs]

    def cond(st):
        return st[5] > 0

    def body(st):
        it, lo, hi, cnt_lo, open_, _ = st
        mid = (lo >> 1) + (hi >> 1) + (lo & hi & 1)
        if value_steps:
            vmid = _key(0.5 * _unkey(lo) + 0.5 * _unkey(hi))
            vmid = jnp.minimum(jnp.maximum(vmid, lo + 1), jnp.maximum(hi - 1, lo + 1))
            mid = jnp.where(jnp.logical_and(it < value_steps, positive), vmid, mid)
        cand = _unkey(mid)
        cnt, = count([lambda s: s >= cand])
        ge = jnp.logical_and(open_ > 0, cnt >= kf)
        lt = jnp.logical_and(open_ > 0, cnt < kf)
        lo = jnp.where(ge, mid, lo)
        cnt_lo = jnp.where(ge, cnt, cnt_lo)
        hi = jnp.where(lt, mid, hi)
        settled = (cnt_lo == kf) | (hi == lo + 1)
        open_ = jnp.where(settled, 0, open_)
        go = jnp.where(it < max_steps, jnp.max(open_), 0)
        return it + 1, lo, hi, cnt_lo, open_, go

    if bounds is None:
        lo0 = jnp.full(qshape, _KEY_NEG_INF, I32)
        hi0 = jnp.full(qshape, _KEY_POS_INF, I32)
        cnt0 = jnp.full(qshape, -1.0, F32)
        open0 = active
    else:
        n_gt0, n_ge0 = count([lambda s: s > 0.0, lambda s: s >= 0.0])
        positive = n_gt0 >= kf
        non_negative = n_ge0 >= kf
        key_zero = _key(jnp.zeros(qshape, F32))
        lo0 = jnp.where(non_negative, key_zero, _key(bounds[0]))
        cnt0 = jnp.where(non_negative, n_ge0, -1.0)
        hi0 = jnp.where(positive, _key(bounds[1]) + 1, jnp.where(non_negative, key_zero + 1, key_zero))
        lo0 = jnp.where(active, lo0, _KEY_NEG_INF)
        hi0 = jnp.where(active, hi0, _KEY_POS_INF)
        open0 = jnp.logical_and(active, jnp.logical_not((cnt0 == kf) | (hi0 == lo0 + 1)))
    open0 = jnp.where(open0, 1, 0).astype(I32)
    _, lo, _, _, _, _ = lax.while_loop(cond, body, (jnp.int32(0), lo0, hi0, cnt0, open0, jnp.max(open0)))
    thr = jnp.where(active, _unkey(lo), -jnp.inf)
    n_gt, n_ge = count([lambda s: s > thr, lambda s: s >= thr])
    need = jnp.where(active, kf - n_gt, 0.0)
    ties = jnp.where(jnp.logical_and(active, (n_ge - n_gt) > need), 1, 0).astype(I32)
    return thr, need, ties


def _earlier_matrix(w, key_axis):
    r_i = lax.broadcasted_iota(I32, (w, w), 0)
    c_i = lax.broadcasted_iota(I32, (w, w), 1)
    return jnp.where(r_i < c_i if key_axis == 1 else c_i < r_i, 1.0, 0.0).astype(BF16)


def _selection_bias(s, thr, need, tie_flag, tie_cnt_ref, bias_ref, earlier_ref, key_axis=1):
    w = s.shape[key_axis]

    @pl.when(tie_flag == 0)
    def _():
        sel = jnp.logical_and(s >= thr, s > -jnp.inf)
        bias_ref[...] = jnp.where(sel, 0.0, NEG_BIG)

    @pl.when(tie_flag != 0)
    def _():
        eq = s == thr
        eqf = jnp.where(eq, 1.0, 0.0)
        earlier = earlier_ref[:w, :w]
        if key_axis == 1:
            rank = jnp.dot(eqf.astype(BF16), earlier, preferred_element_type=F32)
        else:
            rank = jnp.dot(earlier, eqf.astype(BF16), preferred_element_type=F32)
        rank = rank + tie_cnt_ref[...]
        sel = jnp.logical_or(s > thr, jnp.logical_and(eq, rank < need))
        bias_ref[...] = jnp.where(sel, 0.0, NEG_BIG)
        tie_cnt_ref[...] = tie_cnt_ref[...] + jnp.sum(eqf, axis=key_axis, keepdims=True)


def _dsa_prompt_kernel(qb_ref, qi_ref, wi_ref, ki_ref, k_hbm, vt_hbm, o_ref,
                       k_scr, vt_scr, sc_scr, qm_scr, m_scr, l_scr, acc_scr, bias_scr, tie_scr, earlier_scr, sem,
                       *, q_blk, k_blk, seq, n_sel):
    b = pl.program_id(0)
    i = pl.program_id(1)

    @pl.when(i == 0)
    def _():
        ck = pltpu.make_async_copy(k_hbm.at[pl.ds(b * seq, seq)], k_scr, sem.at[0])
        cv = pltpu.make_async_copy(vt_hbm.at[:, pl.ds(b * seq, seq)], vt_scr, sem.at[1])
        ck.start()
        cv.start()
        ck.wait()
        cv.wait()

    q0 = i * q_blk
    n_blocks = (q0 + q_blk + k_blk - 1) // k_blk
    q_pos = q0 + lax.broadcasted_iota(I32, (1, q_blk), 1)

    qit = qi_ref[...].T.astype(BF16)
    qit_h = [qit[h * IDX_DIM:(h + 1) * IDX_DIM] for h in range(IDX_HEADS)]
    wit = wi_ref[...].T
    w_h = [wit[_WI_LANE + h:_WI_LANE + h + 1] for h in range(IDX_HEADS)]

    def score_body(c, carry):
        off = pl.multiple_of(c * k_blk, k_blk)
        kc = ki_ref[pl.ds(off, k_blk), :]
        s = w_h[0] * jnp.maximum(jnp.dot(kc, qit_h[0], preferred_element_type=F32), 0.0)
        for h in range(1, IDX_HEADS):
            s = s + w_h[h] * jnp.maximum(jnp.dot(kc, qit_h[h], preferred_element_type=F32), 0.0)
        k_pos = off + lax.broadcasted_iota(I32, (k_blk, 1), 0)
        adm = k_pos <= q_pos
        sc_scr[pl.ds(off, k_blk), :] = jnp.where(adm, s, -jnp.inf)
        hi_part, lo_part = carry
        hi_part = jnp.maximum(hi_part, _reduce_rows(jnp.where(adm, s, -jnp.inf), jnp.max))
        lo_part = jnp.minimum(lo_part, _reduce_rows(jnp.where(adm, s, jnp.inf), jnp.min))
        return hi_part, lo_part

    hi_part, lo_part = lax.fori_loop(
        0, n_blocks, score_body,
        (jnp.full((SUBLANES, q_blk), -jnp.inf, F32), jnp.full((SUBLANES, q_blk), jnp.inf, F32)))
    s_max = jnp.max(hi_part, axis=0, keepdims=True)
    s_min = jnp.min(lo_part, axis=0, keepdims=True)

    def read_block(c):
        return sc_scr[pl.ds(pl.multiple_of(c * k_blk, k_blk), k_blk), :]

    active = (q_pos + 1) > n_sel
    thr, need, ties = _kth_largest_threshold(read_block, n_blocks, q_blk, n_sel, active, key_axis=0,
                                             bounds=(s_min, s_max))
    tie_flag = jnp.max(ties)

    qt = (qb_ref[...] * (DSA_DH ** -0.5)).T
    row_lo = lax.broadcasted_iota(I32, (LANES, 1), 0) < DSA_DH
    for p in range(DSA_HEADS // 2):
        qp = qt[p * LANES:(p + 1) * LANES]
        qm_scr[2 * p] = jnp.where(row_lo, qp, 0.0).astype(BF16)
        qm_scr[2 * p + 1] = jnp.where(row_lo, 0.0, qp).astype(BF16)
    m_scr[...] = jnp.full(m_scr.shape, NEG_BIG, F32)
    l_scr[...] = jnp.zeros(l_scr.shape, F32)
    acc_scr[...] = jnp.zeros(acc_scr.shape, F32)
    tie_scr[...] = jnp.zeros(tie_scr.shape, F32)

    @pl.when(tie_flag != 0)
    def _():
        earlier_scr[...] = _earlier_matrix(k_blk, 0)

    def att_body(j, carry):
        off = pl.multiple_of(j * k_blk, k_blk)
        _selection_bias(sc_scr[pl.ds(off, k_blk), :], thr, need, tie_flag, tie_scr, bias_scr, earlier_scr,
                        key_axis=0)
        bias = bias_scr[...]

        def qk(h):
            p = h // 2
            kp = k_scr[pl.ds(off, k_blk), p * LANES:(p + 1) * LANES]
            return jnp.dot(kp, qm_scr[h], preferred_element_type=F32)

        prs, alphas = [], []
        for h in range(DSA_HEADS):
            logit = qk(h) + bias
            m_old = m_scr[h]
            m_new = jnp.maximum(m_old, jnp.max(_reduce_rows(logit, jnp.max), axis=0, keepdims=True))
            alpha = jnp.exp(m_old - m_new)
            pe = jnp.exp(logit - m_new)
            l_scr[h] = alpha * l_scr[h] + jnp.sum(_reduce_rows(pe, jnp.sum), axis=0, keepdims=True)
            m_scr[h] = m_new
            prs.append(pe.astype(BF16))
            alphas.append(alpha)
        for h in range(DSA_HEADS):
            rows = slice(h * DSA_DH, (h + 1) * DSA_DH)
            vth = vt_scr[rows, pl.ds(off, k_blk)]
            acc_scr[rows, :] = alphas[h] * acc_scr[rows, :] + jnp.dot(vth, prs[h], preferred_element_type=F32)
        return carry

    lax.fori_loop(0, n_blocks, att_body, 0)

    for h in range(DSA_HEADS):
        rows = slice(h * DSA_DH, (h + 1) * DSA_DH)
        acc_scr[rows, :] = acc_scr[rows, :] / l_scr[h]
    o_ref[...] = acc_scr[...].T


def _dsa_prompt(qb, qi, wi, ki16, k16, vt16, batch, seq, q_blk, k_blk):
    n_sel = min(DSA_TOPK, seq // 4)
    nq = seq // q_blk
    qspec = lambda w: pl.BlockSpec((q_blk, w), lambda b, i: (b * nq + i, 0))
    return pl.pallas_call(
        functools.partial(_dsa_prompt_kernel, q_blk=q_blk, k_blk=k_blk, seq=seq, n_sel=n_sel),
        grid=(batch, nq),
        in_specs=[qspec(DSA_W), qspec(IDX_W), qspec(LANES),
                  pl.BlockSpec((seq, IDX_DIM), lambda b, i: (b, 0)),
                  pl.BlockSpec(memory_space=pl.ANY), pl.BlockSpec(memory_space=pl.ANY)],
        out_specs=qspec(DSA_W),
        out_shape=jax.ShapeDtypeStruct((batch * seq, DSA_W), F32),
        scratch_shapes=[pltpu.VMEM((seq, DSA_W), BF16), pltpu.VMEM((DSA_W, seq), BF16),
                        pltpu.VMEM((seq, q_blk), F32),
                        pltpu.VMEM((DSA_HEADS, LANES, q_blk), BF16),
                        pltpu.VMEM((DSA_HEADS, 1, q_blk), F32), pltpu.VMEM((DSA_HEADS, 1, q_blk), F32),
                        pltpu.VMEM((DSA_W, q_blk), F32), pltpu.VMEM((k_blk, q_blk), F32),
                        pltpu.VMEM((1, q_blk), F32), pltpu.VMEM((k_blk, k_blk), BF16),
                        pltpu.SemaphoreType.DMA((2,))],
        compiler_params=_params(("arbitrary", "arbitrary")),
        name="dsa_prompt",
    )(qb, qi, wi, ki16, k16, vt16)


PAGES_PER_STEP = 16
ATTN_PARTS = 2
SCORE_PAGES_PER_STEP = 32
_STEP_KEYS = PAGES_PER_STEP * PAGE_SIZE
_NT = (((1,), (1,)), ((), ()))


def _page_specs(page_shape, n_pages, per_step, clamp_last):
    specs = []
    zeros = (0,) * len(page_shape)
    for s in range(per_step):
        def imap(b, c, pt, s=s):
            page = jnp.minimum(c * per_step + s, n_pages - per_step + s) if clamp_last else c * per_step + s
            return (pt[b, page],) + zeros
        specs.append(pl.BlockSpec((1,) + page_shape, imap))
    return specs


def _stack_heads(qi):
    return jnp.concatenate([qi[:, h * IDX_DIM:(h + 1) * IDX_DIM] for h in range(IDX_HEADS)], axis=0)


def _idx_score(s4, wi, t):
    s = wi[:, _WI_LANE:_WI_LANE + 1] * jnp.maximum(s4[:t], 0.0)
    for h in range(1, IDX_HEADS):
        s = s + wi[:, _WI_LANE + h:_WI_LANE + h + 1] * jnp.maximum(s4[h * t:(h + 1) * t], 0.0)
    return s


def _dsa_sample_score_kernel(pt_ref, qi_ref, wi_ref, kin_ref, *rest, t, past):
    pages = rest[:SCORE_PAGES_PER_STEP]
    sc_ref, = rest[SCORE_PAGES_PER_STEP:]
    c = pl.program_id(1)
    nch = pl.num_programs(1)
    step_keys = SCORE_PAGES_PER_STEP * PAGE_SIZE
    qi4 = _stack_heads(qi_ref[...]).astype(BF16)
    wi = wi_ref[...]
    keys_t = jnp.concatenate([p[0] for p in pages], axis=1).astype(BF16)
    off = pl.multiple_of(c * step_keys, step_keys)
    sc_ref[0, :, pl.ds(off, step_keys)] = _idx_score(
        jnp.dot(qi4, keys_t, preferred_element_type=F32), wi, t)

    @pl.when(c == nch - 1)
    def _():
        knew = jnp.concatenate([kin_ref[...], jnp.zeros((LANES - t, IDX_DIM), F32)], axis=0).astype(BF16)
        s = _idx_score(lax.dot_general(qi4, knew, _NT, preferred_element_type=F32), wi, t)
        col = lax.broadcasted_iota(I32, (t, LANES), 1)
        row = lax.broadcasted_iota(I32, (t, LANES), 0)
        sc_ref[0, :, past:past + LANES] = jnp.where(col <= row, s, -jnp.inf)


THRESHOLD_GROUP = 8


def _dsa_sample_threshold_kernel(sc_ref, thr_ref, need_ref, flag_ref, *, t, past, n_sel):
    g = sc_ref.shape[0]
    rows = g * t

    def scores(_):
        return sc_ref[...].reshape(rows, sc_ref.shape[2])

    s = scores(0)
    s_max = jnp.max(s, axis=1, keepdims=True)
    s_min = jnp.min(jnp.where(s > -jnp.inf, s, jnp.inf), axis=1, keepdims=True)
    q_in_seq = lax.broadcasted_iota(I32, (g, t, 1), 1).reshape(rows, 1)
    active = (past + 1 + q_in_seq) > n_sel
    thr, need, ties = _kth_largest_threshold(scores, 1, rows, n_sel, active, bounds=(s_min, s_max))
    thr_ref[...] = thr.reshape(g, t, 1)
    need_ref[...] = need.reshape(g, t, 1)
    flag_ref[...] = ties.astype(F32).reshape(g, t, 1)


def _dsa_sample_attn_kernel(pt_ref, qb_ref, sc_ref, thr_ref, need_ref, flag_ref, kn_ref, vn_ref, *rest,
                            t, past):
    kpages = rest[:PAGES_PER_STEP]
    vpages = rest[PAGES_PER_STEP:2 * PAGES_PER_STEP]
    o_ref, m_scr, l_scr, acc_scr, bias_scr, biasn_scr, tie_scr, earlier_scr = rest[2 * PAGES_PER_STEP:]
    c = pl.program_id(1)
    nch = pl.num_programs(1) - 1
    flag = jnp.max(flag_ref[0]).astype(I32)

    @pl.when(jnp.logical_and(c == 0, flag != 0))
    def _():
        earlier_scr[...] = _earlier_matrix(_STEP_KEYS, 1)

    @pl.when(c == 0)
    def _():
        m_scr[...] = jnp.full(m_scr.shape, NEG_BIG, F32)
        l_scr[...] = jnp.zeros(l_scr.shape, F32)
        acc_scr[...] = jnp.zeros(acc_scr.shape, F32)
        tie_scr[...] = jnp.zeros(tie_scr.shape, F32)

    rows = DSA_HEADS * t
    blockmask = (lax.broadcasted_iota(I32, (rows, DSA_W), 0) // t
                 == lax.broadcasted_iota(I32, (rows, DSA_W), 1) // DSA_DH)
    qs = qb_ref[...] * (DSA_DH ** -0.5)
    qbd = jnp.where(blockmask, jnp.concatenate([qs] * DSA_HEADS, axis=0), 0.0).astype(BF16)
    thr = thr_ref[0]
    need = need_ref[0]

    def probs(part, bias, k_op, keys_on_lanes):
        if keys_on_lanes:
            logit = jnp.dot(qbd, k_op, preferred_element_type=F32)
        else:
            logit = lax.dot_general(qbd, k_op, _NT, preferred_element_type=F32)
        logit = logit + jnp.concatenate([bias] * DSA_HEADS, axis=0)
        m_old = m_scr[part]
        m_new = jnp.maximum(m_old, jnp.max(logit, axis=-1, keepdims=True))
        alpha = jnp.exp(m_old - m_new)
        pr = jnp.exp(logit - m_new).astype(BF16)
        l_scr[part] = alpha * l_scr[part] + jnp.sum(pr.astype(F32), axis=-1, keepdims=True)
        m_scr[part] = m_new
        return pr, alpha

    def accumulate(part, pr, alpha, v_op, keys_on_lanes):
        if keys_on_lanes:
            pv = lax.dot_general(pr, v_op, _NT, preferred_element_type=F32)
        else:
            pv = jnp.dot(pr, v_op, preferred_element_type=F32)
        acc_scr[part] = alpha * acc_scr[part] + pv

    @pl.when(c < nch)
    def _():
        off = pl.multiple_of(c * _STEP_KEYS, _STEP_KEYS)
        _selection_bias(sc_ref[0, :, pl.ds(off, _STEP_KEYS)], thr, need, flag, tie_scr, bias_scr, earlier_scr)
        bias = bias_scr[...]
        per = PAGES_PER_STEP // ATTN_PARTS
        stats = []
        for part in range(ATTN_PARTS):
            kc = jnp.concatenate([p[0].reshape(DSA_W, PAGE_SIZE)
                                  for p in kpages[part * per:(part + 1) * per]], axis=1).astype(BF16)
            stats.append(probs(part, bias[:, part * per * PAGE_SIZE:(part + 1) * per * PAGE_SIZE], kc, True))
        for part in range(ATTN_PARTS):
            vc = jnp.concatenate([p[0].reshape(DSA_W, PAGE_SIZE)
                                  for p in vpages[part * per:(part + 1) * per]], axis=1).astype(BF16)
            accumulate(part, stats[part][0], stats[part][1], vc, True)

    @pl.when(c == nch)
    def _():
        _selection_bias(sc_ref[0, :, past:past + LANES], thr, need, flag, tie_scr, biasn_scr, earlier_scr)
        zpad = jnp.zeros((LANES - t, DSA_W), F32)
        kc = jnp.concatenate([kn_ref[...], zpad], axis=0).astype(BF16)
        vc = jnp.concatenate([vn_ref[...], zpad], axis=0).astype(BF16)
        pr, alpha = probs(0, biasn_scr[...], kc, False)
        accumulate(0, pr, alpha, vc, False)
        m_all = m_scr[0]
        for part in range(1, ATTN_PARTS):
            m_all = jnp.maximum(m_all, m_scr[part])
        l_all = jnp.zeros_like(m_all)
        acc_all = jnp.zeros(acc_scr.shape[1:], F32)
        for part in range(ATTN_PARTS):
            w_part = jnp.exp(m_scr[part] - m_all)
            l_all = l_all + w_part * l_scr[part]
            acc_all = acc_all + w_part * acc_scr[part]
        outn = jnp.where(blockmask, acc_all / l_all, 0.0)
        out = outn[:t]
        for h in range(1, DSA_HEADS):
            out = out + outn[h * t:(h + 1) * t]
        o_ref[...] = out


def _dsa_sample(qb, qi, wi, ki_new, k_new, v_new, cache_k, cache_v, cache_idx_k, page_table, batch, t):
    n_pages = page_table.shape[1]
    past = n_pages * PAGE_SIZE
    n_sel = min(DSA_TOPK, (past + t) // 4)
    nch = n_pages // PAGES_PER_STEP
    nch_score = n_pages // SCORE_PAGES_PER_STEP
    lp = past + LANES
    kv_page = (DSA_HEADS, DSA_DH, PAGE_SIZE)
    ck = jnp.transpose(cache_k, (0, 2, 3, 1))
    cv = jnp.transpose(cache_v, (0, 2, 3, 1))
    cik = jnp.transpose(cache_idx_k, (0, 2, 1))
    rspec = lambda w: pl.BlockSpec((t, w), lambda b, c, pt: (b, 0))
    bspec = lambda w: pl.BlockSpec((1, t, w), lambda b, c, pt: (b, 0, 0))
    scores = pl.pallas_call(
        functools.partial(_dsa_sample_score_kernel, t=t, past=past),
        grid_spec=pltpu.PrefetchScalarGridSpec(
            num_scalar_prefetch=1,
            grid=(batch, nch_score),
            in_specs=[rspec(IDX_W), rspec(LANES), rspec(IDX_DIM)]
            + _page_specs((IDX_DIM, PAGE_SIZE), n_pages, SCORE_PAGES_PER_STEP, False),
            out_specs=bspec(lp),
        ),
        out_shape=jax.ShapeDtypeStruct((batch, t, lp), F32),
        compiler_params=_params(("arbitrary", "arbitrary")),
        name="dsa_sample_score",
    )(page_table, qi, wi, ki_new, *([cik] * SCORE_PAGES_PER_STEP))
    grp = THRESHOLD_GROUP if batch % THRESHOLD_GROUP == 0 else 1
    gspec = lambda w: pl.BlockSpec((grp, t, w), lambda i: (i, 0, 0))
    thr, need, flag = pl.pallas_call(
        functools.partial(_dsa_sample_threshold_kernel, t=t, past=past, n_sel=n_sel),
        grid=(batch // grp,),
        in_specs=[gspec(lp)],
        out_specs=[gspec(1)] * 3,
        out_shape=[jax.ShapeDtypeStruct((batch, t, 1), F32)] * 3,
        compiler_params=_params(("parallel",)),
        name="dsa_sample_threshold",
    )(scores)
    return pl.pallas_call(
        functools.partial(_dsa_sample_attn_kernel, t=t, past=past),
        grid_spec=pltpu.PrefetchScalarGridSpec(
            num_scalar_prefetch=1,
            grid=(batch, nch + 1),
            in_specs=[rspec(DSA_W), bspec(lp), bspec(1), bspec(1), bspec(1), rspec(DSA_W), rspec(DSA_W)]
            + _page_specs(kv_page, n_pages, PAGES_PER_STEP, True)
            + _page_specs(kv_page, n_pages, PAGES_PER_STEP, True),
            out_specs=rspec(DSA_W),
            scratch_shapes=[pltpu.VMEM((ATTN_PARTS, DSA_HEADS * t, 1), F32),
                            pltpu.VMEM((ATTN_PARTS, DSA_HEADS * t, 1), F32),
                            pltpu.VMEM((ATTN_PARTS, DSA_HEADS * t, DSA_W), F32), pltpu.VMEM((t, _STEP_KEYS), F32),
                            pltpu.VMEM((t, LANES), F32), pltpu.VMEM((t, 1), F32),
                            pltpu.VMEM((_STEP_KEYS, _STEP_KEYS), BF16)],
        ),
        out_shape=jax.ShapeDtypeStruct((batch * t, DSA_W), F32),
        compiler_params=_params(("arbitrary", "arbitrary")),
        name="dsa_sample_attn",
    )(page_table, qb, scores, thr, need, flag, k_new, v_new,
      *([ck] * PAGES_PER_STEP), *([cv] * PAGES_PER_STEP))


MOE_ROWS = 256
ROUTE_TILE = 512
TOKEN_TILE = 256


def _split_bf16(a):
    hi = a.astype(BF16)
    lo = (a - hi.astype(F32)).astype(BF16)
    return hi, lo


def _router_kernel(x_ref, whi_ref, wlo_ref, b_ref, idx_ref, gate_ref, rank_ref, cnt_ref, carry_scr):
    i = pl.program_id(0)
    tm = x_ref.shape[0]

    @pl.when(i == 0)
    def _():
        carry_scr[...] = jnp.zeros(carry_scr.shape, F32)

    xhi, xlo = _split_bf16(x_ref[...])
    whi = whi_ref[...]
    logits = (jnp.dot(xhi, whi, preferred_element_type=F32)
              + jnp.dot(xlo, whi, preferred_element_type=F32)
              + jnp.dot(xhi, wlo_ref[...], preferred_element_type=F32)) + b_ref[...]
    lane = lax.broadcasted_iota(I32, (tm, N_EXPERTS), 1)
    slot = lax.broadcasted_iota(I32, (tm, TOP_K), 1)
    vals, idxs = [], []
    cur = logits
    for _ in range(TOP_K):
        m = jnp.max(cur, axis=-1, keepdims=True)
        ix = jnp.min(jnp.where(cur == m, lane, N_EXPERTS), axis=-1, keepdims=True)
        vals.append(m)
        idxs.append(ix)
        cur = jnp.where(lane == ix, -jnp.inf, cur)
    es = [jnp.exp(v - vals[0]) for v in vals]
    denom = es[0] + es[1] + es[2] + es[3]
    onehot = jnp.zeros((tm, N_EXPERTS), F32)
    for ix in idxs:
        onehot = onehot + jnp.where(lane == ix, 1.0, 0.0)
    earlier = (lax.broadcasted_iota(I32, (tm, tm), 1) < lax.broadcasted_iota(I32, (tm, tm), 0))
    excl = jnp.dot(jnp.where(earlier, 1.0, 0.0).astype(BF16), onehot.astype(BF16),
                   preferred_element_type=F32) + carry_scr[...]
    idx_out = jnp.zeros((tm, TOP_K), I32)
    gate_out = jnp.zeros((tm, TOP_K), F32)
    rank_out = jnp.zeros((tm, TOP_K), F32)
    for k in range(TOP_K):
        rk = jnp.sum(jnp.where(lane == idxs[k], excl, 0.0), axis=-1, keepdims=True)
        idx_out = jnp.where(slot == k, idxs[k], idx_out)
        gate_out = jnp.where(slot == k, es[k] / denom, gate_out)
        rank_out = jnp.where(slot == k, rk, rank_out)
    idx_ref[...] = idx_out
    gate_ref[...] = gate_out
    rank_ref[...] = rank_out.astype(I32)
    total = carry_scr[...] + jnp.sum(onehot, axis=0, keepdims=True)
    carry_scr[...] = total
    cnt_ref[...] = total


def _router(x, w_router, b_router):
    n = x.shape[0]
    tm = ROUTE_TILE if n % ROUTE_TILE == 0 else TOKEN_TILE
    whi, wlo = _split_bf16(w_router)
    b2 = b_router.reshape(1, N_EXPERTS)
    kspec = pl.BlockSpec((tm, TOP_K), lambda i: (i, 0))
    return pl.pallas_call(
        _router_kernel,
        grid=(n // tm,),
        in_specs=[_row_spec(tm, D_MODEL), _full_spec(whi), _full_spec(wlo), _full_spec(b2)],
        out_specs=[kspec, kspec, kspec, pl.BlockSpec((1, N_EXPERTS), lambda i: (0, 0))],
        out_shape=[jax.ShapeDtypeStruct((n, TOP_K), I32), jax.ShapeDtypeStruct((n, TOP_K), F32),
                   jax.ShapeDtypeStruct((n, TOP_K), I32), jax.ShapeDtypeStruct((1, N_EXPERTS), F32)],
        scratch_shapes=[pltpu.VMEM((1, N_EXPERTS), F32)],
        compiler_params=_params(("arbitrary",)),
        name="router",
    )(x, whi, wlo, b2)


def _row_copy(src, src_row, dst, dst_row, sem):
    return pltpu.make_async_copy(src.at[pl.ds(src_row, 1)], dst.at[pl.ds(dst_row, 1)], sem)


def _dispatch_kernel(dest_ref, x_ref, xs_in, xs_out, sem):
    del xs_in
    tm = x_ref.shape[0]

    def issue(r, c):
        for k in range(TOP_K):
            _row_copy(x_ref, r, xs_out, dest_ref[r * TOP_K + k], sem).start(priority=k % 2)
        return c

    lax.fori_loop(0, tm, issue, 0)
    for _ in range(TOP_K):
        pltpu.make_async_copy(x_ref, x_ref, sem).wait()


def _dispatch(x, dest_flat, n_rows):
    n = x.shape[0]
    tm = TOKEN_TILE
    return pl.pallas_call(
        _dispatch_kernel,
        grid=(n // tm,),
        in_specs=[pl.BlockSpec((tm * TOP_K,), lambda i: (i,), memory_space=pltpu.SMEM),
                  _row_spec(tm, D_MODEL), pl.BlockSpec(memory_space=pl.ANY)],
        out_specs=pl.BlockSpec(memory_space=pl.ANY),
        out_shape=jax.ShapeDtypeStruct((n_rows, D_MODEL), F32),
        scratch_shapes=[pltpu.SemaphoreType.DMA(())],
        input_output_aliases={2: 0},
        compiler_params=_params(("arbitrary",)),
        name="moe_dispatch",
    )(dest_flat, x, jnp.zeros((n_rows, D_MODEL), F32))


def _expert_kernel(be_ref, na_ref, x_ref, w1_ref, b1_ref, w2_ref, b2_ref, o_ref, w1b_scr, w2b_scr):
    i = pl.program_id(0)
    active = i < na_ref[0]
    changed = jnp.logical_or(i == 0, be_ref[i] != be_ref[jnp.maximum(i - 1, 0)])

    @pl.when(jnp.logical_and(active, changed))
    def _():
        w1b_scr[...] = w1_ref[0].astype(BF16)
        w2b_scr[...] = w2_ref[0].astype(BF16)

    @pl.when(active)
    def _():
        h = jnp.dot(x_ref[...].astype(BF16), w1b_scr[...], preferred_element_type=F32) + b1_ref[0]
        gt = jnp.minimum(h[:, :D_FF], SWIGLU_LIMIT)
        up = jnp.clip(h[:, D_FF:], -SWIGLU_LIMIT, SWIGLU_LIMIT)
        act = (up + 1.0) * gt * jax.nn.sigmoid(SWIGLU_ALPHA * gt)
        o_ref[...] = jnp.dot(act.astype(BF16), w2b_scr[...], preferred_element_type=F32) + b2_ref[0]

    @pl.when(jnp.logical_not(active))
    def _():
        o_ref[...] = jnp.zeros(o_ref.shape, F32)


def _experts(xs, block_e, n_active, w1, b1, w2, b2):
    n_rows = xs.shape[0]
    nblk = n_rows // MOE_ROWS
    w1 = w1.reshape(-1, D_MODEL, 2 * D_FF)
    w2 = w2.reshape(-1, D_FF, D_MODEL)
    b1r = b1.reshape(-1, 1, 2 * D_FF)
    b2r = b2.reshape(-1, 1, D_MODEL)
    last = lambda i, na: jnp.minimum(i, na[0] - 1)
    return pl.pallas_call(
        _expert_kernel,
        grid_spec=pltpu.PrefetchScalarGridSpec(
            num_scalar_prefetch=2,
            grid=(nblk,),
            in_specs=[pl.BlockSpec((MOE_ROWS, D_MODEL), lambda i, be, na: (last(i, na), 0)),
                      pl.BlockSpec((1, D_MODEL, 2 * D_FF), lambda i, be, na: (be[i], 0, 0)),
                      pl.BlockSpec((1, 1, 2 * D_FF), lambda i, be, na: (be[i], 0, 0)),
                      pl.BlockSpec((1, D_FF, D_MODEL), lambda i, be, na: (be[i], 0, 0)),
                      pl.BlockSpec((1, 1, D_MODEL), lambda i, be, na: (be[i], 0, 0))],
            out_specs=pl.BlockSpec((MOE_ROWS, D_MODEL), lambda i, be, na: (i, 0)),
            scratch_shapes=[pltpu.VMEM((D_MODEL, 2 * D_FF), BF16), pltpu.VMEM((D_FF, D_MODEL), BF16)],
        ),
        out_shape=jax.ShapeDtypeStruct((n_rows, D_MODEL), F32),
        compiler_params=_params(("arbitrary",)),
        name="moe_experts",
    )(block_e, n_active, xs, w1, b1r, w2, b2r)


def _combine_kernel(dest_ref, x_ref, gate_ref, g_ref, b_ref, yb_hbm, *rest, first_tiles):
    outs, (buf, sem) = rest[:-2], rest[-2:]
    tm = x_ref.shape[0]

    def issue(r, c):
        for k in range(TOP_K):
            _row_copy(yb_hbm, dest_ref[r * TOP_K + k], buf.at[k], r, sem).start(priority=k % 2)
        return c

    lax.fori_loop(0, tm, issue, 0)
    for k in range(TOP_K):
        pltpu.make_async_copy(buf.at[k], buf.at[k], sem).wait()
    gate = gate_ref[...]
    y = gate[:, 0:1] * buf[0]
    for k in range(1, TOP_K):
        y = y + gate[:, k:k + 1] * buf[k]
    res = _layer_norm(DN_ALPHA * x_ref[...] + y, g_ref[...], b_ref[...])
    if first_tiles is None:
        outs[0][...] = res
    else:
        i = pl.program_id(0)

        @pl.when(i < first_tiles)
        def _():
            outs[0][...] = res

        @pl.when(i >= first_tiles)
        def _():
            outs[1][...] = res


def _combine(x, yb, dest_flat, gate, g, b, split=None):
    n = x.shape[0]
    tm = TOKEN_TILE
    g2, b2 = g.reshape(1, D_MODEL), b.reshape(1, D_MODEL)
    if split is None:
        first_tiles = None
        out_specs = _row_spec(tm, D_MODEL)
        out_shape = jax.ShapeDtypeStruct((n, D_MODEL), F32)
    else:
        first_tiles = split // tm
        out_specs = [pl.BlockSpec((tm, D_MODEL), lambda i: (jnp.minimum(i, first_tiles - 1), 0)),
                     pl.BlockSpec((tm, D_MODEL), lambda i: (jnp.maximum(i - first_tiles, 0), 0))]
        out_shape = [jax.ShapeDtypeStruct((split, D_MODEL), F32), jax.ShapeDtypeStruct((n - split, D_MODEL), F32)]
    return pl.pallas_call(
        functools.partial(_combine_kernel, first_tiles=first_tiles),
        grid=(n // tm,),
        in_specs=[pl.BlockSpec((tm * TOP_K,), lambda i: (i,), memory_space=pltpu.SMEM),
                  _row_spec(tm, D_MODEL), pl.BlockSpec((tm, TOP_K), lambda i: (i, 0)),
                  _full_spec(g2), _full_spec(b2), pl.BlockSpec(memory_space=pl.ANY)],
        out_specs=out_specs,
        out_shape=out_shape,
        scratch_shapes=[pltpu.VMEM((TOP_K, tm, D_MODEL), F32), pltpu.SemaphoreType.DMA(())],
        compiler_params=_params(("arbitrary",)),
        name="moe_combine",
    )(dest_flat, x, gate, g2, b2, yb)


def _moe_layer(x, layer, w_router, b_router, w1, b1, w2, b2, g, b, split=None):
    n = x.shape[0]
    idx, gate, rank, counts = _router(x, w_router, b_router)
    nblk = (n * TOP_K) // MOE_ROWS + N_EXPERTS
    cnt = counts[0].astype(I32)
    padded = (cnt + MOE_ROWS - 1) // MOE_ROWS * MOE_ROWS
    pad_end = jnp.cumsum(padded)
    pad_start = pad_end - padded
    dest = (pad_start[idx] + rank).reshape(-1)
    n_active = (pad_end[-1] // MOE_ROWS).astype(I32)
    blk = jnp.arange(nblk, dtype=I32)
    blk = jnp.minimum(blk, n_active - 1)
    block_e = jnp.sum((pad_end[None, :] <= (blk * MOE_ROWS)[:, None]).astype(I32), axis=1)
    block_e = jnp.minimum(block_e, N_EXPERTS - 1) + layer * N_EXPERTS
    xs = _dispatch(x, dest, nblk * MOE_ROWS)
    yb = _experts(xs, block_e, n_active.reshape(1), w1, b1, w2, b2)
    return _combine(x, yb, dest, gate, g, b, split)


POOL_HALO = 16


def _pool_kernel(x_ref, halo_ref, w_ref, sc_ref, g_ref, b_ref, o_ref, ext_scr, *, tiles_per_seq, n_prev):
    tm = x_ref.shape[0]
    tile = pl.program_id(0) % tiles_per_seq
    x = x_ref[...]
    halo = halo_ref[...]
    if n_prev == 0:
        halo = jnp.where(tile == 0, 0.0, halo)
    ext_scr[0:POOL_HALO] = halo
    ext_scr[POOL_HALO:POOL_HALO + tm] = x
    pos = n_prev + tile * tm + lax.broadcasted_iota(I32, (tm, 1), 0)
    parts = []
    for g, w in enumerate(POOL_WINDOWS):
        cols = slice(g * POOL_GROUP, (g + 1) * POOL_GROUP)
        xg = x[:, cols]
        s = xg
        for d in range(1, w):
            s = s + ext_scr[POOL_HALO - d:POOL_HALO - d + tm, cols]
        cnt = jnp.minimum(pos + 1, w).astype(F32)
        dg = s / cnt - xg
        parts.append(jnp.dot(dg.astype(BF16), w_ref[g], preferred_element_type=F32))
    mix = jnp.concatenate(parts, axis=1) * sc_ref[...]
    o_ref[...] = _layer_norm(DN_ALPHA * x + mix, g_ref[...], b_ref[...])


def _pool(x, halo_src, pool_w, pool_scale, g, b, tm, tiles_per_seq, n_prev, halo_map, n, row_off=0):
    w16 = pool_w.astype(BF16)
    sc = pool_scale.reshape(1, D_MODEL)
    g2, b2 = g.reshape(1, D_MODEL), b.reshape(1, D_MODEL)
    blk_off = row_off // tm
    return pl.pallas_call(
        functools.partial(_pool_kernel, tiles_per_seq=tiles_per_seq, n_prev=n_prev),
        grid=(n // tm,),
        in_specs=[pl.BlockSpec((tm, D_MODEL), lambda i: (i + blk_off, 0)),
                  pl.BlockSpec((POOL_HALO, D_MODEL), halo_map),
                  _full_spec(w16), _full_spec(sc), _full_spec(g2), _full_spec(b2)],
        out_specs=_row_spec(tm, D_MODEL),
        out_shape=jax.ShapeDtypeStruct((n, D_MODEL), F32),
        scratch_shapes=[pltpu.VMEM((POOL_HALO + tm, D_MODEL), F32)],
        compiler_params=_params(("parallel",)),
        name="pool",
    )(x, halo_src, w16, sc, g2, b2)


PROJ_TILE = 512
DSA_Q_BLOCK = 512
DSA_K_BLOCK = 512
POOL_TILE = 512


def kernel(x_prompt, x_sample, cache_k, cache_v, cache_idx_k, page_table, state_gla, state_pool,
           w_in, gla_fg_w2, gla_fg_b, gla_norm_g, idx_kn_g, idx_kn_b, w_out,
           pool_w, pool_scale, ln_mix_g, ln_mix_b, ln_ffn_g, ln_ffn_b,
           moe_router_w, moe_router_b, moe_w1, moe_b1, moe_w2, moe_b2):
    bp, sp, _ = x_prompt.shape
    bs, ts, _ = x_sample.shape
    n_p, n_s = bp * sp, bs * ts
    xp = x_prompt.reshape(n_p, D_MODEL)
    xs = x_sample.reshape(n_s, D_MODEL)
    proj_w = (w_in, gla_fg_w2, gla_fg_b, idx_kn_g, idx_kn_b)

    (qa, ka, va, ga, lf, qb, _, _, qi, _, wi, kb16, vt16, ki16,
     kt, vt, kit) = _project(xp, *proj_w, tm=PROJ_TILE, seq=sp)
    oa_p, gla_p = _gla(qa, ka, va, lf, ga, jnp.zeros((bp, GLA_HEADS, GLA_DK, GLA_DV), F32), gla_norm_g, bp, sp)
    ob_p = _dsa_prompt(qb, qi, wi, ki16, kb16, vt16, bp, sp, DSA_Q_BLOCK, DSA_K_BLOCK)
    x1p = _merge(xp, oa_p, ob_p, w_out, ln_mix_g[0], ln_mix_b[0], PROJ_TILE)
    k_p = jnp.transpose(kt.reshape(bp, DSA_HEADS, DSA_DH, sp), (0, 3, 1, 2))
    v_p = jnp.transpose(vt.reshape(bp, DSA_HEADS, DSA_DH, sp), (0, 3, 1, 2))
    kidx_p = jnp.transpose(kit, (0, 2, 1))

    qa, ka, va, ga, lf, qb, kb, vb, qi, ki, wi, _, _, _ = _project(xs, *proj_w, tm=n_s)
    oa_s, gla_s = _gla(qa, ka, va, lf, ga, state_gla, gla_norm_g, bs, ts)
    ob_s = _dsa_sample(qb, qi, wi, ki, kb, vb, cache_k, cache_v, cache_idx_k, page_table, bs, ts)
    x1s = _merge(xs, oa_s, ob_s, w_out, ln_mix_g[0], ln_mix_b[0], n_s)
    k_s = kb.reshape(bs, ts, DSA_HEADS, DSA_DH)
    v_s = vb.reshape(bs, ts, DSA_HEADS, DSA_DH)
    kidx_s = ki.reshape(bs, ts, IDX_DIM)

    x_all = _moe_layer(jnp.concatenate([x1p, x1s], axis=0), 0, moe_router_w[0], moe_router_b[0],
                       moe_w1, moe_b1, moe_w2, moe_b2, ln_ffn_g[0], ln_ffn_b[0])

    pool_p = jnp.stack([x_all[(s + 1) * sp - POOL_STATE:(s + 1) * sp] for s in range(bp)])
    xs3 = x_all[n_p:].reshape(bs, ts, D_MODEL)
    pool_s = jnp.concatenate([state_pool, xs3], axis=1)[:, -POOL_STATE:]
    per_seq = sp // POOL_TILE
    halo_step = POOL_TILE // POOL_HALO
    x2p = _pool(x_all, x_all, pool_w, pool_scale, ln_mix_g[1], ln_mix_b[1], POOL_TILE, per_seq, 0,
                lambda i: (jnp.maximum(i * halo_step - 1, 0), 0), n_p)
    halo_s = jnp.concatenate([jnp.zeros((bs, POOL_HALO - POOL_STATE, D_MODEL), F32), state_pool], axis=1)
    x2s = _pool(x_all, halo_s.reshape(bs * POOL_HALO, D_MODEL), pool_w, pool_scale, ln_mix_g[1], ln_mix_b[1],
                ts, 1, POOL_STATE, lambda i: (i, 0), n_s, row_off=n_p)
    y_p, y_s = _moe_layer(jnp.concatenate([x2p, x2s], axis=0), 1, moe_router_w[1], moe_router_b[1],
                          moe_w1, moe_b1, moe_w2, moe_b2, ln_ffn_g[1], ln_ffn_b[1], split=n_p)
    y_p = y_p.reshape(bp, sp, D_MODEL)
    y_s = y_s.reshape(bs, ts, D_MODEL)
    return (y_p, y_s, k_p, v_p, kidx_p, gla_p, pool_p, k_s, v_s, kidx_s, gla_s, pool_s)
```

```python
import functools
import math

import jax
import jax.numpy as jnp
from jax import lax
from jax.experimental import pallas as pl
from jax.experimental.pallas import tpu as pltpu

F32 = jnp.float32
BF16 = jnp.bfloat16
I32 = jnp.int32

D_MODEL = 1024
DEPTH = 2
PAGE_SIZE = 128
GLA_HEADS = 4
GLA_DK = 64
GLA_DV = 128
GLA_GATE_RANK = 16
GLA_TAU = 16.0
GLA_CHUNK = 64
GLA_TILE = 2 * GLA_CHUNK
DSA_HEADS = 8
DSA_DH = 64
IDX_HEADS = 4
IDX_DIM = 64
DSA_TOPK = 256
IDX_W_SCALE = (IDX_HEADS ** -0.5) * (IDX_DIM ** -0.5)
POOL_WINDOWS = (2, 4, 8, 16)
POOL_GROUP = D_MODEL // 4
POOL_STATE = 16 - 1
N_EXPERTS = 32
TOP_K = 4
D_FF = D_MODEL
SWIGLU_ALPHA = 1.702
SWIGLU_LIMIT = 7.0
DN_ALPHA = (2 * DEPTH) ** 0.25
LN_EPS = 1e-5
GLA_QK = GLA_HEADS * GLA_DK
GLA_V = GLA_HEADS * GLA_DV
DSA_W = DSA_HEADS * DSA_DH
IDX_W = IDX_HEADS * IDX_DIM

LANES = 128
SUBLANES = 8
VMEM_LIMIT_BYTES = 56 * 1024 * 1024

NEG_BIG = -1e30


def _params(sem, vmem=VMEM_LIMIT_BYTES):
    return pltpu.CompilerParams(dimension_semantics=sem, vmem_limit_bytes=vmem)


def _layer_norm(x, g, b):
    mu = jnp.mean(x, axis=-1, keepdims=True)
    xc = x - mu
    var = jnp.mean(xc * xc, axis=-1, keepdims=True)
    return xc * lax.rsqrt(var + LN_EPS) * g + b


def _row_spec(tm, w):
    return pl.BlockSpec((tm, w), lambda i: (i, 0))


def _full_spec(a):
    nd = a.ndim
    return pl.BlockSpec(a.shape, lambda i: (0,) * nd)


_MAIN_W = (GLA_QK, GLA_QK, GLA_V, GLA_V, DSA_W, DSA_W, DSA_W, IDX_W)
_MAIN_OFF = tuple(sum(_MAIN_W[:i]) for i in range(len(_MAIN_W) + 1))


def _proj_kernel(x_ref, wm_ref, ws_ref, fgw_ref, fgb_ref, kng_ref, knb_ref,
                 qa_ref, ka_ref, va_ref, ga_ref, lf_ref, qb_ref, kb_ref, vb_ref,
                 qi_ref, ki_ref, wi_ref, kb16_ref, vt16_ref, ki16_ref, *channel_major):
    xb = x_ref[...].astype(BF16)

    def mm(n):
        return jnp.dot(xb, wm_ref[:, _MAIN_OFF[n]:_MAIN_OFF[n + 1]], preferred_element_type=F32)

    qa_ref[...] = mm(0) * (GLA_DK ** -0.5)
    ka_ref[...] = mm(1)
    va_ref[...] = mm(2)
    ga_ref[...] = mm(3)
    qb_ref[...] = mm(4)
    kb = mm(5)
    kb_ref[...] = kb
    kb16_ref[...] = kb.astype(BF16)
    vb = mm(6)
    vb_ref[...] = vb
    vbt = vb.T
    vt16_ref[...] = vbt.astype(BF16)
    qi_ref[...] = mm(7)

    small = jnp.dot(xb, ws_ref[...], preferred_element_type=F32)
    ki = _layer_norm(small[:, :IDX_DIM], kng_ref[...], knb_ref[...])
    ki_ref[...] = ki
    ki16_ref[...] = ki.astype(BF16)
    if channel_major:
        kt_ref, vt_ref, kit_ref = channel_major
        kt_ref[0] = kb.T
        vt_ref[0] = vbt
        kit_ref[0] = jnp.concatenate([ki, jnp.zeros_like(ki)], axis=1).T[:IDX_DIM]
    fa = small[:, IDX_DIM:IDX_DIM + GLA_GATE_RANK]
    z = jnp.dot(fa, fgw_ref[...], preferred_element_type=F32,
                precision=lax.Precision.HIGHEST) + fgb_ref[...]
    lf_ref[...] = (jnp.minimum(z, 0.0) - jnp.log(1.0 + jnp.exp(-jnp.abs(z)))) * (1.0 / GLA_TAU)
    wi_ref[...] = small * IDX_W_SCALE


def _project(x2d, w_in, gla_fg_w2, gla_fg_b, idx_kn_g, idx_kn_b, tm, seq=None):
    n = x2d.shape[0]
    pts = [0]
    for s in (GLA_QK, GLA_QK, GLA_V, GLA_V, GLA_GATE_RANK, DSA_W, DSA_W, DSA_W, IDX_W, IDX_DIM, IDX_HEADS):
        pts.append(pts[-1] + s)
    seg = lambda i: w_in[:, pts[i]:pts[i + 1]]
    wm = jnp.concatenate([seg(0), seg(1), seg(2), seg(3), seg(5), seg(6), seg(7), seg(8)], axis=1).astype(BF16)
    pad = LANES - IDX_DIM - GLA_GATE_RANK - IDX_HEADS
    ws = jnp.concatenate([seg(9), seg(4), seg(10), jnp.zeros((D_MODEL, pad), w_in.dtype)], axis=1).astype(BF16)
    fgb = gla_fg_b.reshape(1, GLA_QK)
    kng = idx_kn_g.reshape(1, IDX_DIM)
    knb = idx_kn_b.reshape(1, IDX_DIM)
    widths = (GLA_QK, GLA_QK, GLA_V, GLA_V, GLA_QK, DSA_W, DSA_W, DSA_W, IDX_W, IDX_DIM, LANES)
    out_shape = [jax.ShapeDtypeStruct((n, w), F32) for w in widths]
    out_shape += [jax.ShapeDtypeStruct((n, DSA_W), BF16), jax.ShapeDtypeStruct((DSA_W, n), BF16),
                  jax.ShapeDtypeStruct((n, IDX_DIM), BF16)]
    out_specs = [_row_spec(tm, w) for w in widths]
    out_specs += [_row_spec(tm, DSA_W), pl.BlockSpec((DSA_W, tm), lambda i: (0, i)), _row_spec(tm, IDX_DIM)]
    if seq is not None:
        tps = seq // tm
        for w in (DSA_W, DSA_W, IDX_DIM):
            out_shape.append(jax.ShapeDtypeStruct((n // seq, w, seq), F32))
            out_specs.append(pl.BlockSpec((1, w, tm), lambda i: (i // tps, 0, i % tps)))
    return pl.pallas_call(
        _proj_kernel,
        grid=(n // tm,),
        in_specs=[_row_spec(tm, D_MODEL), _full_spec(wm), _full_spec(ws), _full_spec(gla_fg_w2),
                  _full_spec(fgb), _full_spec(kng), _full_spec(knb)],
        out_specs=out_specs,
        out_shape=out_shape,
        compiler_params=_params(("parallel",)),
        name="proj",
    )(x2d, wm, ws, gla_fg_w2, fgb, kng, knb)


def _gla_kernel(q_ref, k_ref, v_ref, lf_ref, ga_ref, s0_ref, ng_ref, o_ref, sout_ref, s_scr, *, chunk):
    c = pl.program_id(1)
    nc = pl.num_programs(1)

    @pl.when(c == 0)
    def _():
        s_scr[...] = s0_ref[0]

    g = lf_ref[...]
    ri = lax.broadcasted_iota(I32, (chunk, chunk), 0)
    ci = lax.broadcasted_iota(I32, (chunk, chunk), 1)
    causal = ci <= ri
    tri = jnp.where(causal, 1.0, 0.0).astype(F32)
    b = jnp.dot(tri, g, preferred_element_type=F32, precision=lax.Precision.HIGHEST)
    mid = chunk // 2
    b_mid = b[mid:mid + 1, :]
    b_last = b[chunk - 1:chunk, :]
    q = q_ref[...]
    k = k_ref[...]
    q_in = q * jnp.exp(b)
    q_rel = q * jnp.exp(b - b_mid)
    k_rel = k * jnp.exp(b_mid - b)
    k_out = k * jnp.exp(b_last - b)
    dec_last = jnp.exp(b_last)
    v = v_ref[...]
    ga = ga_ref[...]
    ng = ng_ref[...]
    eye = jnp.where(lax.broadcasted_iota(I32, (GLA_DK, GLA_DK), 0)
                    == lax.broadcasted_iota(I32, (GLA_DK, GLA_DK), 1), 1.0, 0.0).astype(F32)
    for h in range(GLA_HEADS):
        ks = slice(h * GLA_DK, (h + 1) * GLA_DK)
        vs = slice(h * GLA_DV, (h + 1) * GLA_DV)
        s_h = s_scr[h]
        vh = v[:, vs]
        inter = jnp.dot(q_in[:, ks], s_h, preferred_element_type=F32)
        att = lax.dot_general(q_rel[:, ks], k_rel[:, ks], (((1,), (1,)), ((), ())),
                              preferred_element_type=F32)
        att = jnp.where(causal, att, 0.0)
        o = inter + jnp.dot(att, vh, preferred_element_type=F32)
        kv = lax.dot_general(k_out[:, ks], vh, (((0,), (0,)), ((), ())), preferred_element_type=F32)
        s_scr[h] = jnp.dot(eye * dec_last[:, ks], s_h, preferred_element_type=F32,
                           precision=lax.Precision.HIGHEST) + kv
        ms = jnp.mean(o * o, axis=-1, keepdims=True)
        gh = ga[:, vs]
        o_ref[:, vs] = o * lax.rsqrt(ms + LN_EPS) * ng * (gh * jax.nn.sigmoid(gh))

    @pl.when(c == nc - 1)
    def _():
        sout_ref[0] = s_scr[...]


def _gla(qa, ka, va, lf, ga, s0, gla_norm_g, batch, seq):
    chunk = math.gcd(seq, GLA_TILE)
    nc = seq // chunk
    ng = gla_norm_g.reshape(1, GLA_DV)
    spec = lambda w: pl.BlockSpec((chunk, w), lambda b, c: (b * nc + c, 0))
    sspec = pl.BlockSpec((1, GLA_HEADS, GLA_DK, GLA_DV), lambda b, c: (b, 0, 0, 0))
    return pl.pallas_call(
        functools.partial(_gla_kernel, chunk=chunk),
        grid=(batch, nc),
        in_specs=[spec(GLA_QK), spec(GLA_QK), spec(GLA_V), spec(GLA_QK), spec(GLA_V), sspec,
                  pl.BlockSpec((1, GLA_DV), lambda b, c: (0, 0))],
        out_specs=[spec(GLA_V), sspec],
        out_shape=[jax.ShapeDtypeStruct((batch * seq, GLA_V), F32),
                   jax.ShapeDtypeStruct((batch, GLA_HEADS, GLA_DK, GLA_DV), F32)],
        scratch_shapes=[pltpu.VMEM((GLA_HEADS, GLA_DK, GLA_DV), F32)],
        compiler_params=_params(("parallel", "arbitrary")),
        name="gla",
    )(qa, ka, va, lf, ga, s0, ng)


def _merge_kernel(x_ref, oa_ref, ob_ref, w_ref, g_ref, b_ref, o_ref):
    mix = jnp.dot(oa_ref[...].astype(BF16), w_ref[:GLA_V, :], preferred_element_type=F32)
    mix = mix + jnp.dot(ob_ref[...].astype(BF16), w_ref[GLA_V:, :], preferred_element_type=F32)
    o_ref[...] = _layer_norm(DN_ALPHA * x_ref[...] + mix, g_ref[...], b_ref[...])


def _merge(x2d, oa, ob, w_out, g, b, tm):
    n = x2d.shape[0]
    w16 = w_out.astype(BF16)
    g2, b2 = g.reshape(1, D_MODEL), b.reshape(1, D_MODEL)
    return pl.pallas_call(
        _merge_kernel,
        grid=(n // tm,),
        in_specs=[_row_spec(tm, D_MODEL), _row_spec(tm, GLA_V), _row_spec(tm, DSA_W),
                  _full_spec(w16), _full_spec(g2), _full_spec(b2)],
        out_specs=_row_spec(tm, D_MODEL),
        out_shape=jax.ShapeDtypeStruct((n, D_MODEL), F32),
        compiler_params=_params(("parallel",)),
        name="merge",
    )(x2d, oa, ob, w16, g2, b2)


_KEY_NEG_INF = -2139095041
_KEY_POS_INF = 2139095040
_WI_LANE = IDX_DIM + GLA_GATE_RANK


def _unkey(kk):
    return lax.bitcast_convert_type(jnp.where(kk < 0, kk ^ 0x7FFFFFFF, kk), F32)


def _key(v):
    i = lax.bitcast_convert_type(v, I32)
    return jnp.where(i < 0, i ^ 0x7FFFFFFF, i)


_REDUCE_CHAINS = 8


def _reduce_rows(x, reduce_fn):
    r, q = x.shape
    groups = r // SUBLANES
    chains = _REDUCE_CHAINS if groups % _REDUCE_CHAINS == 0 else 1
    y = x.reshape(chains, groups // chains, SUBLANES, q)
    return reduce_fn(reduce_fn(y, axis=1), axis=0)


def _fold_keys(m, key_axis):
    if key_axis == 1:
        out = m[:, :LANES]
        for t in range(1, m.shape[1] // LANES):
            out = out + m[:, t * LANES:(t + 1) * LANES]
        return out
    return _reduce_rows(m, jnp.sum)


_VALUE_BISECTIONS = 12
_KEY_BITS = 32


def _kth_largest_threshold(read_block, n_blocks, rows, k, active, key_axis=1, bounds=None):
    kf = float(k)
    qshape = (rows, 1) if key_axis == 1 else (1, rows)
    part = (rows, LANES) if key_axis == 1 else (SUBLANES, rows)
    value_steps = _VALUE_BISECTIONS if bounds is not None else 0
    max_steps = value_steps + _KEY_BITS + 2

    def count(preds):
        def body(c, accs):
            s = read_block(c)
            return tuple(a + _fold_keys(jnp.where(p(s), 1.0, 0.0), key_axis) for a, p in zip(accs, preds))
        accs = lax.fori_loop(0, n_blocks, body, tuple(jnp.zeros(part, F32) for _ in preds))
        return [jnp.sum(a, axis=key_axis, keepdims=True) for a in accs]

    def cond(st):
        return st[5] > 0

    def body(st):
        it, lo, hi, cnt_lo, open_, _ = st
        mid = (lo >> 1) + (hi >> 1) + (lo & hi & 1)
        if value_steps:
            vmid = _key(0.5 * _unkey(lo) + 0.5 * _unkey(hi))
            vmid = jnp.minimum(jnp.maximum(vmid, lo + 1), jnp.maximum(hi - 1, lo + 1))
            mid = jnp.where(jnp.logical_and(it < value_steps, positive), vmid, mid)
        cand = _unkey(mid)
        cnt, = count([lambda s: s >= cand])
        ge = jnp.logical_and(open_ > 0, cnt >= kf)
        lt = jnp.logical_and(open_ > 0, cnt < kf)
        lo = jnp.where(ge, mid, lo)
        cnt_lo = jnp.where(ge, cnt, cnt_lo)
        hi = jnp.where(lt, mid, hi)
        settled = (cnt_lo == kf) | (hi == lo + 1)
        open_ = jnp.where(settled, 0, open_)
        go = jnp.where(it < max_steps, jnp.max(open_), 0)
        return it + 1, lo, hi, cnt_lo, open_, go

    if bounds is None:
        lo0 = jnp.full(qshape, _KEY_NEG_INF, I32)
        hi0 = jnp.full(qshape, _KEY_POS_INF, I32)
        cnt0 = jnp.full(qshape, -1.0, F32)
        open0 = active
    else:
        n_gt0, n_ge0 = count([lambda s: s > 0.0, lambda s: s >= 0.0])
        positive = n_gt0 >= kf
        non_negative = n_ge0 >= kf
        key_zero = _key(jnp.zeros(qshape, F32))
        lo0 = jnp.where(non_negative, key_zero, _key(bounds[0]))
        cnt0 = jnp.where(non_negative, n_ge0, -1.0)
        hi0 = jnp.where(positive, _key(bounds[1]) + 1, jnp.where(non_negative, key_zero + 1, key_zero))
        lo0 = jnp.where(active, lo0, _KEY_NEG_INF)
        hi0 = jnp.where(active, hi0, _KEY_POS_INF)
        open0 = jnp.logical_and(active, jnp.logical_not((cnt0 == kf) | (hi0 == lo0 + 1)))
    open0 = jnp.where(open0, 1, 0).astype(I32)
    _, lo, _, _, _, _ = lax.while_loop(cond, body, (jnp.int32(0), lo0, hi0, cnt0, open0, jnp.max(open0)))
    thr = jnp.where(active, _unkey(lo), -jnp.inf)
    n_gt, n_ge = count([lambda s: s > thr, lambda s: s >= thr])
    need = jnp.where(active, kf - n_gt, 0.0)
    ties = jnp.where(jnp.logical_and(active, (n_ge - n_gt) > need), 1, 0).astype(I32)
    return thr, need, ties


def _earlier_matrix(w, key_axis):
    r_i = lax.broadcasted_iota(I32, (w, w), 0)
    c_i = lax.broadcasted_iota(I32, (w, w), 1)
    return jnp.where(r_i < c_i if key_axis == 1 else c_i < r_i, 1.0, 0.0).astype(BF16)


def _selection_bias(s, thr, need, tie_flag, tie_cnt_ref, bias_ref, earlier_ref, key_axis=1):
    w = s.shape[key_axis]

    @pl.when(tie_flag == 0)
    def _():
        sel = jnp.logical_and(s >= thr, s > -jnp.inf)
        bias_ref[...] = jnp.where(sel, 0.0, NEG_BIG)

    @pl.when(tie_flag != 0)
    def _():
        step = min(w, earlier_ref.shape[0])
        earlier = earlier_ref[:step, :step]
        seen = tie_cnt_ref[...]
        for a in range(0, w, step):
            piece = (slice(None), slice(a, a + step)) if key_axis == 1 else (slice(a, a + step), slice(None))
            sp = s[piece]
            eq = sp == thr
            eqf = jnp.where(eq, 1.0, 0.0)
            if key_axis == 1:
                rank = jnp.dot(eqf.astype(BF16), earlier, preferred_element_type=F32)
            else:
                rank = jnp.dot(earlier, eqf.astype(BF16), preferred_element_type=F32)
            sel = jnp.logical_or(sp > thr, jnp.logical_and(eq, rank + seen < need))
            bias_ref[piece] = jnp.where(sel, 0.0, NEG_BIG)
            seen = seen + jnp.sum(eqf, axis=key_axis, keepdims=True)
        tie_cnt_ref[...] = seen


def _dsa_prompt_kernel(qb_ref, qi_ref, wi_ref, ki_ref, k_hbm, vt_hbm, o_ref,
                       k_scr, vt_scr, sc_scr, qm_scr, m_scr, l_scr, acc_scr, bias_scr, tie_scr, earlier_scr, sem,
                       *, q_blk, k_blk, seq, n_sel):
    b = pl.program_id(0)
    i = pl.program_id(1)

    @pl.when(i == 0)
    def _():
        ck = pltpu.make_async_copy(k_hbm.at[pl.ds(b * seq, seq)], k_scr, sem.at[0])
        cv = pltpu.make_async_copy(vt_hbm.at[:, pl.ds(b * seq, seq)], vt_scr, sem.at[1])
        ck.start()
        cv.start()
        ck.wait()
        cv.wait()

    q0 = i * q_blk
    n_blocks = (q0 + q_blk + k_blk - 1) // k_blk
    q_pos = q0 + lax.broadcasted_iota(I32, (1, q_blk), 1)

    qit = qi_ref[...].T.astype(BF16)
    qit_h = [qit[h * IDX_DIM:(h + 1) * IDX_DIM] for h in range(IDX_HEADS)]
    wit = wi_ref[...].T
    w_h = [wit[_WI_LANE + h:_WI_LANE + h + 1] for h in range(IDX_HEADS)]

    def score_body(c, carry):
        off = pl.multiple_of(c * k_blk, k_blk)
        kc = ki_ref[pl.ds(off, k_blk), :]
        s = w_h[0] * jnp.maximum(jnp.dot(kc, qit_h[0], preferred_element_type=F32), 0.0)
        for h in range(1, IDX_HEADS):
            s = s + w_h[h] * jnp.maximum(jnp.dot(kc, qit_h[h], preferred_element_type=F32), 0.0)
        k_pos = off + lax.broadcasted_iota(I32, (k_blk, 1), 0)
        adm = k_pos <= q_pos
        sc_scr[pl.ds(off, k_blk), :] = jnp.where(adm, s, -jnp.inf)
        hi_part, lo_part = carry
        hi_part = jnp.maximum(hi_part, _reduce_rows(jnp.where(adm, s, -jnp.inf), jnp.max))
        lo_part = jnp.minimum(lo_part, _reduce_rows(jnp.where(adm, s, jnp.inf), jnp.min))
        return hi_part, lo_part

    hi_part, lo_part = lax.fori_loop(
        0, n_blocks, score_body,
        (jnp.full((SUBLANES, q_blk), -jnp.inf, F32), jnp.full((SUBLANES, q_blk), jnp.inf, F32)))
    s_max = jnp.max(hi_part, axis=0, keepdims=True)
    s_min = jnp.min(lo_part, axis=0, keepdims=True)

    def read_block(c):
        return sc_scr[pl.ds(pl.multiple_of(c * k_blk, k_blk), k_blk), :]

    active = (q_pos + 1) > n_sel
    thr, need, ties = _kth_largest_threshold(read_block, n_blocks, q_blk, n_sel, active, key_axis=0,
                                             bounds=(s_min, s_max))
    tie_flag = jnp.max(ties)

    qt = (qb_ref[...] * (DSA_DH ** -0.5)).T
    row_lo = lax.broadcasted_iota(I32, (LANES, 1), 0) < DSA_DH
    for p in range(DSA_HEADS // 2):
        qp = qt[p * LANES:(p + 1) * LANES]
        qm_scr[2 * p] = jnp.where(row_lo, qp, 0.0).astype(BF16)
        qm_scr[2 * p + 1] = jnp.where(row_lo, 0.0, qp).astype(BF16)
    m_scr[...] = jnp.full(m_scr.shape, NEG_BIG, F32)
    l_scr[...] = jnp.zeros(l_scr.shape, F32)
    acc_scr[...] = jnp.zeros(acc_scr.shape, F32)
    tie_scr[...] = jnp.zeros(tie_scr.shape, F32)

    @pl.when(tie_flag != 0)
    def _():
        earlier_scr[...] = _earlier_matrix(k_blk, 0)

    def att_body(j, carry):
        off = pl.multiple_of(j * k_blk, k_blk)
        _selection_bias(sc_scr[pl.ds(off, k_blk), :], thr, need, tie_flag, tie_scr, bias_scr, earlier_scr,
                        key_axis=0)
        bias = bias_scr[...]

        def qk(h):
            p = h // 2
            kp = k_scr[pl.ds(off, k_blk), p * LANES:(p + 1) * LANES]
            return jnp.dot(kp, qm_scr[h], preferred_element_type=F32)

        prs, alphas = [], []
        for h in range(DSA_HEADS):
            logit = qk(h) + bias
            m_old = m_scr[h]
            m_new = jnp.maximum(m_old, jnp.max(_reduce_rows(logit, jnp.max), axis=0, keepdims=True))
            alpha = jnp.exp(m_old - m_new)
            pe = jnp.exp(logit - m_new)
            l_scr[h] = alpha * l_scr[h] + jnp.sum(_reduce_rows(pe, jnp.sum), axis=0, keepdims=True)
            m_scr[h] = m_new
            prs.append(pe.astype(BF16))
            alphas.append(alpha)
        for h in range(DSA_HEADS):
            rows = slice(h * DSA_DH, (h + 1) * DSA_DH)
            vth = vt_scr[rows, pl.ds(off, k_blk)]
            acc_scr[rows, :] = alphas[h] * acc_scr[rows, :] + jnp.dot(vth, prs[h], preferred_element_type=F32)
        return carry

    lax.fori_loop(0, n_blocks, att_body, 0)

    for h in range(DSA_HEADS):
        rows = slice(h * DSA_DH, (h + 1) * DSA_DH)
        acc_scr[rows, :] = acc_scr[rows, :] / l_scr[h]
    o_ref[...] = acc_scr[...].T


def _dsa_prompt(qb, qi, wi, ki16, k16, vt16, batch, seq, q_blk, k_blk):
    n_sel = min(DSA_TOPK, seq // 4)
    nq = seq // q_blk
    qspec = lambda w: pl.BlockSpec((q_blk, w), lambda b, i: (b * nq + i, 0))
    return pl.pallas_call(
        functools.partial(_dsa_prompt_kernel, q_blk=q_blk, k_blk=k_blk, seq=seq, n_sel=n_sel),
        grid=(batch, nq),
        in_specs=[qspec(DSA_W), qspec(IDX_W), qspec(LANES),
                  pl.BlockSpec((seq, IDX_DIM), lambda b, i: (b, 0)),
                  pl.BlockSpec(memory_space=pl.ANY), pl.BlockSpec(memory_space=pl.ANY)],
        out_specs=qspec(DSA_W),
        out_shape=jax.ShapeDtypeStruct((batch * seq, DSA_W), F32),
        scratch_shapes=[pltpu.VMEM((seq, DSA_W), BF16), pltpu.VMEM((DSA_W, seq), BF16),
                        pltpu.VMEM((seq, q_blk), F32),
                        pltpu.VMEM((DSA_HEADS, LANES, q_blk), BF16),
                        pltpu.VMEM((DSA_HEADS, 1, q_blk), F32), pltpu.VMEM((DSA_HEADS, 1, q_blk), F32),
                        pltpu.VMEM((DSA_W, q_blk), F32), pltpu.VMEM((k_blk, q_blk), F32),
                        pltpu.VMEM((1, q_blk), F32), pltpu.VMEM((k_blk, k_blk), BF16),
                        pltpu.SemaphoreType.DMA((2,))],
        compiler_params=_params(("arbitrary", "arbitrary")),
        name="dsa_prompt",
    )(qb, qi, wi, ki16, k16, vt16)


PAGES_PER_STEP = 16
ATTN_PARTS = 2
TIE_RANK_WIDTH = 1024
SCORE_PAGES_PER_STEP = 32
_STEP_KEYS = PAGES_PER_STEP * PAGE_SIZE
_NT = (((1,), (1,)), ((), ()))


def _page_specs(page_shape, n_pages, per_step, clamp_last):
    specs = []
    zeros = (0,) * len(page_shape)
    for s in range(per_step):
        def imap(b, c, pt, s=s):
            page = jnp.minimum(c * per_step + s, n_pages - per_step + s) if clamp_last else c * per_step + s
            return (pt[b, page],) + zeros
        specs.append(pl.BlockSpec((1,) + page_shape, imap))
    return specs


def _stack_heads(qi):
    return jnp.concatenate([qi[:, h * IDX_DIM:(h + 1) * IDX_DIM] for h in range(IDX_HEADS)], axis=0)


def _idx_score(s4, wi, t):
    s = wi[:, _WI_LANE:_WI_LANE + 1] * jnp.maximum(s4[:t], 0.0)
    for h in range(1, IDX_HEADS):
        s = s + wi[:, _WI_LANE + h:_WI_LANE + h + 1] * jnp.maximum(s4[h * t:(h + 1) * t], 0.0)
    return s


def _dsa_sample_score_kernel(pt_ref, qi_ref, wi_ref, kin_ref, *rest, t, past):
    pages = rest[:SCORE_PAGES_PER_STEP]
    sc_ref, = rest[SCORE_PAGES_PER_STEP:]
    c = pl.program_id(1)
    nch = pl.num_programs(1)
    step_keys = SCORE_PAGES_PER_STEP * PAGE_SIZE
    qi4 = _stack_heads(qi_ref[...]).astype(BF16)
    wi = wi_ref[...]
    keys_t = jnp.concatenate([p[0] for p in pages], axis=1).astype(BF16)
    off = pl.multiple_of(c * step_keys, step_keys)
    sc_ref[0, :, pl.ds(off, step_keys)] = _idx_score(
        jnp.dot(qi4, keys_t, preferred_element_type=F32), wi, t)

    @pl.when(c == nch - 1)
    def _():
        knew = jnp.concatenate([kin_ref[...], jnp.zeros((LANES - t, IDX_DIM), F32)], axis=0).astype(BF16)
        s = _idx_score(lax.dot_general(qi4, knew, _NT, preferred_element_type=F32), wi, t)
        col = lax.broadcasted_iota(I32, (t, LANES), 1)
        row = lax.broadcasted_iota(I32, (t, LANES), 0)
        sc_ref[0, :, past:past + LANES] = jnp.where(col <= row, s, -jnp.inf)


THRESHOLD_GROUP = 8


def _dsa_sample_threshold_kernel(sc_ref, thr_ref, need_ref, flag_ref, *, t, past, n_sel):
    g = sc_ref.shape[0]
    rows = g * t

    def scores(_):
        return sc_ref[...].reshape(rows, sc_ref.shape[2])

    s = scores(0)
    s_max = jnp.max(s, axis=1, keepdims=True)
    s_min = jnp.min(jnp.where(s > -jnp.inf, s, jnp.inf), axis=1, keepdims=True)
    q_in_seq = lax.broadcasted_iota(I32, (g, t, 1), 1).reshape(rows, 1)
    active = (past + 1 + q_in_seq) > n_sel
    thr, need, ties = _kth_largest_threshold(scores, 1, rows, n_sel, active, bounds=(s_min, s_max))
    thr_ref[...] = thr.reshape(g, t, 1)
    need_ref[...] = need.reshape(g, t, 1)
    flag_ref[...] = ties.astype(F32).reshape(g, t, 1)


def _dsa_sample_attn_kernel(pt_ref, qb_ref, sc_ref, thr_ref, need_ref, flag_ref, kn_ref, vn_ref, *rest,
                            t, past):
    kpages = rest[:PAGES_PER_STEP]
    vpages = rest[PAGES_PER_STEP:2 * PAGES_PER_STEP]
    o_ref, m_scr, l_scr, acc_scr, bias_scr, biasn_scr, tie_scr, earlier_scr = rest[2 * PAGES_PER_STEP:]
    c = pl.program_id(1)
    nch = pl.num_programs(1) - 1
    flag = jnp.max(flag_ref[0]).astype(I32)

    @pl.when(jnp.logical_and(c == 0, flag != 0))
    def _():
        earlier_scr[...] = _earlier_matrix(TIE_RANK_WIDTH, 1)

    @pl.when(c == 0)
    def _():
        m_scr[...] = jnp.full(m_scr.shape, NEG_BIG, F32)
        l_scr[...] = jnp.zeros(l_scr.shape, F32)
        acc_scr[...] = jnp.zeros(acc_scr.shape, F32)
        tie_scr[...] = jnp.zeros(tie_scr.shape, F32)

    rows = DSA_HEADS * t
    blockmask = (lax.broadcasted_iota(I32, (rows, DSA_W), 0) // t
                 == lax.broadcasted_iota(I32, (rows, DSA_W), 1) // DSA_DH)
    qs = qb_ref[...] * (DSA_DH ** -0.5)
    qbd = jnp.where(blockmask, jnp.concatenate([qs] * DSA_HEADS, axis=0), 0.0).astype(BF16)
    thr = thr_ref[0]
    need = need_ref[0]

    def probs(part, bias, k_op, keys_on_lanes):
        if keys_on_lanes:
            logit = jnp.dot(qbd, k_op, preferred_element_type=F32)
        else:
            logit = lax.dot_general(qbd, k_op, _NT, preferred_element_type=F32)
        logit = logit + jnp.concatenate([bias] * DSA_HEADS, axis=0)
        m_old = m_scr[part]
        m_new = jnp.maximum(m_old, jnp.max(logit, axis=-1, keepdims=True))
        alpha = jnp.exp(m_old - m_new)
        pr = jnp.exp(logit - m_new).astype(BF16)
        l_scr[part] = alpha * l_scr[part] + jnp.sum(pr.astype(F32), axis=-1, keepdims=True)
        m_scr[part] = m_new
        return pr, alpha

    def accumulate(part, pr, alpha, v_op, keys_on_lanes):
        if keys_on_lanes:
            pv = lax.dot_general(pr, v_op, _NT, preferred_element_type=F32)
        else:
            pv = jnp.dot(pr, v_op, preferred_element_type=F32)
        acc_scr[part] = alpha * acc_scr[part] + pv

    @pl.when(c < nch)
    def _():
        off = pl.multiple_of(c * _STEP_KEYS, _STEP_KEYS)
        _selection_bias(sc_ref[0, :, pl.ds(off, _STEP_KEYS)], thr, need, flag, tie_scr, bias_scr, earlier_scr)
        bias = bias_scr[...]
        per = PAGES_PER_STEP // ATTN_PARTS
        stats = []
        for part in range(ATTN_PARTS):
            kc = jnp.concatenate([p[0].reshape(DSA_W, PAGE_SIZE)
                                  for p in kpages[part * per:(part + 1) * per]], axis=1).astype(BF16)
            stats.append(probs(part, bias[:, part * per * PAGE_SIZE:(part + 1) * per * PAGE_SIZE], kc, True))
        for part in range(ATTN_PARTS):
            vc = jnp.concatenate([p[0].reshape(DSA_W, PAGE_SIZE)
                                  for p in vpages[part * per:(part + 1) * per]], axis=1).astype(BF16)
            accumulate(part, stats[part][0], stats[part][1], vc, True)

    @pl.when(c == nch)
    def _():
        _selection_bias(sc_ref[0, :, past:past + LANES], thr, need, flag, tie_scr, biasn_scr, earlier_scr)
        zpad = jnp.zeros((LANES - t, DSA_W), F32)
        kc = jnp.concatenate([kn_ref[...], zpad], axis=0).astype(BF16)
        vc = jnp.concatenate([vn_ref[...], zpad], axis=0).astype(BF16)
        pr, alpha = probs(0, biasn_scr[...], kc, False)
        accumulate(0, pr, alpha, vc, False)
        m_all = m_scr[0]
        for part in range(1, ATTN_PARTS):
            m_all = jnp.maximum(m_all, m_scr[part])
        l_all = jnp.zeros_like(m_all)
        acc_all = jnp.zeros(acc_scr.shape[1:], F32)
        for part in range(ATTN_PARTS):
            w_part = jnp.exp(m_scr[part] - m_all)
            l_all = l_all + w_part * l_scr[part]
            acc_all = acc_all + w_part * acc_scr[part]
        outn = jnp.where(blockmask, acc_all / l_all, 0.0)
        out = outn[:t]
        for h in range(1, DSA_HEADS):
            out = out + outn[h * t:(h + 1) * t]
        o_ref[...] = out


def _dsa_sample(qb, qi, wi, ki_new, k_new, v_new, cache_k, cache_v, cache_idx_k, page_table, batch, t):
    n_pages = page_table.shape[1]
    past = n_pages * PAGE_SIZE
    n_sel = min(DSA_TOPK, (past + t) // 4)
    nch = n_pages // PAGES_PER_STEP
    nch_score = n_pages // SCORE_PAGES_PER_STEP
    lp = past + LANES
    kv_page = (DSA_HEADS, DSA_DH, PAGE_SIZE)
    ck = jnp.transpose(cache_k, (0, 2, 3, 1))
    cv = jnp.transpose(cache_v, (0, 2, 3, 1))
    cik = jnp.transpose(cache_idx_k, (0, 2, 1))
    rspec = lambda w: pl.BlockSpec((t, w), lambda b, c, pt: (b, 0))
    bspec = lambda w: pl.BlockSpec((1, t, w), lambda b, c, pt: (b, 0, 0))
    scores = pl.pallas_call(
        functools.partial(_dsa_sample_score_kernel, t=t, past=past),
        grid_spec=pltpu.PrefetchScalarGridSpec(
            num_scalar_prefetch=1,
            grid=(batch, nch_score),
            in_specs=[rspec(IDX_W), rspec(LANES), rspec(IDX_DIM)]
            + _page_specs((IDX_DIM, PAGE_SIZE), n_pages, SCORE_PAGES_PER_STEP, False),
            out_specs=bspec(lp),
        ),
        out_shape=jax.ShapeDtypeStruct((batch, t, lp), F32),
        compiler_params=_params(("arbitrary", "arbitrary")),
        name="dsa_sample_score",
    )(page_table, qi, wi, ki_new, *([cik] * SCORE_PAGES_PER_STEP))
    grp = THRESHOLD_GROUP if batch % THRESHOLD_GROUP == 0 else 1
    gspec = lambda w: pl.BlockSpec((grp, t, w), lambda i: (i, 0, 0))
    thr, need, flag = pl.pallas_call(
        functools.partial(_dsa_sample_threshold_kernel, t=t, past=past, n_sel=n_sel),
        grid=(batch // grp,),
        in_specs=[gspec(lp)],
        out_specs=[gspec(1)] * 3,
        out_shape=[jax.ShapeDtypeStruct((batch, t, 1), F32)] * 3,
        compiler_params=_params(("parallel",)),
        name="dsa_sample_threshold",
    )(scores)
    return pl.pallas_call(
        functools.partial(_dsa_sample_attn_kernel, t=t, past=past),
        grid_spec=pltpu.PrefetchScalarGridSpec(
            num_scalar_prefetch=1,
            grid=(batch, nch + 1),
            in_specs=[rspec(DSA_W), bspec(lp), bspec(1), bspec(1), bspec(1), rspec(DSA_W), rspec(DSA_W)]
            + _page_specs(kv_page, n_pages, PAGES_PER_STEP, True)
            + _page_specs(kv_page, n_pages, PAGES_PER_STEP, True),
            out_specs=rspec(DSA_W),
            scratch_shapes=[pltpu.VMEM((ATTN_PARTS, DSA_HEADS * t, 1), F32),
                            pltpu.VMEM((ATTN_PARTS, DSA_HEADS * t, 1), F32),
                            pltpu.VMEM((ATTN_PARTS, DSA_HEADS * t, DSA_W), F32), pltpu.VMEM((t, _STEP_KEYS), F32),
                            pltpu.VMEM((t, LANES), F32), pltpu.VMEM((t, 1), F32),
                            pltpu.VMEM((TIE_RANK_WIDTH, TIE_RANK_WIDTH), BF16)],
        ),
        out_shape=jax.ShapeDtypeStruct((batch * t, DSA_W), F32),
        compiler_params=_params(("arbitrary", "arbitrary")),
        name="dsa_sample_attn",
    )(page_table, qb, scores, thr, need, flag, k_new, v_new,
      *([ck] * PAGES_PER_STEP), *([cv] * PAGES_PER_STEP))


MOE_ROWS = 256
ROUTE_TILE = 512
TOKEN_TILE = 256


def _split_bf16(a):
    hi = a.astype(BF16)
    lo = (a - hi.astype(F32)).astype(BF16)
    return hi, lo


def _router_kernel(x_ref, whi_ref, wlo_ref, b_ref, idx_ref, gate_ref, rank_ref, cnt_ref, carry_scr):
    i = pl.program_id(0)
    tm = x_ref.shape[0]

    @pl.when(i == 0)
    def _():
        carry_scr[...] = jnp.zeros(carry_scr.shape, F32)

    xhi, xlo = _split_bf16(x_ref[...])
    whi = whi_ref[...]
    logits = (jnp.dot(xhi, whi, preferred_element_type=F32)
              + jnp.dot(xlo, whi, preferred_element_type=F32)
              + jnp.dot(xhi, wlo_ref[...], preferred_element_type=F32)) + b_ref[...]
    lane = lax.broadcasted_iota(I32, (tm, N_EXPERTS), 1)
    slot = lax.broadcasted_iota(I32, (tm, LANES), 1)
    vals, idxs = [], []
    cur = logits
    for _ in range(TOP_K):
        m = jnp.max(cur, axis=-1, keepdims=True)
        ix = jnp.min(jnp.where(cur == m, lane, N_EXPERTS), axis=-1, keepdims=True)
        vals.append(m)
        idxs.append(ix)
        cur = jnp.where(lane == ix, -jnp.inf, cur)
    es = [jnp.exp(v - vals[0]) for v in vals]
    denom = es[0] + es[1] + es[2] + es[3]
    onehot = jnp.zeros((tm, N_EXPERTS), F32)
    for ix in idxs:
        onehot = onehot + jnp.where(lane == ix, 1.0, 0.0)
    earlier = (lax.broadcasted_iota(I32, (tm, tm), 1) < lax.broadcasted_iota(I32, (tm, tm), 0))
    excl = jnp.dot(jnp.where(earlier, 1.0, 0.0).astype(BF16), onehot.astype(BF16),
                   preferred_element_type=F32) + carry_scr[...]
    idx_out = jnp.zeros((tm, LANES), I32)
    gate_out = jnp.zeros((tm, LANES), F32)
    rank_out = jnp.zeros((tm, LANES), F32)
    for k in range(TOP_K):
        rk = jnp.sum(jnp.where(lane == idxs[k], excl, 0.0), axis=-1, keepdims=True)
        idx_out = jnp.where(slot == k, idxs[k], idx_out)
        gate_out = jnp.where(slot == k, es[k] / denom, gate_out)
        rank_out = jnp.where(slot == k, rk, rank_out)
    idx_ref[...] = idx_out
    gate_ref[...] = gate_out
    rank_ref[...] = rank_out.astype(I32)
    total = carry_scr[...] + jnp.sum(onehot, axis=0, keepdims=True)
    carry_scr[...] = total
    cnt_ref[...] = total


def _router(x, w_router, b_router):
    n = x.shape[0]
    tm = ROUTE_TILE if n % ROUTE_TILE == 0 else TOKEN_TILE
    whi, wlo = _split_bf16(w_router)
    b2 = b_router.reshape(1, N_EXPERTS)
    kspec = _row_spec(tm, LANES)
    return pl.pallas_call(
        _router_kernel,
        grid=(n // tm,),
        in_specs=[_row_spec(tm, D_MODEL), _full_spec(whi), _full_spec(wlo), _full_spec(b2)],
        out_specs=[kspec, kspec, kspec, pl.BlockSpec((1, N_EXPERTS), lambda i: (0, 0))],
        out_shape=[jax.ShapeDtypeStruct((n, LANES), I32), jax.ShapeDtypeStruct((n, LANES), F32),
                   jax.ShapeDtypeStruct((n, LANES), I32), jax.ShapeDtypeStruct((1, N_EXPERTS), F32)],
        scratch_shapes=[pltpu.VMEM((1, N_EXPERTS), F32)],
        compiler_params=_params(("arbitrary",)),
        name="router",
    )(x, whi, wlo, b2)


def _row_copy(src, src_row, dst, dst_row, sem):
    return pltpu.make_async_copy(src.at[pl.ds(src_row, 1)], dst.at[pl.ds(dst_row, 1)], sem)


def _dispatch_kernel(dest_ref, x_ref, xs_in, xs_out, sem):
    del xs_in
    tm = x_ref.shape[0]

    def issue(r, c):
        for k in range(TOP_K):
            _row_copy(x_ref, r, xs_out, dest_ref[r * TOP_K + k], sem).start(priority=k % 2)
        return c

    lax.fori_loop(0, tm, issue, 0)
    for _ in range(TOP_K):
        pltpu.make_async_copy(x_ref, x_ref, sem).wait()


def _dispatch(x, dest_flat, n_rows):
    n = x.shape[0]
    tm = TOKEN_TILE
    return pl.pallas_call(
        _dispatch_kernel,
        grid=(n // tm,),
        in_specs=[pl.BlockSpec((tm * TOP_K,), lambda i: (i,), memory_space=pltpu.SMEM),
                  _row_spec(tm, D_MODEL), pl.BlockSpec(memory_space=pl.ANY)],
        out_specs=pl.BlockSpec(memory_space=pl.ANY),
        out_shape=jax.ShapeDtypeStruct((n_rows, D_MODEL), F32),
        scratch_shapes=[pltpu.SemaphoreType.DMA(())],
        input_output_aliases={2: 0},
        compiler_params=_params(("arbitrary",)),
        name="moe_dispatch",
    )(dest_flat, x, jnp.zeros((n_rows, D_MODEL), F32))


def _expert_kernel(be_ref, na_ref, x_ref, w1_ref, b1_ref, w2_ref, b2_ref, o_ref, w1b_scr, w2b_scr):
    i = pl.program_id(0)
    active = i < na_ref[0]
    changed = jnp.logical_or(i == 0, be_ref[i] != be_ref[jnp.maximum(i - 1, 0)])

    @pl.when(jnp.logical_and(active, changed))
    def _():
        w1b_scr[...] = w1_ref[0].astype(BF16)
        w2b_scr[...] = w2_ref[0].astype(BF16)

    @pl.when(active)
    def _():
        h = jnp.dot(x_ref[...].astype(BF16), w1b_scr[...], preferred_element_type=F32) + b1_ref[0]
        gt = jnp.minimum(h[:, :D_FF], SWIGLU_LIMIT)
        up = jnp.clip(h[:, D_FF:], -SWIGLU_LIMIT, SWIGLU_LIMIT)
        act = (up + 1.0) * gt * jax.nn.sigmoid(SWIGLU_ALPHA * gt)
        o_ref[...] = jnp.dot(act.astype(BF16), w2b_scr[...], preferred_element_type=F32) + b2_ref[0]

    @pl.when(jnp.logical_not(active))
    def _():
        o_ref[...] = jnp.zeros(o_ref.shape, F32)


def _experts(xs, block_e, n_active, w1, b1, w2, b2):
    n_rows = xs.shape[0]
    nblk = n_rows // MOE_ROWS
    w1 = w1.reshape(-1, D_MODEL, 2 * D_FF)
    w2 = w2.reshape(-1, D_FF, D_MODEL)
    b1r = b1.reshape(-1, 1, 2 * D_FF)
    b2r = b2.reshape(-1, 1, D_MODEL)
    last = lambda i, na: jnp.minimum(i, na[0] - 1)
    return pl.pallas_call(
        _expert_kernel,
        grid_spec=pltpu.PrefetchScalarGridSpec(
            num_scalar_prefetch=2,
            grid=(nblk,),
            in_specs=[pl.BlockSpec((MOE_ROWS, D_MODEL), lambda i, be, na: (last(i, na), 0)),
                      pl.BlockSpec((1, D_MODEL, 2 * D_FF), lambda i, be, na: (be[i], 0, 0)),
                      pl.BlockSpec((1, 1, 2 * D_FF), lambda i, be, na: (be[i], 0, 0)),
                      pl.BlockSpec((1, D_FF, D_MODEL), lambda i, be, na: (be[i], 0, 0)),
                      pl.BlockSpec((1, 1, D_MODEL), lambda i, be, na: (be[i], 0, 0))],
            out_specs=pl.BlockSpec((MOE_ROWS, D_MODEL), lambda i, be, na: (i, 0)),
            scratch_shapes=[pltpu.VMEM((D_MODEL, 2 * D_FF), BF16), pltpu.VMEM((D_FF, D_MODEL), BF16)],
        ),
        out_shape=jax.ShapeDtypeStruct((n_rows, D_MODEL), F32),
        compiler_params=_params(("arbitrary",)),
        name="moe_experts",
    )(block_e, n_active, xs, w1, b1r, w2, b2r)


def _combine_kernel(dest_ref, x_ref, gate_ref, g_ref, b_ref, yb_hbm, *rest, first_tiles):
    outs, (buf, sem) = rest[:-2], rest[-2:]
    tm = x_ref.shape[0]

    def issue(r, c):
        for k in range(TOP_K):
            _row_copy(yb_hbm, dest_ref[r * TOP_K + k], buf.at[k], r, sem).start(priority=k % 2)
        return c

    lax.fori_loop(0, tm, issue, 0)
    for k in range(TOP_K):
        pltpu.make_async_copy(buf.at[k], buf.at[k], sem).wait()
    gate = gate_ref[...]
    y = gate[:, 0:1] * buf[0]
    for k in range(1, TOP_K):
        y = y + gate[:, k:k + 1] * buf[k]
    res = _layer_norm(DN_ALPHA * x_ref[...] + y, g_ref[...], b_ref[...])
    if first_tiles is None:
        outs[0][...] = res
    else:
        i = pl.program_id(0)

        @pl.when(i < first_tiles)
        def _():
            outs[0][...] = res

        @pl.when(i >= first_tiles)
        def _():
            outs[1][...] = res


def _combine(x, yb, dest_flat, gate, g, b, split=None):
    n = x.shape[0]
    tm = TOKEN_TILE
    g2, b2 = g.reshape(1, D_MODEL), b.reshape(1, D_MODEL)
    if split is None:
        first_tiles = None
        out_specs = _row_spec(tm, D_MODEL)
        out_shape = jax.ShapeDtypeStruct((n, D_MODEL), F32)
    else:
        first_tiles = split // tm
        out_specs = [pl.BlockSpec((tm, D_MODEL), lambda i: (jnp.minimum(i, first_tiles - 1), 0)),
                     pl.BlockSpec((tm, D_MODEL), lambda i: (jnp.maximum(i - first_tiles, 0), 0))]
        out_shape = [jax.ShapeDtypeStruct((split, D_MODEL), F32), jax.ShapeDtypeStruct((n - split, D_MODEL), F32)]
    return pl.pallas_call(
        functools.partial(_combine_kernel, first_tiles=first_tiles),
        grid=(n // tm,),
        in_specs=[pl.BlockSpec((tm * TOP_K,), lambda i: (i,), memory_space=pltpu.SMEM),
                  _row_spec(tm, D_MODEL), _row_spec(tm, LANES),
                  _full_spec(g2), _full_spec(b2), pl.BlockSpec(memory_space=pl.ANY)],
        out_specs=out_specs,
        out_shape=out_shape,
        scratch_shapes=[pltpu.VMEM((TOP_K, tm, D_MODEL), F32), pltpu.SemaphoreType.DMA(())],
        compiler_params=_params(("arbitrary",)),
        name="moe_combine",
    )(dest_flat, x, gate, g2, b2, yb)


def _moe_layer(x, layer, w_router, b_router, w1, b1, w2, b2, g, b, split=None):
    n = x.shape[0]
    idx, gate, rank, counts = _router(x, w_router, b_router)
    nblk = (n * TOP_K) // MOE_ROWS + N_EXPERTS
    cnt = counts[0].astype(I32)
    padded = (cnt + MOE_ROWS - 1) // MOE_ROWS * MOE_ROWS
    pad_end = jnp.cumsum(padded)
    pad_start = pad_end - padded
    dest = (pad_start[idx] + rank)[:, :TOP_K].reshape(-1)
    n_active = (pad_end[-1] // MOE_ROWS).astype(I32)
    blk = jnp.arange(nblk, dtype=I32)
    blk = jnp.minimum(blk, n_active - 1)
    block_e = jnp.sum((pad_end[None, :] <= (blk * MOE_ROWS)[:, None]).astype(I32), axis=1)
    block_e = jnp.minimum(block_e, N_EXPERTS - 1) + layer * N_EXPERTS
    xs = _dispatch(x, dest, nblk * MOE_ROWS)
    yb = _experts(xs, block_e, n_active.reshape(1), w1, b1, w2, b2)
    return _combine(x, yb, dest, gate, g, b, split)


POOL_HALO = 16


def _pool_kernel(x_ref, halo_ref, w_ref, sc_ref, g_ref, b_ref, o_ref, ext_scr, *, tiles_per_seq, n_prev):
    tm = x_ref.shape[0]
    tile = pl.program_id(0) % tiles_per_seq
    x = x_ref[...]
    halo = halo_ref[...]
    if n_prev == 0:
        halo = jnp.where(tile == 0, 0.0, halo)
    ext_scr[0:POOL_HALO] = halo
    ext_scr[POOL_HALO:POOL_HALO + tm] = x
    pos = n_prev + tile * tm + lax.broadcasted_iota(I32, (tm, 1), 0)
    parts = []
    for g, w in enumerate(POOL_WINDOWS):
        cols = slice(g * POOL_GROUP, (g + 1) * POOL_GROUP)
        xg = x[:, cols]
        s = xg
        for d in range(1, w):
            s = s + ext_scr[POOL_HALO - d:POOL_HALO - d + tm, cols]
        cnt = jnp.minimum(pos + 1, w).astype(F32)
        dg = s / cnt - xg
        parts.append(jnp.dot(dg.astype(BF16), w_ref[g], preferred_element_type=F32))
    mix = jnp.concatenate(parts, axis=1) * sc_ref[...]
    o_ref[...] = _layer_norm(DN_ALPHA * x + mix, g_ref[...], b_ref[...])


def _pool(x, halo_src, pool_w, pool_scale, g, b, tm, tiles_per_seq, n_prev, halo_map, n, row_off=0):
    w16 = pool_w.astype(BF16)
    sc = pool_scale.reshape(1, D_MODEL)
    g2, b2 = g.reshape(1, D_MODEL), b.reshape(1, D_MODEL)
    blk_off = row_off // tm
    return pl.pallas_call(
        functools.partial(_pool_kernel, tiles_per_seq=tiles_per_seq, n_prev=n_prev),
        grid=(n // tm,),
        in_specs=[pl.BlockSpec((tm, D_MODEL), lambda i: (i + blk_off, 0)),
                  pl.BlockSpec((POOL_HALO, D_MODEL), halo_map),
                  _full_spec(w16), _full_spec(sc), _full_spec(g2), _full_spec(b2)],
        out_specs=_row_spec(tm, D_MODEL),
        out_shape=jax.ShapeDtypeStruct((n, D_MODEL), F32),
        scratch_shapes=[pltpu.VMEM((POOL_HALO + tm, D_MODEL), F32)],
        compiler_params=_params(("parallel",)),
        name="pool",
    )(x, halo_src, w16, sc, g2, b2)


PROJ_TILE = 512
DSA_Q_BLOCK = 512
DSA_K_BLOCK = 512
POOL_TILE = 512


def kernel(x_prompt, x_sample, cache_k, cache_v, cache_idx_k, page_table, state_gla, state_pool,
           w_in, gla_fg_w2, gla_fg_b, gla_norm_g, idx_kn_g, idx_kn_b, w_out,
           pool_w, pool_scale, ln_mix_g, ln_mix_b, ln_ffn_g, ln_ffn_b,
           moe_router_w, moe_router_b, moe_w1, moe_b1, moe_w2, moe_b2):
    bp, sp, _ = x_prompt.shape
    bs, ts, _ = x_sample.shape
    n_p, n_s = bp * sp, bs * ts
    xp = x_prompt.reshape(n_p, D_MODEL)
    xs = x_sample.reshape(n_s, D_MODEL)
    proj_w = (w_in, gla_fg_w2, gla_fg_b, idx_kn_g, idx_kn_b)

    (qa, ka, va, ga, lf, qb, _, _, qi, _, wi, kb16, vt16, ki16,
     kt, vt, kit) = _project(xp, *proj_w, tm=PROJ_TILE, seq=sp)
    oa_p, gla_p = _gla(qa, ka, va, lf, ga, jnp.zeros((bp, GLA_HEADS, GLA_DK, GLA_DV), F32), gla_norm_g, bp, sp)
    ob_p = _dsa_prompt(qb, qi, wi, ki16, kb16, vt16, bp, sp, DSA_Q_BLOCK, DSA_K_BLOCK)
    x1p = _merge(xp, oa_p, ob_p, w_out, ln_mix_g[0], ln_mix_b[0], PROJ_TILE)
    k_p = jnp.transpose(kt.reshape(bp, DSA_HEADS, DSA_DH, sp), (0, 3, 1, 2))
    v_p = jnp.transpose(vt.reshape(bp, DSA_HEADS, DSA_DH, sp), (0, 3, 1, 2))
    kidx_p = jnp.transpose(kit, (0, 2, 1))

    qa, ka, va, ga, lf, qb, kb, vb, qi, ki, wi, _, _, _ = _project(xs, *proj_w, tm=n_s)
    oa_s, gla_s = _gla(qa, ka, va, lf, ga, state_gla, gla_norm_g, bs, ts)
    ob_s = _dsa_sample(qb, qi, wi, ki, kb, vb, cache_k, cache_v, cache_idx_k, page_table, bs, ts)
    x1s = _merge(xs, oa_s, ob_s, w_out, ln_mix_g[0], ln_mix_b[0], n_s)
    k_s = kb.reshape(bs, ts, DSA_HEADS, DSA_DH)
    v_s = vb.reshape(bs, ts, DSA_HEADS, DSA_DH)
    kidx_s = ki.reshape(bs, ts, IDX_DIM)

    x_all = _moe_layer(jnp.concatenate([x1p, x1s], axis=0), 0, moe_router_w[0], moe_router_b[0],
                       moe_w1, moe_b1, moe_w2, moe_b2, ln_ffn_g[0], ln_ffn_b[0])

    pool_p = jnp.stack([x_all[(s + 1) * sp - POOL_STATE:(s + 1) * sp] for s in range(bp)])
    xs3 = x_all[n_p:].reshape(bs, ts, D_MODEL)
    pool_s = jnp.concatenate([state_pool, xs3], axis=1)[:, -POOL_STATE:]
    per_seq = sp // POOL_TILE
    halo_step = POOL_TILE // POOL_HALO
    x2p = _pool(x_all, x_all, pool_w, pool_scale, ln_mix_g[1], ln_mix_b[1], POOL_TILE, per_seq, 0,
                lambda i: (jnp.maximum(i * halo_step - 1, 0), 0), n_p)
    halo_s = jnp.concatenate([jnp.zeros((bs, POOL_HALO - POOL_STATE, D_MODEL), F32), state_pool], axis=1)
    x2s = _pool(x_all, halo_s.reshape(bs * POOL_HALO, D_MODEL), pool_w, pool_scale, ln_mix_g[1], ln_mix_b[1],
                ts, 1, POOL_STATE, lambda i: (i, 0), n_s, row_off=n_p)
    y_p, y_s = _moe_layer(jnp.concatenate([x2p, x2s], axis=0), 1, moe_router_w[1], moe_router_b[1],
                          moe_w1, moe_b1, moe_w2, moe_b2, ln_ffn_g[1], ln_ffn_b[1], split=n_p)
    y_p = y_p.reshape(bp, sp, D_MODEL)
    y_s = y_s.reshape(bs, ts, D_MODEL)
    return (y_p, y_s, k_p, v_p, kidx_p, gla_p, pool_p, k_s, v_s, kidx_s, gla_s, pool_s)
```

```python
import functools
import math

import jax
import jax.numpy as jnp
from jax import lax
from jax.experimental import pallas as pl
from jax.experimental.pallas import tpu as pltpu

F32 = jnp.float32
BF16 = jnp.bfloat16
I32 = jnp.int32

D_MODEL = 1024
DEPTH = 2
PAGE_SIZE = 128
GLA_HEADS = 4
GLA_DK = 64
GLA_DV = 128
GLA_GATE_RANK = 16
GLA_TAU = 16.0
GLA_CHUNK = 64
GLA_TILE = 2 * GLA_CHUNK
DSA_HEADS = 8
DSA_DH = 64
IDX_HEADS = 4
IDX_DIM = 64
DSA_TOPK = 256
IDX_W_SCALE = (IDX_HEADS ** -0.5) * (IDX_DIM ** -0.5)
POOL_WINDOWS = (2, 4, 8, 16)
POOL_GROUP = D_MODEL // 4
POOL_STATE = 16 - 1
N_EXPERTS = 32
TOP_K = 4
D_FF = D_MODEL
SWIGLU_ALPHA = 1.702
SWIGLU_LIMIT = 7.0
DN_ALPHA = (2 * DEPTH) ** 0.25
LN_EPS = 1e-5
GLA_QK = GLA_HEADS * GLA_DK
GLA_V = GLA_HEADS * GLA_DV
DSA_W = DSA_HEADS * DSA_DH
IDX_W = IDX_HEADS * IDX_DIM

LANES = 128
SUBLANES = 8
VMEM_LIMIT_BYTES = 56 * 1024 * 1024

NEG_BIG = -1e30


def _params(sem, vmem=VMEM_LIMIT_BYTES):
    return pltpu.CompilerParams(dimension_semantics=sem, vmem_limit_bytes=vmem)


def _layer_norm(x, g, b):
    mu = jnp.mean(x, axis=-1, keepdims=True)
    xc = x - mu
    var = jnp.mean(xc * xc, axis=-1, keepdims=True)
    return xc * lax.rsqrt(var + LN_EPS) * g + b


def _row_spec(tm, w):
    return pl.BlockSpec((tm, w), lambda i: (i, 0))


def _full_spec(a):
    nd = a.ndim
    return pl.BlockSpec(a.shape, lambda i: (0,) * nd)


_MAIN_W = (GLA_QK, GLA_QK, GLA_V, GLA_V, DSA_W, DSA_W, DSA_W, IDX_W)
_MAIN_OFF = tuple(sum(_MAIN_W[:i]) for i in range(len(_MAIN_W) + 1))


def _proj_kernel(x_ref, wm_ref, ws_ref, fgw_ref, fgb_ref, kng_ref, knb_ref,
                 qa_ref, ka_ref, va_ref, ga_ref, lf_ref, qb_ref, kb_ref, vb_ref,
                 qi_ref, ki_ref, wi_ref, kb16_ref, vt16_ref, ki16_ref, *channel_major):
    xb = x_ref[...].astype(BF16)

    def mm(n):
        return jnp.dot(xb, wm_ref[:, _MAIN_OFF[n]:_MAIN_OFF[n + 1]], preferred_element_type=F32)

    qa_ref[...] = mm(0) * (GLA_DK ** -0.5)
    ka_ref[...] = mm(1)
    va_ref[...] = mm(2)
    ga_ref[...] = mm(3)
    qb_ref[...] = mm(4)
    kb = mm(5)
    kb_ref[...] = kb
    kb16_ref[...] = kb.astype(BF16)
    vb = mm(6)
    vb_ref[...] = vb
    vbt = vb.T
    vt16_ref[...] = vbt.astype(BF16)
    qi_ref[...] = mm(7)

    small = jnp.dot(xb, ws_ref[...], preferred_element_type=F32)
    ki = _layer_norm(small[:, :IDX_DIM], kng_ref[...], knb_ref[...])
    ki_ref[...] = ki
    ki16_ref[...] = ki.astype(BF16)
    if channel_major:
        kt_ref, vt_ref, kit_ref = channel_major
        kt_ref[0] = kb.T
        vt_ref[0] = vbt
        kit_ref[0] = jnp.concatenate([ki, jnp.zeros_like(ki)], axis=1).T[:IDX_DIM]
    fa = small[:, IDX_DIM:IDX_DIM + GLA_GATE_RANK]
    z = jnp.dot(fa, fgw_ref[...], preferred_element_type=F32,
                precision=lax.Precision.HIGHEST) + fgb_ref[...]
    lf_ref[...] = (jnp.minimum(z, 0.0) - jnp.log(1.0 + jnp.exp(-jnp.abs(z)))) * (1.0 / GLA_TAU)
    wi_ref[...] = small * IDX_W_SCALE


def _project(x2d, w_in, gla_fg_w2, gla_fg_b, idx_kn_g, idx_kn_b, tm, seq=None):
    n = x2d.shape[0]
    pts = [0]
    for s in (GLA_QK, GLA_QK, GLA_V, GLA_V, GLA_GATE_RANK, DSA_W, DSA_W, DSA_W, IDX_W, IDX_DIM, IDX_HEADS):
        pts.append(pts[-1] + s)
    seg = lambda i: w_in[:, pts[i]:pts[i + 1]]
    wm = jnp.concatenate([seg(0), seg(1), seg(2), seg(3), seg(5), seg(6), seg(7), seg(8)], axis=1).astype(BF16)
    pad = LANES - IDX_DIM - GLA_GATE_RANK - IDX_HEADS
    ws = jnp.concatenate([seg(9), seg(4), seg(10), jnp.zeros((D_MODEL, pad), w_in.dtype)], axis=1).astype(BF16)
    fgb = gla_fg_b.reshape(1, GLA_QK)
    kng = idx_kn_g.reshape(1, IDX_DIM)
    knb = idx_kn_b.reshape(1, IDX_DIM)
    widths = (GLA_QK, GLA_QK, GLA_V, GLA_V, GLA_QK, DSA_W, DSA_W, DSA_W, IDX_W, IDX_DIM, LANES)
    out_shape = [jax.ShapeDtypeStruct((n, w), F32) for w in widths]
    out_shape += [jax.ShapeDtypeStruct((n, DSA_W), BF16), jax.ShapeDtypeStruct((DSA_W, n), BF16),
                  jax.ShapeDtypeStruct((n, IDX_DIM), BF16)]
    out_specs = [_row_spec(tm, w) for w in widths]
    out_specs += [_row_spec(tm, DSA_W), pl.BlockSpec((DSA_W, tm), lambda i: (0, i)), _row_spec(tm, IDX_DIM)]
    if seq is not None:
        tps = seq // tm
        for w in (DSA_W, DSA_W, IDX_DIM):
            out_shape.append(jax.ShapeDtypeStruct((n // seq, w, seq), F32))
            out_specs.append(pl.BlockSpec((1, w, tm), lambda i: (i // tps, 0, i % tps)))
    return pl.pallas_call(
        _proj_kernel,
        grid=(n // tm,),
        in_specs=[_row_spec(tm, D_MODEL), _full_spec(wm), _full_spec(ws), _full_spec(gla_fg_w2),
                  _full_spec(fgb), _full_spec(kng), _full_spec(knb)],
        out_specs=out_specs,
        out_shape=out_shape,
        compiler_params=_params(("parallel",)),
        name="proj",
    )(x2d, wm, ws, gla_fg_w2, fgb, kng, knb)


def _gla_kernel(q_ref, k_ref, v_ref, lf_ref, ga_ref, s0_ref, ng_ref, o_ref, sout_ref, s_scr, *, chunk):
    c = pl.program_id(1)
    nc = pl.num_programs(1)

    @pl.when(c == 0)
    def _():
        s_scr[...] = s0_ref[0]

    g = lf_ref[...]
    ri = lax.broadcasted_iota(I32, (chunk, chunk), 0)
    ci = lax.broadcasted_iota(I32, (chunk, chunk), 1)
    causal = ci <= ri
    tri = jnp.where(causal, 1.0, 0.0).astype(F32)
    b = jnp.dot(tri, g, preferred_element_type=F32, precision=lax.Precision.HIGHEST)
    mid = chunk // 2
    b_mid = b[mid:mid + 1, :]
    b_last = b[chunk - 1:chunk, :]
    q = q_ref[...]
    k = k_ref[...]
    q_in = q * jnp.exp(b)
    q_rel = q * jnp.exp(b - b_mid)
    k_rel = k * jnp.exp(b_mid - b)
    k_out = k * jnp.exp(b_last - b)
    dec_last = jnp.exp(b_last)
    v = v_ref[...]
    ga = ga_ref[...]
    ng = ng_ref[...]
    eye = jnp.where(lax.broadcasted_iota(I32, (GLA_DK, GLA_DK), 0)
                    == lax.broadcasted_iota(I32, (GLA_DK, GLA_DK), 1), 1.0, 0.0).astype(F32)
    for h in range(GLA_HEADS):
        ks = slice(h * GLA_DK, (h + 1) * GLA_DK)
        vs = slice(h * GLA_DV, (h + 1) * GLA_DV)
        s_h = s_scr[h]
        vh = v[:, vs]
        inter = jnp.dot(q_in[:, ks], s_h, preferred_element_type=F32)
        att = lax.dot_general(q_rel[:, ks], k_rel[:, ks], (((1,), (1,)), ((), ())),
                              preferred_element_type=F32)
        att = jnp.where(causal, att, 0.0)
        o = inter + jnp.dot(att, vh, preferred_element_type=F32)
        kv = lax.dot_general(k_out[:, ks], vh, (((0,), (0,)), ((), ())), preferred_element_type=F32)
        s_scr[h] = jnp.dot(eye * dec_last[:, ks], s_h, preferred_element_type=F32,
                           precision=lax.Precision.HIGHEST) + kv
        ms = jnp.mean(o * o, axis=-1, keepdims=True)
        gh = ga[:, vs]
        o_ref[:, vs] = o * lax.rsqrt(ms + LN_EPS) * ng * (gh * jax.nn.sigmoid(gh))

    @pl.when(c == nc - 1)
    def _():
        sout_ref[0] = s_scr[...]


def _gla(qa, ka, va, lf, ga, s0, gla_norm_g, batch, seq):
    chunk = math.gcd(seq, GLA_TILE)
    nc = seq // chunk
    ng = gla_norm_g.reshape(1, GLA_DV)
    spec = lambda w: pl.BlockSpec((chunk, w), lambda b, c: (b * nc + c, 0))
    sspec = pl.BlockSpec((1, GLA_HEADS, GLA_DK, GLA_DV), lambda b, c: (b, 0, 0, 0))
    return pl.pallas_call(
        functools.partial(_gla_kernel, chunk=chunk),
        grid=(batch, nc),
        in_specs=[spec(GLA_QK), spec(GLA_QK), spec(GLA_V), spec(GLA_QK), spec(GLA_V), sspec,
                  pl.BlockSpec((1, GLA_DV), lambda b, c: (0, 0))],
        out_specs=[spec(GLA_V), sspec],
        out_shape=[jax.ShapeDtypeStruct((batch * seq, GLA_V), F32),
                   jax.ShapeDtypeStruct((batch, GLA_HEADS, GLA_DK, GLA_DV), F32)],
        scratch_shapes=[pltpu.VMEM((GLA_HEADS, GLA_DK, GLA_DV), F32)],
        compiler_params=_params(("parallel", "arbitrary")),
        name="gla",
    )(qa, ka, va, lf, ga, s0, ng)


def _merge_kernel(x_ref, oa_ref, ob_ref, w_ref, g_ref, b_ref, o_ref):
    mix = jnp.dot(oa_ref[...].astype(BF16), w_ref[:GLA_V, :], preferred_element_type=F32)
    mix = mix + jnp.dot(ob_ref[...].astype(BF16), w_ref[GLA_V:, :], preferred_element_type=F32)
    o_ref[...] = _layer_norm(DN_ALPHA * x_ref[...] + mix, g_ref[...], b_ref[...])


def _merge(x2d, oa, ob, w_out, g, b, tm):
    n = x2d.shape[0]
    w16 = w_out.astype(BF16)
    g2, b2 = g.reshape(1, D_MODEL), b.reshape(1, D_MODEL)
    return pl.pallas_call(
        _merge_kernel,
        grid=(n // tm,),
        in_specs=[_row_spec(tm, D_MODEL), _row_spec(tm, GLA_V), _row_spec(tm, DSA_W),
                  _full_spec(w16), _full_spec(g2), _full_spec(b2)],
        out_specs=_row_spec(tm, D_MODEL),
        out_shape=jax.ShapeDtypeStruct((n, D_MODEL), F32),
        compiler_params=_params(("parallel",)),
        name="merge",
    )(x2d, oa, ob, w16, g2, b2)


_KEY_NEG_INF = -2139095041
_KEY_POS_INF = 2139095040
_WI_LANE = IDX_DIM + GLA_GATE_RANK


def _unkey(kk):
    return lax.bitcast_convert_type(jnp.where(kk < 0, kk ^ 0x7FFFFFFF, kk), F32)


def _key(v):
    i = lax.bitcast_convert_type(v, I32)
    return jnp.where(i < 0, i ^ 0x7FFFFFFF, i)


_REDUCE_CHAINS = 8


def _reduce_rows(x, reduce_fn):
    r, q = x.shape
    groups = r // SUBLANES
    chains = _REDUCE_CHAINS if groups % _REDUCE_CHAINS == 0 else 1
    y = x.reshape(chains, groups // chains, SUBLANES, q)
    return reduce_fn(reduce_fn(y, axis=1), axis=0)


def _fold_keys(m, key_axis):
    if key_axis == 1:
        out = m[:, :LANES]
        for t in range(1, m.shape[1] // LANES):
            out = out + m[:, t * LANES:(t + 1) * LANES]
        return out
    return _reduce_rows(m, jnp.sum)


_VALUE_BISECTIONS = 12
_KEY_BITS = 32


def _kth_largest_threshold(read_block, n_blocks, rows, k, active, key_axis=1, bounds=None):
    kf = float(k)
    qshape = (rows, 1) if key_axis == 1 else (1, rows)
    part = (rows, LANES) if key_axis == 1 else (SUBLANES, rows)
    value_steps = _VALUE_BISECTIONS if bounds is not None else 0
    max_steps = value_steps + _KEY_BITS + 2

    def count(preds):
        def body(c, accs):
            s = read_block(c)
            return tuple(a + _fold_keys(jnp.where(p(s), 1.0, 0.0), key_axis) for a, p in zip(accs, preds))
        accs = lax.fori_loop(0, n_blocks, body, tuple(jnp.zeros(part, F32) for _ in preds))
        return [jnp.sum(a, axis=key_axis, keepdims=True) for a in accs]

    def cond(st):
        return st[5] > 0

    def body(st):
        it, lo, hi, cnt_lo, open_, _ = st
        mid = (lo >> 1) + (hi >> 1) + (lo & hi & 1)
        if value_steps:
            vmid = _key(0.5 * _unkey(lo) + 0.5 * _unkey(hi))
            vmid = jnp.minimum(jnp.maximum(vmid, lo + 1), jnp.maximum(hi - 1, lo + 1))
            mid = jnp.where(jnp.logical_and(it < value_steps, positive), vmid, mid)
        cand = _unkey(mid)
        cnt, = count([lambda s: s >= cand])
        ge = jnp.logical_and(open_ > 0, cnt >= kf)
        lt = jnp.logical_and(open_ > 0, cnt < kf)
        lo = jnp.where(ge, mid, lo)
        cnt_lo = jnp.where(ge, cnt, cnt_lo)
        hi = jnp.where(lt, mid, hi)
        settled = (cnt_lo == kf) | (hi == lo + 1)
        open_ = jnp.where(settled, 0, open_)
        go = jnp.where(it < max_steps, jnp.max(open_), 0)
        return it + 1, lo, hi, cnt_lo, open_, go

    if bounds is None:
        lo0 = jnp.full(qshape, _KEY_NEG_INF, I32)
        hi0 = jnp.full(qshape, _KEY_POS_INF, I32)
        cnt0 = jnp.full(qshape, -1.0, F32)
        open0 = active
    else:
        n_gt0, n_ge0 = count([lambda s: s > 0.0, lambda s: s >= 0.0])
        positive = n_gt0 >= kf
        non_negative = n_ge0 >= kf
        key_zero = _key(jnp.zeros(qshape, F32))
        lo0 = jnp.where(non_negative, key_zero, _key(bounds[0]))
        cnt0 = jnp.where(non_negative, n_ge0, -1.0)
        hi0 = jnp.where(positive, _key(bounds[1]) + 1, jnp.where(non_negative, key_zero + 1, key_zero))
        lo0 = jnp.where(active, lo0, _KEY_NEG_INF)
        hi0 = jnp.where(active, hi0, _KEY_POS_INF)
        open0 = jnp.logical_and(active, jnp.logical_not((cnt0 == kf) | (hi0 == lo0 + 1)))
    open0 = jnp.where(open0, 1, 0).astype(I32)
    _, lo, _, _, _, _ = lax.while_loop(cond, body, (jnp.int32(0), lo0, hi0, cnt0, open0, jnp.max(open0)))
    thr = jnp.where(active, _unkey(lo), -jnp.inf)
    n_gt, n_ge = count([lambda s: s > thr, lambda s: s >= thr])
    need = jnp.where(active, kf - n_gt, 0.0)
    ties = jnp.where(jnp.logical_and(active, (n_ge - n_gt) > need), 1, 0).astype(I32)
    return thr, need, ties


def _earlier_matrix(w, key_axis):
    r_i = lax.broadcasted_iota(I32, (w, w), 0)
    c_i = lax.broadcasted_iota(I32, (w, w), 1)
    return jnp.where(r_i < c_i if key_axis == 1 else c_i < r_i, 1.0, 0.0).astype(BF16)


def _selection_bias(s, thr, need, tie_flag, tie_cnt_ref, bias_ref, earlier_ref, key_axis=1):
    w = s.shape[key_axis]

    @pl.when(tie_flag == 0)
    def _():
        sel = jnp.logical_and(s >= thr, s > -jnp.inf)
        bias_ref[...] = jnp.where(sel, 0.0, NEG_BIG)

    @pl.when(tie_flag != 0)
    def _():
        step = min(w, earlier_ref.shape[0])
        earlier = earlier_ref[:step, :step]
        seen = tie_cnt_ref[...]
        for a in range(0, w, step):
            piece = (slice(None), slice(a, a + step)) if key_axis == 1 else (slice(a, a + step), slice(None))
            sp = s[piece]
            eq = sp == thr
            eqf = jnp.where(eq, 1.0, 0.0)
            if key_axis == 1:
                rank = jnp.dot(eqf.astype(BF16), earlier, preferred_element_type=F32)
            else:
                rank = jnp.dot(earlier, eqf.astype(BF16), preferred_element_type=F32)
            sel = jnp.logical_or(sp > thr, jnp.logical_and(eq, rank + seen < need))
            bias_ref[piece] = jnp.where(sel, 0.0, NEG_BIG)
            seen = seen + jnp.sum(eqf, axis=key_axis, keepdims=True)
        tie_cnt_ref[...] = seen


def _dsa_prompt_kernel(qb_ref, qi_ref, wi_ref, ki_ref, k_hbm, vt_hbm, o_ref,
                       k_scr, vt_scr, sc_scr, qm_scr, m_scr, l_scr, acc_scr, bias_scr, tie_scr, earlier_scr, sem,
                       *, q_blk, k_blk, seq, n_sel):
    b = pl.program_id(0)
    i = pl.program_id(1)

    @pl.when(i == 0)
    def _():
        ck = pltpu.make_async_copy(k_hbm.at[pl.ds(b * seq, seq)], k_scr, sem.at[0])
        cv = pltpu.make_async_copy(vt_hbm.at[:, pl.ds(b * seq, seq)], vt_scr, sem.at[1])
        ck.start()
        cv.start()
        ck.wait()
        cv.wait()

    q0 = i * q_blk
    n_blocks = (q0 + q_blk + k_blk - 1) // k_blk
    q_pos = q0 + lax.broadcasted_iota(I32, (1, q_blk), 1)

    qit = qi_ref[...].T.astype(BF16)
    qit_h = [qit[h * IDX_DIM:(h + 1) * IDX_DIM] for h in range(IDX_HEADS)]
    wit = wi_ref[...].T
    w_h = [wit[_WI_LANE + h:_WI_LANE + h + 1] for h in range(IDX_HEADS)]

    def score_body(c, carry):
        off = pl.multiple_of(c * k_blk, k_blk)
        kc = ki_ref[pl.ds(off, k_blk), :]
        s = w_h[0] * jnp.maximum(jnp.dot(kc, qit_h[0], preferred_element_type=F32), 0.0)
        for h in range(1, IDX_HEADS):
            s = s + w_h[h] * jnp.maximum(jnp.dot(kc, qit_h[h], preferred_element_type=F32), 0.0)
        k_pos = off + lax.broadcasted_iota(I32, (k_blk, 1), 0)
        adm = k_pos <= q_pos
        sc_scr[pl.ds(off, k_blk), :] = jnp.where(adm, s, -jnp.inf)
        hi_part, lo_part = carry
        hi_part = jnp.maximum(hi_part, _reduce_rows(jnp.where(adm, s, -jnp.inf), jnp.max))
        lo_part = jnp.minimum(lo_part, _reduce_rows(jnp.where(adm, s, jnp.inf), jnp.min))
        return hi_part, lo_part

    hi_part, lo_part = lax.fori_loop(
        0, n_blocks, score_body,
        (jnp.full((SUBLANES, q_blk), -jnp.inf, F32), jnp.full((SUBLANES, q_blk), jnp.inf, F32)))
    s_max = jnp.max(hi_part, axis=0, keepdims=True)
    s_min = jnp.min(lo_part, axis=0, keepdims=True)

    def read_block(c):
        return sc_scr[pl.ds(pl.multiple_of(c * k_blk, k_blk), k_blk), :]

    active = (q_pos + 1) > n_sel
    thr, need, ties = _kth_largest_threshold(read_block, n_blocks, q_blk, n_sel, active, key_axis=0,
                                             bounds=(s_min, s_max))
    tie_flag = jnp.max(ties)

    qt = (qb_ref[...] * (DSA_DH ** -0.5)).T
    row_lo = lax.broadcasted_iota(I32, (LANES, 1), 0) < DSA_DH
    for p in range(DSA_HEADS // 2):
        qp = qt[p * LANES:(p + 1) * LANES]
        qm_scr[2 * p] = jnp.where(row_lo, qp, 0.0).astype(BF16)
        qm_scr[2 * p + 1] = jnp.where(row_lo, 0.0, qp).astype(BF16)
    m_scr[...] = jnp.full(m_scr.shape, NEG_BIG, F32)
    l_scr[...] = jnp.zeros(l_scr.shape, F32)
    acc_scr[...] = jnp.zeros(acc_scr.shape, F32)
    tie_scr[...] = jnp.zeros(tie_scr.shape, F32)

    @pl.when(tie_flag != 0)
    def _():
        earlier_scr[...] = _earlier_matrix(k_blk, 0)

    def att_body(j, carry):
        off = pl.multiple_of(j * k_blk, k_blk)
        _selection_bias(sc_scr[pl.ds(off, k_blk), :], thr, need, tie_flag, tie_scr, bias_scr, earlier_scr,
                        key_axis=0)
        bias = bias_scr[...]

        def qk(h):
            p = h // 2
            kp = k_scr[pl.ds(off, k_blk), p * LANES:(p + 1) * LANES]
            return jnp.dot(kp, qm_scr[h], preferred_element_type=F32)

        prs, alphas = [], []
        for h in range(DSA_HEADS):
            logit = qk(h) + bias
            m_old = m_scr[h]
            m_new = jnp.maximum(m_old, jnp.max(_reduce_rows(logit, jnp.max), axis=0, keepdims=True))
            alpha = jnp.exp(m_old - m_new)
            pe = jnp.exp(logit - m_new)
            l_scr[h] = alpha * l_scr[h] + jnp.sum(_reduce_rows(pe, jnp.sum), axis=0, keepdims=True)
            m_scr[h] = m_new
            prs.append(pe.astype(BF16))
            alphas.append(alpha)
        for h in range(DSA_HEADS):
            rows = slice(h * DSA_DH, (h + 1) * DSA_DH)
            vth = vt_scr[rows, pl.ds(off, k_blk)]
            acc_scr[rows, :] = alphas[h] * acc_scr[rows, :] + jnp.dot(vth, prs[h], preferred_element_type=F32)
        return carry

    lax.fori_loop(0, n_blocks, att_body, 0)

    for h in range(DSA_HEADS):
        rows = slice(h * DSA_DH, (h + 1) * DSA_DH)
        acc_scr[rows, :] = acc_scr[rows, :] / l_scr[h]
    o_ref[...] = acc_scr[...].T


def _dsa_prompt(qb, qi, wi, ki16, k16, vt16, batch, seq, q_blk, k_blk):
    n_sel = min(DSA_TOPK, seq // 4)
    nq = seq // q_blk
    qspec = lambda w: pl.BlockSpec((q_blk, w), lambda b, i: (b * nq + i, 0))
    return pl.pallas_call(
        functools.partial(_dsa_prompt_kernel, q_blk=q_blk, k_blk=k_blk, seq=seq, n_sel=n_sel),
        grid=(batch, nq),
        in_specs=[qspec(DSA_W), qspec(IDX_W), qspec(LANES),
                  pl.BlockSpec((seq, IDX_DIM), lambda b, i: (b, 0)),
                  pl.BlockSpec(memory_space=pl.ANY), pl.BlockSpec(memory_space=pl.ANY)],
        out_specs=qspec(DSA_W),
        out_shape=jax.ShapeDtypeStruct((batch * seq, DSA_W), F32),
        scratch_shapes=[pltpu.VMEM((seq, DSA_W), BF16), pltpu.VMEM((DSA_W, seq), BF16),
                        pltpu.VMEM((seq, q_blk), F32),
                        pltpu.VMEM((DSA_HEADS, LANES, q_blk), BF16),
                        pltpu.VMEM((DSA_HEADS, 1, q_blk), F32), pltpu.VMEM((DSA_HEADS, 1, q_blk), F32),
                        pltpu.VMEM((DSA_W, q_blk), F32), pltpu.VMEM((k_blk, q_blk), F32),
                        pltpu.VMEM((1, q_blk), F32), pltpu.VMEM((k_blk, k_blk), BF16),
                        pltpu.SemaphoreType.DMA((2,))],
        compiler_params=_params(("arbitrary", "arbitrary")),
        name="dsa_prompt",
    )(qb, qi, wi, ki16, k16, vt16)


PAGES_PER_STEP = 32
ATTN_PARTS = 2
TIE_RANK_WIDTH = 1024
SCORE_PAGES_PER_STEP = 32
_STEP_KEYS = PAGES_PER_STEP * PAGE_SIZE
_NT = (((1,), (1,)), ((), ()))


def _page_specs(page_shape, n_pages, per_step, clamp_last):
    specs = []
    zeros = (0,) * len(page_shape)
    for s in range(per_step):
        def imap(b, c, pt, s=s):
            page = jnp.minimum(c * per_step + s, n_pages - per_step + s) if clamp_last else c * per_step + s
            return (pt[b, page],) + zeros
        specs.append(pl.BlockSpec((1,) + page_shape, imap))
    return specs


def _stack_heads(qi):
    return jnp.concatenate([qi[:, h * IDX_DIM:(h + 1) * IDX_DIM] for h in range(IDX_HEADS)], axis=0)


def _idx_score(s4, wi, t):
    s = wi[:, _WI_LANE:_WI_LANE + 1] * jnp.maximum(s4[:t], 0.0)
    for h in range(1, IDX_HEADS):
        s = s + wi[:, _WI_LANE + h:_WI_LANE + h + 1] * jnp.maximum(s4[h * t:(h + 1) * t], 0.0)
    return s


def _dsa_sample_score_kernel(pt_ref, qi_ref, wi_ref, kin_ref, *rest, t, past):
    pages = rest[:SCORE_PAGES_PER_STEP]
    sc_ref, = rest[SCORE_PAGES_PER_STEP:]
    c = pl.program_id(1)
    nch = pl.num_programs(1)
    step_keys = SCORE_PAGES_PER_STEP * PAGE_SIZE
    qi4 = _stack_heads(qi_ref[...]).astype(BF16)
    wi = wi_ref[...]
    keys_t = jnp.concatenate([p[0] for p in pages], axis=1).astype(BF16)
    off = pl.multiple_of(c * step_keys, step_keys)
    sc_ref[0, :, pl.ds(off, step_keys)] = _idx_score(
        jnp.dot(qi4, keys_t, preferred_element_type=F32), wi, t)

    @pl.when(c == nch - 1)
    def _():
        knew = jnp.concatenate([kin_ref[...], jnp.zeros((LANES - t, IDX_DIM), F32)], axis=0).astype(BF16)
        s = _idx_score(lax.dot_general(qi4, knew, _NT, preferred_element_type=F32), wi, t)
        col = lax.broadcasted_iota(I32, (t, LANES), 1)
        row = lax.broadcasted_iota(I32, (t, LANES), 0)
        sc_ref[0, :, past:past + LANES] = jnp.where(col <= row, s, -jnp.inf)


THRESHOLD_GROUP = 8


def _dsa_sample_threshold_kernel(sc_ref, thr_ref, need_ref, flag_ref, *, t, past, n_sel):
    g = sc_ref.shape[0]
    rows = g * t

    def scores(_):
        return sc_ref[...].reshape(rows, sc_ref.shape[2])

    s = scores(0)
    s_max = jnp.max(s, axis=1, keepdims=True)
    s_min = jnp.min(jnp.where(s > -jnp.inf, s, jnp.inf), axis=1, keepdims=True)
    q_in_seq = lax.broadcasted_iota(I32, (g, t, 1), 1).reshape(rows, 1)
    active = (past + 1 + q_in_seq) > n_sel
    thr, need, ties = _kth_largest_threshold(scores, 1, rows, n_sel, active, bounds=(s_min, s_max))
    thr_ref[...] = thr.reshape(g, t, 1)
    need_ref[...] = need.reshape(g, t, 1)
    flag_ref[...] = ties.astype(F32).reshape(g, t, 1)


def _dsa_sample_attn_kernel(pt_ref, qb_ref, sc_ref, thr_ref, need_ref, flag_ref, kn_ref, vn_ref, *rest,
                            t, past):
    kpages = rest[:PAGES_PER_STEP]
    vpages = rest[PAGES_PER_STEP:2 * PAGES_PER_STEP]
    o_ref, m_scr, l_scr, acc_scr, bias_scr, biasn_scr, tie_scr, earlier_scr = rest[2 * PAGES_PER_STEP:]
    c = pl.program_id(1)
    nch = pl.num_programs(1) - 1
    flag = jnp.max(flag_ref[0]).astype(I32)

    @pl.when(jnp.logical_and(c == 0, flag != 0))
    def _():
        earlier_scr[...] = _earlier_matrix(TIE_RANK_WIDTH, 1)

    @pl.when(c == 0)
    def _():
        m_scr[...] = jnp.full(m_scr.shape, NEG_BIG, F32)
        l_scr[...] = jnp.zeros(l_scr.shape, F32)
        acc_scr[...] = jnp.zeros(acc_scr.shape, F32)
        tie_scr[...] = jnp.zeros(tie_scr.shape, F32)

    rows = DSA_HEADS * t
    blockmask = (lax.broadcasted_iota(I32, (rows, DSA_W), 0) // t
                 == lax.broadcasted_iota(I32, (rows, DSA_W), 1) // DSA_DH)
    qs = qb_ref[...] * (DSA_DH ** -0.5)
    qbd = jnp.where(blockmask, jnp.concatenate([qs] * DSA_HEADS, axis=0), 0.0).astype(BF16)
    thr = thr_ref[0]
    need = need_ref[0]

    def probs(part, bias, k_op, keys_on_lanes):
        if keys_on_lanes:
            logit = jnp.dot(qbd, k_op, preferred_element_type=F32)
        else:
            logit = lax.dot_general(qbd, k_op, _NT, preferred_element_type=F32)
        logit = logit + jnp.concatenate([bias] * DSA_HEADS, axis=0)
        m_old = m_scr[part]
        m_new = jnp.maximum(m_old, jnp.max(logit, axis=-1, keepdims=True))
        alpha = jnp.exp(m_old - m_new)
        pr = jnp.exp(logit - m_new).astype(BF16)
        l_scr[part] = alpha * l_scr[part] + jnp.sum(pr.astype(F32), axis=-1, keepdims=True)
        m_scr[part] = m_new
        return pr, alpha

    def accumulate(part, pr, alpha, v_op, keys_on_lanes):
        if keys_on_lanes:
            pv = lax.dot_general(pr, v_op, _NT, preferred_element_type=F32)
        else:
            pv = jnp.dot(pr, v_op, preferred_element_type=F32)
        acc_scr[part] = alpha * acc_scr[part] + pv

    @pl.when(c < nch)
    def _():
        off = pl.multiple_of(c * _STEP_KEYS, _STEP_KEYS)
        _selection_bias(sc_ref[0, :, pl.ds(off, _STEP_KEYS)], thr, need, flag, tie_scr, bias_scr, earlier_scr)
        bias = bias_scr[...]
        per = PAGES_PER_STEP // ATTN_PARTS
        stats = []
        for part in range(ATTN_PARTS):
            kc = jnp.concatenate([p[0].reshape(DSA_W, PAGE_SIZE)
                                  for p in kpages[part * per:(part + 1) * per]], axis=1).astype(BF16)
            stats.append(probs(part, bias[:, part * per * PAGE_SIZE:(part + 1) * per * PAGE_SIZE], kc, True))
        for part in range(ATTN_PARTS):
            vc = jnp.concatenate([p[0].reshape(DSA_W, PAGE_SIZE)
                                  for p in vpages[part * per:(part + 1) * per]], axis=1).astype(BF16)
            accumulate(part, stats[part][0], stats[part][1], vc, True)

    @pl.when(c == nch)
    def _():
        _selection_bias(sc_ref[0, :, past:past + LANES], thr, need, flag, tie_scr, biasn_scr, earlier_scr)
        zpad = jnp.zeros((LANES - t, DSA_W), F32)
        kc = jnp.concatenate([kn_ref[...], zpad], axis=0).astype(BF16)
        vc = jnp.concatenate([vn_ref[...], zpad], axis=0).astype(BF16)
        pr, alpha = probs(0, biasn_scr[...], kc, False)
        accumulate(0, pr, alpha, vc, False)
        m_all = m_scr[0]
        for part in range(1, ATTN_PARTS):
            m_all = jnp.maximum(m_all, m_scr[part])
        l_all = jnp.zeros_like(m_all)
        acc_all = jnp.zeros(acc_scr.shape[1:], F32)
        for part in range(ATTN_PARTS):
            w_part = jnp.exp(m_scr[part] - m_all)
            l_all = l_all + w_part * l_scr[part]
            acc_all = acc_all + w_part * acc_scr[part]
        outn = jnp.where(blockmask, acc_all / l_all, 0.0)
        out = outn[:t]
        for h in range(1, DSA_HEADS):
            out = out + outn[h * t:(h + 1) * t]
        o_ref[...] = out


def _dsa_sample(qb, qi, wi, ki_new, k_new, v_new, cache_k, cache_v, cache_idx_k, page_table, batch, t):
    n_pages = page_table.shape[1]
    past = n_pages * PAGE_SIZE
    n_sel = min(DSA_TOPK, (past + t) // 4)
    nch = n_pages // PAGES_PER_STEP
    nch_score = n_pages // SCORE_PAGES_PER_STEP
    lp = past + LANES
    kv_page = (DSA_HEADS, DSA_DH, PAGE_SIZE)
    ck = jnp.transpose(cache_k, (0, 2, 3, 1))
    cv = jnp.transpose(cache_v, (0, 2, 3, 1))
    cik = jnp.transpose(cache_idx_k, (0, 2, 1))
    rspec = lambda w: pl.BlockSpec((t, w), lambda b, c, pt: (b, 0))
    bspec = lambda w: pl.BlockSpec((1, t, w), lambda b, c, pt: (b, 0, 0))
    scores = pl.pallas_call(
        functools.partial(_dsa_sample_score_kernel, t=t, past=past),
        grid_spec=pltpu.PrefetchScalarGridSpec(
            num_scalar_prefetch=1,
            grid=(batch, nch_score),
            in_specs=[rspec(IDX_W), rspec(LANES), rspec(IDX_DIM)]
            + _page_specs((IDX_DIM, PAGE_SIZE), n_pages, SCORE_PAGES_PER_STEP, False),
            out_specs=bspec(lp),
        ),
        out_shape=jax.ShapeDtypeStruct((batch, t, lp), F32),
        compiler_params=_params(("arbitrary", "arbitrary")),
        name="dsa_sample_score",
    )(page_table, qi, wi, ki_new, *([cik] * SCORE_PAGES_PER_STEP))
    grp = THRESHOLD_GROUP if batch % THRESHOLD_GROUP == 0 else 1
    gspec = lambda w: pl.BlockSpec((grp, t, w), lambda i: (i, 0, 0))
    thr, need, flag = pl.pallas_call(
        functools.partial(_dsa_sample_threshold_kernel, t=t, past=past, n_sel=n_sel),
        grid=(batch // grp,),
        in_specs=[gspec(lp)],
        out_specs=[gspec(1)] * 3,
        out_shape=[jax.ShapeDtypeStruct((batch, t, 1), F32)] * 3,
        compiler_params=_params(("parallel",)),
        name="dsa_sample_threshold",
    )(scores)
    return pl.pallas_call(
        functools.partial(_dsa_sample_attn_kernel, t=t, past=past),
        grid_spec=pltpu.PrefetchScalarGridSpec(
            num_scalar_prefetch=1,
            grid=(batch, nch + 1),
            in_specs=[rspec(DSA_W), bspec(lp), bspec(1), bspec(1), bspec(1), rspec(DSA_W), rspec(DSA_W)]
            + _page_specs(kv_page, n_pages, PAGES_PER_STEP, True)
            + _page_specs(kv_page, n_pages, PAGES_PER_STEP, True),
            out_specs=rspec(DSA_W),
            scratch_shapes=[pltpu.VMEM((ATTN_PARTS, DSA_HEADS * t, 1), F32),
                            pltpu.VMEM((ATTN_PARTS, DSA_HEADS * t, 1), F32),
                            pltpu.VMEM((ATTN_PARTS, DSA_HEADS * t, DSA_W), F32), pltpu.VMEM((t, _STEP_KEYS), F32),
                            pltpu.VMEM((t, LANES), F32), pltpu.VMEM((t, 1), F32),
                            pltpu.VMEM((TIE_RANK_WIDTH, TIE_RANK_WIDTH), BF16)],
        ),
        out_shape=jax.ShapeDtypeStruct((batch * t, DSA_W), F32),
        compiler_params=_params(("arbitrary", "arbitrary")),
        name="dsa_sample_attn",
    )(page_table, qb, scores, thr, need, flag, k_new, v_new,
      *([ck] * PAGES_PER_STEP), *([cv] * PAGES_PER_STEP))


MOE_ROWS = 256
ROUTE_TILE = 512
TOKEN_TILE = 256


def _split_bf16(a):
    hi = a.astype(BF16)
    lo = (a - hi.astype(F32)).astype(BF16)
    return hi, lo


def _router_kernel(x_ref, whi_ref, wlo_ref, b_ref, idx_ref, gate_ref, rank_ref, cnt_ref, carry_scr):
    i = pl.program_id(0)
    tm = x_ref.shape[0]

    @pl.when(i == 0)
    def _():
        carry_scr[...] = jnp.zeros(carry_scr.shape, F32)

    xhi, xlo = _split_bf16(x_ref[...])
    whi = whi_ref[...]
    logits = (jnp.dot(xhi, whi, preferred_element_type=F32)
              + jnp.dot(xlo, whi, preferred_element_type=F32)
              + jnp.dot(xhi, wlo_ref[...], preferred_element_type=F32)) + b_ref[...]
    lane = lax.broadcasted_iota(I32, (tm, N_EXPERTS), 1)
    slot = lax.broadcasted_iota(I32, (tm, LANES), 1)
    vals, idxs = [], []
    cur = logits
    for _ in range(TOP_K):
        m = jnp.max(cur, axis=-1, keepdims=True)
        ix = jnp.min(jnp.where(cur == m, lane, N_EXPERTS), axis=-1, keepdims=True)
        vals.append(m)
        idxs.append(ix)
        cur = jnp.where(lane == ix, -jnp.inf, cur)
    es = [jnp.exp(v - vals[0]) for v in vals]
    denom = es[0] + es[1] + es[2] + es[3]
    onehot = jnp.zeros((tm, N_EXPERTS), F32)
    for ix in idxs:
        onehot = onehot + jnp.where(lane == ix, 1.0, 0.0)
    earlier = (lax.broadcasted_iota(I32, (tm, tm), 1) < lax.broadcasted_iota(I32, (tm, tm), 0))
    excl = jnp.dot(jnp.where(earlier, 1.0, 0.0).astype(BF16), onehot.astype(BF16),
                   preferred_element_type=F32) + carry_scr[...]
    idx_out = jnp.zeros((tm, LANES), I32)
    gate_out = jnp.zeros((tm, LANES), F32)
    rank_out = jnp.zeros((tm, LANES), F32)
    for k in range(TOP_K):
        rk = jnp.sum(jnp.where(lane == idxs[k], excl, 0.0), axis=-1, keepdims=True)
        idx_out = jnp.where(slot == k, idxs[k], idx_out)
        gate_out = jnp.where(slot == k, es[k] / denom, gate_out)
        rank_out = jnp.where(slot == k, rk, rank_out)
    idx_ref[...] = idx_out
    gate_ref[...] = gate_out
    rank_ref[...] = rank_out.astype(I32)
    total = carry_scr[...] + jnp.sum(onehot, axis=0, keepdims=True)
    carry_scr[...] = total
    cnt_ref[...] = total


def _router(x, w_router, b_router):
    n = x.shape[0]
    tm = ROUTE_TILE if n % ROUTE_TILE == 0 else TOKEN_TILE
    whi, wlo = _split_bf16(w_router)
    b2 = b_router.reshape(1, N_EXPERTS)
    kspec = _row_spec(tm, LANES)
    return pl.pallas_call(
        _router_kernel,
        grid=(n // tm,),
        in_specs=[_row_spec(tm, D_MODEL), _full_spec(whi), _full_spec(wlo), _full_spec(b2)],
        out_specs=[kspec, kspec, kspec, pl.BlockSpec((1, N_EXPERTS), lambda i: (0, 0))],
        out_shape=[jax.ShapeDtypeStruct((n, LANES), I32), jax.ShapeDtypeStruct((n, LANES), F32),
                   jax.ShapeDtypeStruct((n, LANES), I32), jax.ShapeDtypeStruct((1, N_EXPERTS), F32)],
        scratch_shapes=[pltpu.VMEM((1, N_EXPERTS), F32)],
        compiler_params=_params(("arbitrary",)),
        name="router",
    )(x, whi, wlo, b2)


def _row_copy(src, src_row, dst, dst_row, sem):
    return pltpu.make_async_copy(src.at[pl.ds(src_row, 1)], dst.at[pl.ds(dst_row, 1)], sem)


def _dispatch_kernel(dest_ref, x_ref, xs_in, xs_out, sem):
    del xs_in
    tm = x_ref.shape[0]

    def issue(r, c):
        for k in range(TOP_K):
            _row_copy(x_ref, r, xs_out, dest_ref[r * TOP_K + k], sem).start(priority=k % 2)
        return c

    lax.fori_loop(0, tm, issue, 0)
    for _ in range(TOP_K):
        pltpu.make_async_copy(x_ref, x_ref, sem).wait()


def _dispatch(x, dest_flat, n_rows):
    n = x.shape[0]
    tm = TOKEN_TILE
    return pl.pallas_call(
        _dispatch_kernel,
        grid=(n // tm,),
        in_specs=[pl.BlockSpec((tm * TOP_K,), lambda i: (i,), memory_space=pltpu.SMEM),
                  _row_spec(tm, D_MODEL), pl.BlockSpec(memory_space=pl.ANY)],
        out_specs=pl.BlockSpec(memory_space=pl.ANY),
        out_shape=jax.ShapeDtypeStruct((n_rows, D_MODEL), F32),
        scratch_shapes=[pltpu.SemaphoreType.DMA(())],
        input_output_aliases={2: 0},
        compiler_params=_params(("arbitrary",)),
        name="moe_dispatch",
    )(dest_flat, x, jnp.zeros((n_rows, D_MODEL), F32))


def _expert_kernel(be_ref, na_ref, x_ref, w1_ref, b1_ref, w2_ref, b2_ref, o_ref, w1b_scr, w2b_scr):
    i = pl.program_id(0)
    active = i < na_ref[0]
    changed = jnp.logical_or(i == 0, be_ref[i] != be_ref[jnp.maximum(i - 1, 0)])

    @pl.when(jnp.logical_and(active, changed))
    def _():
        w1b_scr[...] = w1_ref[0].astype(BF16)
        w2b_scr[...] = w2_ref[0].astype(BF16)

    @pl.when(active)
    def _():
        h = jnp.dot(x_ref[...].astype(BF16), w1b_scr[...], preferred_element_type=F32) + b1_ref[0]
        gt = jnp.minimum(h[:, :D_FF], SWIGLU_LIMIT)
        up = jnp.clip(h[:, D_FF:], -SWIGLU_LIMIT, SWIGLU_LIMIT)
        act = (up + 1.0) * gt * jax.nn.sigmoid(SWIGLU_ALPHA * gt)
        o_ref[...] = jnp.dot(act.astype(BF16), w2b_scr[...], preferred_element_type=F32) + b2_ref[0]

    @pl.when(jnp.logical_not(active))
    def _():
        o_ref[...] = jnp.zeros(o_ref.shape, F32)


def _experts(xs, block_e, n_active, w1, b1, w2, b2):
    n_rows = xs.shape[0]
    nblk = n_rows // MOE_ROWS
    w1 = w1.reshape(-1, D_MODEL, 2 * D_FF)
    w2 = w2.reshape(-1, D_FF, D_MODEL)
    b1r = b1.reshape(-1, 1, 2 * D_FF)
    b2r = b2.reshape(-1, 1, D_MODEL)
    last = lambda i, na: jnp.minimum(i, na[0] - 1)
    return pl.pallas_call(
        _expert_kernel,
        grid_spec=pltpu.PrefetchScalarGridSpec(
            num_scalar_prefetch=2,
            grid=(nblk,),
            in_specs=[pl.BlockSpec((MOE_ROWS, D_MODEL), lambda i, be, na: (last(i, na), 0)),
                      pl.BlockSpec((1, D_MODEL, 2 * D_FF), lambda i, be, na: (be[i], 0, 0)),
                      pl.BlockSpec((1, 1, 2 * D_FF), lambda i, be, na: (be[i], 0, 0)),
                      pl.BlockSpec((1, D_FF, D_MODEL), lambda i, be, na: (be[i], 0, 0)),
                      pl.BlockSpec((1, 1, D_MODEL), lambda i, be, na: (be[i], 0, 0))],
            out_specs=pl.BlockSpec((MOE_ROWS, D_MODEL), lambda i, be, na: (i, 0)),
            scratch_shapes=[pltpu.VMEM((D_MODEL, 2 * D_FF), BF16), pltpu.VMEM((D_FF, D_MODEL), BF16)],
        ),
        out_shape=jax.ShapeDtypeStruct((n_rows, D_MODEL), F32),
        compiler_params=_params(("arbitrary",)),
        name="moe_experts",
    )(block_e, n_active, xs, w1, b1r, w2, b2r)


def _combine_kernel(dest_ref, x_ref, gate_ref, g_ref, b_ref, yb_hbm, *rest, first_tiles):
    outs, (buf, sem) = rest[:-2], rest[-2:]
    tm = x_ref.shape[0]

    def issue(r, c):
        for k in range(TOP_K):
            _row_copy(yb_hbm, dest_ref[r * TOP_K + k], buf.at[k], r, sem).start(priority=k % 2)
        return c

    lax.fori_loop(0, tm, issue, 0)
    for k in range(TOP_K):
        pltpu.make_async_copy(buf.at[k], buf.at[k], sem).wait()
    gate = gate_ref[...]
    y = gate[:, 0:1] * buf[0]
    for k in range(1, TOP_K):
        y = y + gate[:, k:k + 1] * buf[k]
    res = _layer_norm(DN_ALPHA * x_ref[...] + y, g_ref[...], b_ref[...])
    if first_tiles is None:
        outs[0][...] = res
    else:
        i = pl.program_id(0)

        @pl.when(i < first_tiles)
        def _():
            outs[0][...] = res

        @pl.when(i >= first_tiles)
        def _():
            outs[1][...] = res


def _combine(x, yb, dest_flat, gate, g, b, split=None):
    n = x.shape[0]
    tm = TOKEN_TILE
    g2, b2 = g.reshape(1, D_MODEL), b.reshape(1, D_MODEL)
    if split is None:
        first_tiles = None
        out_specs = _row_spec(tm, D_MODEL)
        out_shape = jax.ShapeDtypeStruct((n, D_MODEL), F32)
    else:
        first_tiles = split // tm
        out_specs = [pl.BlockSpec((tm, D_MODEL), lambda i: (jnp.minimum(i, first_tiles - 1), 0)),
                     pl.BlockSpec((tm, D_MODEL), lambda i: (jnp.maximum(i - first_tiles, 0), 0))]
        out_shape = [jax.ShapeDtypeStruct((split, D_MODEL), F32), jax.ShapeDtypeStruct((n - split, D_MODEL), F32)]
    return pl.pallas_call(
        functools.partial(_combine_kernel, first_tiles=first_tiles),
        grid=(n // tm,),
        in_specs=[pl.BlockSpec((tm * TOP_K,), lambda i: (i,), memory_space=pltpu.SMEM),
                  _row_spec(tm, D_MODEL), _row_spec(tm, LANES),
                  _full_spec(g2), _full_spec(b2), pl.BlockSpec(memory_space=pl.ANY)],
        out_specs=out_specs,
        out_shape=out_shape,
        scratch_shapes=[pltpu.VMEM((TOP_K, tm, D_MODEL), F32), pltpu.SemaphoreType.DMA(())],
        compiler_params=_params(("arbitrary",)),
        name="moe_combine",
    )(dest_flat, x, gate, g2, b2, yb)


def _moe_layer(x, layer, w_router, b_router, w1, b1, w2, b2, g, b, split=None):
    n = x.shape[0]
    idx, gate, rank, counts = _router(x, w_router, b_router)
    nblk = (n * TOP_K) // MOE_ROWS + N_EXPERTS
    cnt = counts[0].astype(I32)
    padded = (cnt + MOE_ROWS - 1) // MOE_ROWS * MOE_ROWS
    pad_end = jnp.cumsum(padded)
    pad_start = pad_end - padded
    dest = (pad_start[idx] + rank)[:, :TOP_K].reshape(-1)
    n_active = (pad_end[-1] // MOE_ROWS).astype(I32)
    blk = jnp.arange(nblk, dtype=I32)
    blk = jnp.minimum(blk, n_active - 1)
    block_e = jnp.sum((pad_end[None, :] <= (blk * MOE_ROWS)[:, None]).astype(I32), axis=1)
    block_e = jnp.minimum(block_e, N_EXPERTS - 1) + layer * N_EXPERTS
    xs = _dispatch(x, dest, nblk * MOE_ROWS)
    yb = _experts(xs, block_e, n_active.reshape(1), w1, b1, w2, b2)
    return _combine(x, yb, dest, gate, g, b, split)


POOL_HALO = 16


def _pool_kernel(x_ref, halo_ref, w_ref, sc_ref, g_ref, b_ref, o_ref, ext_scr, *, tiles_per_seq, n_prev):
    tm = x_ref.shape[0]
    tile = pl.program_id(0) % tiles_per_seq
    x = x_ref[...]
    halo = halo_ref[...]
    if n_prev == 0:
        halo = jnp.where(tile == 0, 0.0, halo)
    ext_scr[0:POOL_HALO] = halo
    ext_scr[POOL_HALO:POOL_HALO + tm] = x
    pos = n_prev + tile * tm + lax.broadcasted_iota(I32, (tm, 1), 0)
    parts = []
    for g, w in enumerate(POOL_WINDOWS):
        cols = slice(g * POOL_GROUP, (g + 1) * POOL_GROUP)
        xg = x[:, cols]
        s = xg
        for d in range(1, w):
            s = s + ext_scr[POOL_HALO - d:POOL_HALO - d + tm, cols]
        cnt = jnp.minimum(pos + 1, w).astype(F32)
        dg = s / cnt - xg
        parts.append(jnp.dot(dg.astype(BF16), w_ref[g], preferred_element_type=F32))
    mix = jnp.concatenate(parts, axis=1) * sc_ref[...]
    o_ref[...] = _layer_norm(DN_ALPHA * x + mix, g_ref[...], b_ref[...])


def _pool(x, halo_src, pool_w, pool_scale, g, b, tm, tiles_per_seq, n_prev, halo_map, n, row_off=0):
    w16 = pool_w.astype(BF16)
    sc = pool_scale.reshape(1, D_MODEL)
    g2, b2 = g.reshape(1, D_MODEL), b.reshape(1, D_MODEL)
    blk_off = row_off // tm
    return pl.pallas_call(
        functools.partial(_pool_kernel, tiles_per_seq=tiles_per_seq, n_prev=n_prev),
        grid=(n // tm,),
        in_specs=[pl.BlockSpec((tm, D_MODEL), lambda i: (i + blk_off, 0)),
                  pl.BlockSpec((POOL_HALO, D_MODEL), halo_map),
                  _full_spec(w16), _full_spec(sc), _full_spec(g2), _full_spec(b2)],
        out_specs=_row_spec(tm, D_MODEL),
        out_shape=jax.ShapeDtypeStruct((n, D_MODEL), F32),
        scratch_shapes=[pltpu.VMEM((POOL_HALO + tm, D_MODEL), F32)],
        compiler_params=_params(("parallel",)),
        name="pool",
    )(x, halo_src, w16, sc, g2, b2)


PROJ_TILE = 512
DSA_Q_BLOCK = 512
DSA_K_BLOCK = 512
POOL_TILE = 512


def kernel(x_prompt, x_sample, cache_k, cache_v, cache_idx_k, page_table, state_gla, state_pool,
           w_in, gla_fg_w2, gla_fg_b, gla_norm_g, idx_kn_g, idx_kn_b, w_out,
           pool_w, pool_scale, ln_mix_g, ln_mix_b, ln_ffn_g, ln_ffn_b,
           moe_router_w, moe_router_b, moe_w1, moe_b1, moe_w2, moe_b2):
    bp, sp, _ = x_prompt.shape
    bs, ts, _ = x_sample.shape
    n_p, n_s = bp * sp, bs * ts
    xp = x_prompt.reshape(n_p, D_MODEL)
    xs = x_sample.reshape(n_s, D_MODEL)
    proj_w = (w_in, gla_fg_w2, gla_fg_b, idx_kn_g, idx_kn_b)

    (qa, ka, va, ga, lf, qb, _, _, qi, _, wi, kb16, vt16, ki16,
     kt, vt, kit) = _project(xp, *proj_w, tm=PROJ_TILE, seq=sp)
    oa_p, gla_p = _gla(qa, ka, va, lf, ga, jnp.zeros((bp, GLA_HEADS, GLA_DK, GLA_DV), F32), gla_norm_g, bp, sp)
    ob_p = _dsa_prompt(qb, qi, wi, ki16, kb16, vt16, bp, sp, DSA_Q_BLOCK, DSA_K_BLOCK)
    x1p = _merge(xp, oa_p, ob_p, w_out, ln_mix_g[0], ln_mix_b[0], PROJ_TILE)
    k_p = jnp.transpose(kt.reshape(bp, DSA_HEADS, DSA_DH, sp), (0, 3, 1, 2))
    v_p = jnp.transpose(vt.reshape(bp, DSA_HEADS, DSA_DH, sp), (0, 3, 1, 2))
    kidx_p = jnp.transpose(kit, (0, 2, 1))

    qa, ka, va, ga, lf, qb, kb, vb, qi, ki, wi, _, _, _ = _project(xs, *proj_w, tm=n_s)
    oa_s, gla_s = _gla(qa, ka, va, lf, ga, state_gla, gla_norm_g, bs, ts)
    ob_s = _dsa_sample(qb, qi, wi, ki, kb, vb, cache_k, cache_v, cache_idx_k, page_table, bs, ts)
    x1s = _merge(xs, oa_s, ob_s, w_out, ln_mix_g[0], ln_mix_b[0], n_s)
    k_s = kb.reshape(bs, ts, DSA_HEADS, DSA_DH)
    v_s = vb.reshape(bs, ts, DSA_HEADS, DSA_DH)
    kidx_s = ki.reshape(bs, ts, IDX_DIM)

    x_all = _moe_layer(jnp.concatenate([x1p, x1s], axis=0), 0, moe_router_w[0], moe_router_b[0],
                       moe_w1, moe_b1, moe_w2, moe_b2, ln_ffn_g[0], ln_ffn_b[0])

    pool_p = jnp.stack([x_all[(s + 1) * sp - POOL_STATE:(s + 1) * sp] for s in range(bp)])
    xs3 = x_all[n_p:].reshape(bs, ts, D_MODEL)
    pool_s = jnp.concatenate([state_pool, xs3], axis=1)[:, -POOL_STATE:]
    per_seq = sp // POOL_TILE
    halo_step = POOL_TILE // POOL_HALO
    x2p = _pool(x_all, x_all, pool_w, pool_scale, ln_mix_g[1], ln_mix_b[1], POOL_TILE, per_seq, 0,
                lambda i: (jnp.maximum(i * halo_step - 1, 0), 0), n_p)
    halo_s = jnp.concatenate([jnp.zeros((bs, POOL_HALO - POOL_STATE, D_MODEL), F32), state_pool], axis=1)
    x2s = _pool(x_all, halo_s.reshape(bs * POOL_HALO, D_MODEL), pool_w, pool_scale, ln_mix_g[1], ln_mix_b[1],
                ts, 1, POOL_STATE, lambda i: (i, 0), n_s, row_off=n_p)
    y_p, y_s = _moe_layer(jnp.concatenate([x2p, x2s], axis=0), 1, moe_router_w[1], moe_router_b[1],
                          moe_w1, moe_b1, moe_w2, moe_b2, ln_ffn_g[1], ln_ffn_b[1], split=n_p)
    y_p = y_p.reshape(bp, sp, D_MODEL)
    y_s = y_s.reshape(bs, ts, D_MODEL)
    return (y_p, y_s, k_p, v_p, kidx_p, gla_p, pool_p, k_s, v_s, kidx_s, gla_s, pool_s)
```

```python
import functools
import math

import jax
import jax.numpy as jnp
from jax import lax
from jax.experimental import pallas as pl
from jax.experimental.pallas import tpu as pltpu

F32 = jnp.float32
BF16 = jnp.bfloat16
I32 = jnp.int32

D_MODEL = 1024
DEPTH = 2
PAGE_SIZE = 128
GLA_HEADS = 4
GLA_DK = 64
GLA_DV = 128
GLA_GATE_RANK = 16
GLA_TAU = 16.0
GLA_CHUNK = 64
GLA_TILE = 2 * GLA_CHUNK
DSA_HEADS = 8
DSA_DH = 64
IDX_HEADS = 4
IDX_DIM = 64
DSA_TOPK = 256
IDX_W_SCALE = (IDX_HEADS ** -0.5) * (IDX_DIM ** -0.5)
POOL_WINDOWS = (2, 4, 8, 16)
POOL_GROUP = D_MODEL // 4
POOL_STATE = 16 - 1
N_EXPERTS = 32
TOP_K = 4
D_FF = D_MODEL
SWIGLU_ALPHA = 1.702
SWIGLU_LIMIT = 7.0
DN_ALPHA = (2 * DEPTH) ** 0.25
LN_EPS = 1e-5
GLA_QK = GLA_HEADS * GLA_DK
GLA_V = GLA_HEADS * GLA_DV
DSA_W = DSA_HEADS * DSA_DH
IDX_W = IDX_HEADS * IDX_DIM

LANES = 128
SUBLANES = 8
VMEM_LIMIT_BYTES = 56 * 1024 * 1024

NEG_BIG = -1e30


def _params(sem, vmem=VMEM_LIMIT_BYTES):
    return pltpu.CompilerParams(dimension_semantics=sem, vmem_limit_bytes=vmem)


def _layer_norm(x, g, b):
    mu = jnp.mean(x, axis=-1, keepdims=True)
    xc = x - mu
    var = jnp.mean(xc * xc, axis=-1, keepdims=True)
    return xc * lax.rsqrt(var + LN_EPS) * g + b


def _row_spec(tm, w):
    return pl.BlockSpec((tm, w), lambda i: (i, 0))


def _full_spec(a):
    nd = a.ndim
    return pl.BlockSpec(a.shape, lambda i: (0,) * nd)


_MAIN_W = (GLA_QK, GLA_QK, GLA_V, GLA_V, DSA_W, DSA_W, DSA_W, IDX_W)
_MAIN_OFF = tuple(sum(_MAIN_W[:i]) for i in range(len(_MAIN_W) + 1))


def _proj_kernel(x_ref, wm_ref, ws_ref, fgw_ref, fgb_ref, kng_ref, knb_ref,
                 qa_ref, ka_ref, va_ref, ga_ref, lf_ref, qb_ref, kb_ref, vb_ref,
                 qi_ref, ki_ref, wi_ref, kb16_ref, vt16_ref, ki16_ref, *channel_major):
    xb = x_ref[...].astype(BF16)

    def mm(n):
        return jnp.dot(xb, wm_ref[:, _MAIN_OFF[n]:_MAIN_OFF[n + 1]], preferred_element_type=F32)

    qa_ref[...] = mm(0) * (GLA_DK ** -0.5)
    ka_ref[...] = mm(1)
    va_ref[...] = mm(2)
    ga_ref[...] = mm(3)
    qb_ref[...] = mm(4)
    kb = mm(5)
    kb_ref[...] = kb
    kb16_ref[...] = kb.astype(BF16)
    vb = mm(6)
    vb_ref[...] = vb
    vbt = vb.T
    vt16_ref[...] = vbt.astype(BF16)
    qi_ref[...] = mm(7)

    small = jnp.dot(xb, ws_ref[...], preferred_element_type=F32)
    ki = _layer_norm(small[:, :IDX_DIM], kng_ref[...], knb_ref[...])
    ki_ref[...] = ki
    ki16_ref[...] = ki.astype(BF16)
    if channel_major:
        kt_ref, vt_ref, kit_ref = channel_major
        kt_ref[0] = kb.T
        vt_ref[0] = vbt
        kit_ref[0] = jnp.concatenate([ki, jnp.zeros_like(ki)], axis=1).T[:IDX_DIM]
    fa = small[:, IDX_DIM:IDX_DIM + GLA_GATE_RANK]
    z = jnp.dot(fa, fgw_ref[...], preferred_element_type=F32,
                precision=lax.Precision.HIGHEST) + fgb_ref[...]
    lf_ref[...] = (jnp.minimum(z, 0.0) - jnp.log(1.0 + jnp.exp(-jnp.abs(z)))) * (1.0 / GLA_TAU)
    wi_ref[...] = small * IDX_W_SCALE


def _project(x2d, w_in, gla_fg_w2, gla_fg_b, idx_kn_g, idx_kn_b, tm, seq=None):
    n = x2d.shape[0]
    pts = [0]
    for s in (GLA_QK, GLA_QK, GLA_V, GLA_V, GLA_GATE_RANK, DSA_W, DSA_W, DSA_W, IDX_W, IDX_DIM, IDX_HEADS):
        pts.append(pts[-1] + s)
    seg = lambda i: w_in[:, pts[i]:pts[i + 1]]
    wm = jnp.concatenate([seg(0), seg(1), seg(2), seg(3), seg(5), seg(6), seg(7), seg(8)], axis=1).astype(BF16)
    pad = LANES - IDX_DIM - GLA_GATE_RANK - IDX_HEADS
    ws = jnp.concatenate([seg(9), seg(4), seg(10), jnp.zeros((D_MODEL, pad), w_in.dtype)], axis=1).astype(BF16)
    fgb = gla_fg_b.reshape(1, GLA_QK)
    kng = idx_kn_g.reshape(1, IDX_DIM)
    knb = idx_kn_b.reshape(1, IDX_DIM)
    widths = (GLA_QK, GLA_QK, GLA_V, GLA_V, GLA_QK, DSA_W, DSA_W, DSA_W, IDX_W, IDX_DIM, LANES)
    out_shape = [jax.ShapeDtypeStruct((n, w), F32) for w in widths]
    out_shape += [jax.ShapeDtypeStruct((n, DSA_W), BF16), jax.ShapeDtypeStruct((DSA_W, n), BF16),
                  jax.ShapeDtypeStruct((n, IDX_DIM), BF16)]
    out_specs = [_row_spec(tm, w) for w in widths]
    out_specs += [_row_spec(tm, DSA_W), pl.BlockSpec((DSA_W, tm), lambda i: (0, i)), _row_spec(tm, IDX_DIM)]
    if seq is not None:
        tps = seq // tm
        for w in (DSA_W, DSA_W, IDX_DIM):
            out_shape.append(jax.ShapeDtypeStruct((n // seq, w, seq), F32))
            out_specs.append(pl.BlockSpec((1, w, tm), lambda i: (i // tps, 0, i % tps)))
    return pl.pallas_call(
        _proj_kernel,
        grid=(n // tm,),
        in_specs=[_row_spec(tm, D_MODEL), _full_spec(wm), _full_spec(ws), _full_spec(gla_fg_w2),
                  _full_spec(fgb), _full_spec(kng), _full_spec(knb)],
        out_specs=out_specs,
        out_shape=out_shape,
        compiler_params=_params(("parallel",)),
        name="proj",
    )(x2d, wm, ws, gla_fg_w2, fgb, kng, knb)


def _gla_kernel(q_ref, k_ref, v_ref, lf_ref, ga_ref, s0_ref, ng_ref, o_ref, sout_ref, s_scr, *, chunk):
    c = pl.program_id(1)
    nc = pl.num_programs(1)

    @pl.when(c == 0)
    def _():
        s_scr[...] = s0_ref[0]

    g = lf_ref[...]
    ri = lax.broadcasted_iota(I32, (chunk, chunk), 0)
    ci = lax.broadcasted_iota(I32, (chunk, chunk), 1)
    causal = ci <= ri
    tri = jnp.where(causal, 1.0, 0.0).astype(F32)
    b = jnp.dot(tri, g, preferred_element_type=F32, precision=lax.Precision.HIGHEST)
    mid = chunk // 2
    b_mid = b[mid:mid + 1, :]
    b_last = b[chunk - 1:chunk, :]
    q = q_ref[...]
    k = k_ref[...]
    q_in = q * jnp.exp(b)
    q_rel = q * jnp.exp(b - b_mid)
    k_rel = k * jnp.exp(b_mid - b)
    k_out = k * jnp.exp(b_last - b)
    dec_last = jnp.exp(b_last)
    v = v_ref[...]
    ga = ga_ref[...]
    ng = ng_ref[...]
    eye = jnp.where(lax.broadcasted_iota(I32, (GLA_DK, GLA_DK), 0)
                    == lax.broadcasted_iota(I32, (GLA_DK, GLA_DK), 1), 1.0, 0.0).astype(F32)
    for h in range(GLA_HEADS):
        ks = slice(h * GLA_DK, (h + 1) * GLA_DK)
        vs = slice(h * GLA_DV, (h + 1) * GLA_DV)
        s_h = s_scr[h]
        vh = v[:, vs]
        inter = jnp.dot(q_in[:, ks], s_h, preferred_element_type=F32)
        att = lax.dot_general(q_rel[:, ks], k_rel[:, ks], (((1,), (1,)), ((), ())),
                              preferred_element_type=F32)
        att = jnp.where(causal, att, 0.0)
        o = inter + jnp.dot(att, vh, preferred_element_type=F32)
        kv = lax.dot_general(k_out[:, ks], vh, (((0,), (0,)), ((), ())), preferred_element_type=F32)
        s_scr[h] = jnp.dot(eye * dec_last[:, ks], s_h, preferred_element_type=F32,
                           precision=lax.Precision.HIGHEST) + kv
        ms = jnp.mean(o * o, axis=-1, keepdims=True)
        gh = ga[:, vs]
        o_ref[:, vs] = o * lax.rsqrt(ms + LN_EPS) * ng * (gh * jax.nn.sigmoid(gh))

    @pl.when(c == nc - 1)
    def _():
        sout_ref[0] = s_scr[...]


def _gla(qa, ka, va, lf, ga, s0, gla_norm_g, batch, seq):
    chunk = math.gcd(seq, GLA_TILE)
    nc = seq // chunk
    ng = gla_norm_g.reshape(1, GLA_DV)
    spec = lambda w: pl.BlockSpec((chunk, w), lambda b, c: (b * nc + c, 0))
    sspec = pl.BlockSpec((1, GLA_HEADS, GLA_DK, GLA_DV), lambda b, c: (b, 0, 0, 0))
    return pl.pallas_call(
        functools.partial(_gla_kernel, chunk=chunk),
        grid=(batch, nc),
        in_specs=[spec(GLA_QK), spec(GLA_QK), spec(GLA_V), spec(GLA_QK), spec(GLA_V), sspec,
                  pl.BlockSpec((1, GLA_DV), lambda b, c: (0, 0))],
        out_specs=[spec(GLA_V), sspec],
        out_shape=[jax.ShapeDtypeStruct((batch * seq, GLA_V), F32),
                   jax.ShapeDtypeStruct((batch, GLA_HEADS, GLA_DK, GLA_DV), F32)],
        scratch_shapes=[pltpu.VMEM((GLA_HEADS, GLA_DK, GLA_DV), F32)],
        compiler_params=_params(("parallel", "arbitrary")),
        name="gla",
    )(qa, ka, va, lf, ga, s0, ng)


def _merge_kernel(x_ref, oa_ref, ob_ref, w_ref, g_ref, b_ref, o_ref):
    mix = jnp.dot(oa_ref[...].astype(BF16), w_ref[:GLA_V, :], preferred_element_type=F32)
    mix = mix + jnp.dot(ob_ref[...].astype(BF16), w_ref[GLA_V:, :], preferred_element_type=F32)
    o_ref[...] = _layer_norm(DN_ALPHA * x_ref[...] + mix, g_ref[...], b_ref[...])


def _merge(x2d, oa, ob, w_out, g, b, tm):
    n = x2d.shape[0]
    w16 = w_out.astype(BF16)
    g2, b2 = g.reshape(1, D_MODEL), b.reshape(1, D_MODEL)
    return pl.pallas_call(
        _merge_kernel,
        grid=(n // tm,),
        in_specs=[_row_spec(tm, D_MODEL), _row_spec(tm, GLA_V), _row_spec(tm, DSA_W),
                  _full_spec(w16), _full_spec(g2), _full_spec(b2)],
        out_specs=_row_spec(tm, D_MODEL),
        out_shape=jax.ShapeDtypeStruct((n, D_MODEL), F32),
        compiler_params=_params(("parallel",)),
        name="merge",
    )(x2d, oa, ob, w16, g2, b2)


_KEY_NEG_INF = -2139095041
_KEY_POS_INF = 2139095040
_WI_LANE = IDX_DIM + GLA_GATE_RANK


def _unkey(kk):
    return lax.bitcast_convert_type(jnp.where(kk < 0, kk ^ 0x7FFFFFFF, kk), F32)


def _key(v):
    i = lax.bitcast_convert_type(v, I32)
    return jnp.where(i < 0, i ^ 0x7FFFFFFF, i)


_REDUCE_CHAINS = 8


def _reduce_rows(x, reduce_fn):
    r, q = x.shape
    groups = r // SUBLANES
    chains = _REDUCE_CHAINS if groups % _REDUCE_CHAINS == 0 else 1
    y = x.reshape(chains, groups // chains, SUBLANES, q)
    return reduce_fn(reduce_fn(y, axis=1), axis=0)


def _fold_keys(m, key_axis):
    if key_axis == 1:
        out = m[:, :LANES]
        for t in range(1, m.shape[1] // LANES):
            out = out + m[:, t * LANES:(t + 1) * LANES]
        return out
    return _reduce_rows(m, jnp.sum)


_VALUE_BISECTIONS = 12
_KEY_BITS = 32


def _kth_largest_threshold(read_block, n_blocks, rows, k, active, key_axis=1, bounds=None):
    kf = float(k)
    qshape = (rows, 1) if key_axis == 1 else (1, rows)
    part = (rows, LANES) if key_axis == 1 else (SUBLANES, rows)
    value_steps = _VALUE_BISECTIONS if bounds is not None else 0
    max_steps = value_steps + _KEY_BITS + 2

    def count(preds):
        def body(c, accs):
            s = read_block(c)
            return tuple(a + _fold_keys(jnp.where(p(s), 1.0, 0.0), key_axis) for a, p in zip(accs, preds))
        accs = lax.fori_loop(0, n_blocks, body, tuple(jnp.zeros(part, F32) for _ in preds))
        return [jnp.sum(a, axis=key_axis, keepdims=True) for a in accs]

    def cond(st):
        return st[5] > 0

    def body(st):
        it, lo, hi, cnt_lo, open_, _ = st
        mid = (lo >> 1) + (hi >> 1) + (lo & hi & 1)
        if value_steps:
            vmid = _key(0.5 * _unkey(lo) + 0.5 * _unkey(hi))
            vmid = jnp.minimum(jnp.maximum(vmid, lo + 1), jnp.maximum(hi - 1, lo + 1))
            mid = jnp.where(jnp.logical_and(it < value_steps, positive), vmid, mid)
        cand = _unkey(mid)
        cnt, = count([lambda s: s >= cand])
        ge = jnp.logical_and(open_ > 0, cnt >= kf)
        lt = jnp.logical_and(open_ > 0, cnt < kf)
        lo = jnp.where(ge, mid, lo)
        cnt_lo = jnp.where(ge, cnt, cnt_lo)
        hi = jnp.where(lt, mid, hi)
        settled = (cnt_lo == kf) | (hi == lo + 1)
        open_ = jnp.where(settled, 0, open_)
        go = jnp.where(it < max_steps, jnp.max(open_), 0)
        return it + 1, lo, hi, cnt_lo, open_, go

    if bounds is None:
        lo0 = jnp.full(qshape, _KEY_NEG_INF, I32)
        hi0 = jnp.full(qshape, _KEY_POS_INF, I32)
        cnt0 = jnp.full(qshape, -1.0, F32)
        open0 = active
    else:
        n_gt0, n_ge0 = count([lambda s: s > 0.0, lambda s: s >= 0.0])
        positive = n_gt0 >= kf
        non_negative = n_ge0 >= kf
        key_zero = _key(jnp.zeros(qshape, F32))
        lo0 = jnp.where(non_negative, key_zero, _key(bounds[0]))
        cnt0 = jnp.where(non_negative, n_ge0, -1.0)
        hi0 = jnp.where(positive, _key(bounds[1]) + 1, jnp.where(non_negative, key_zero + 1, key_zero))
        lo0 = jnp.where(active, lo0, _KEY_NEG_INF)
        hi0 = jnp.where(active, hi0, _KEY_POS_INF)
        open0 = jnp.logical_and(active, jnp.logical_not((cnt0 == kf) | (hi0 == lo0 + 1)))
    open0 = jnp.where(open0, 1, 0).astype(I32)
    _, lo, _, _, _, _ = lax.while_loop(cond, body, (jnp.int32(0), lo0, hi0, cnt0, open0, jnp.max(open0)))
    thr = jnp.where(active, _unkey(lo), -jnp.inf)
    n_gt, n_ge = count([lambda s: s > thr, lambda s: s >= thr])
    need = jnp.where(active, kf - n_gt, 0.0)
    ties = jnp.where(jnp.logical_and(active, (n_ge - n_gt) > need), 1, 0).astype(I32)
    return thr, need, ties


def _earlier_matrix(w, key_axis):
    r_i = lax.broadcasted_iota(I32, (w, w), 0)
    c_i = lax.broadcasted_iota(I32, (w, w), 1)
    return jnp.where(r_i < c_i if key_axis == 1 else c_i < r_i, 1.0, 0.0).astype(BF16)


def _selection_bias(s, thr, need, tie_flag, tie_cnt_ref, bias_ref, earlier_ref, key_axis=1):
    w = s.shape[key_axis]

    @pl.when(tie_flag == 0)
    def _():
        sel = jnp.logical_and(s >= thr, s > -jnp.inf)
        bias_ref[...] = jnp.where(sel, 0.0, NEG_BIG)

    @pl.when(tie_flag != 0)
    def _():
        step = min(w, earlier_ref.shape[0])
        earlier = earlier_ref[:step, :step]
        seen = tie_cnt_ref[...]
        for a in range(0, w, step):
            piece = (slice(None), slice(a, a + step)) if key_axis == 1 else (slice(a, a + step), slice(None))
            sp = s[piece]
            eq = sp == thr
            eqf = jnp.where(eq, 1.0, 0.0)
            if key_axis == 1:
                rank = jnp.dot(eqf.astype(BF16), earlier, preferred_element_type=F32)
            else:
                rank = jnp.dot(earlier, eqf.astype(BF16), preferred_element_type=F32)
            sel = jnp.logical_or(sp > thr, jnp.logical_and(eq, rank + seen < need))
            bias_ref[piece] = jnp.where(sel, 0.0, NEG_BIG)
            seen = seen + jnp.sum(eqf, axis=key_axis, keepdims=True)
        tie_cnt_ref[...] = seen


def _dsa_prompt_kernel(qb_ref, qi_ref, wi_ref, ki_ref, k_hbm, vt_hbm, o_ref,
                       k_scr, vt_scr, sc_scr, qm_scr, m_scr, l_scr, acc_scr, bias_scr, tie_scr, earlier_scr, sem,
                       *, q_blk, k_blk, seq, n_sel):
    b = pl.program_id(0)
    i = pl.program_id(1)

    @pl.when(i == 0)
    def _():
        ck = pltpu.make_async_copy(k_hbm.at[pl.ds(b * seq, seq)], k_scr, sem.at[0])
        cv = pltpu.make_async_copy(vt_hbm.at[:, pl.ds(b * seq, seq)], vt_scr, sem.at[1])
        ck.start()
        cv.start()
        ck.wait()
        cv.wait()

    q0 = i * q_blk
    n_blocks = (q0 + q_blk + k_blk - 1) // k_blk
    q_pos = q0 + lax.broadcasted_iota(I32, (1, q_blk), 1)

    qit = qi_ref[...].T.astype(BF16)
    qit_h = [qit[h * IDX_DIM:(h + 1) * IDX_DIM] for h in range(IDX_HEADS)]
    wit = wi_ref[...].T
    w_h = [wit[_WI_LANE + h:_WI_LANE + h + 1] for h in range(IDX_HEADS)]

    def score_body(c, carry):
        off = pl.multiple_of(c * k_blk, k_blk)
        kc = ki_ref[pl.ds(off, k_blk), :]
        s = w_h[0] * jnp.maximum(jnp.dot(kc, qit_h[0], preferred_element_type=F32), 0.0)
        for h in range(1, IDX_HEADS):
            s = s + w_h[h] * jnp.maximum(jnp.dot(kc, qit_h[h], preferred_element_type=F32), 0.0)
        k_pos = off + lax.broadcasted_iota(I32, (k_blk, 1), 0)
        adm = k_pos <= q_pos
        sc_scr[pl.ds(off, k_blk), :] = jnp.where(adm, s, -jnp.inf)
        hi_part, lo_part = carry
        hi_part = jnp.maximum(hi_part, _reduce_rows(jnp.where(adm, s, -jnp.inf), jnp.max))
        lo_part = jnp.minimum(lo_part, _reduce_rows(jnp.where(adm, s, jnp.inf), jnp.min))
        return hi_part, lo_part

    hi_part, lo_part = lax.fori_loop(
        0, n_blocks, score_body,
        (jnp.full((SUBLANES, q_blk), -jnp.inf, F32), jnp.full((SUBLANES, q_blk), jnp.inf, F32)))
    s_max = jnp.max(hi_part, axis=0, keepdims=True)
    s_min = jnp.min(lo_part, axis=0, keepdims=True)

    def read_block(c):
        return sc_scr[pl.ds(pl.multiple_of(c * k_blk, k_blk), k_blk), :]

    active = (q_pos + 1) > n_sel
    thr, need, ties = _kth_largest_threshold(read_block, n_blocks, q_blk, n_sel, active, key_axis=0,
                                             bounds=(s_min, s_max))
    tie_flag = jnp.max(ties)

    qt = (qb_ref[...] * (DSA_DH ** -0.5)).T
    row_lo = lax.broadcasted_iota(I32, (LANES, 1), 0) < DSA_DH
    for p in range(DSA_HEADS // 2):
        qp = qt[p * LANES:(p + 1) * LANES]
        qm_scr[2 * p] = jnp.where(row_lo, qp, 0.0).astype(BF16)
        qm_scr[2 * p + 1] = jnp.where(row_lo, 0.0, qp).astype(BF16)
    m_scr[...] = jnp.full(m_scr.shape, NEG_BIG, F32)
    l_scr[...] = jnp.zeros(l_scr.shape, F32)
    acc_scr[...] = jnp.zeros(acc_scr.shape, F32)
    tie_scr[...] = jnp.zeros(tie_scr.shape, F32)

    @pl.when(tie_flag != 0)
    def _():
        earlier_scr[...] = _earlier_matrix(k_blk, 0)

    def att_body(j, carry):
        off = pl.multiple_of(j * k_blk, k_blk)
        _selection_bias(sc_scr[pl.ds(off, k_blk), :], thr, need, tie_flag, tie_scr, bias_scr, earlier_scr,
                        key_axis=0)
        bias = bias_scr[...]

        def qk(h):
            p = h // 2
            kp = k_scr[pl.ds(off, k_blk), p * LANES:(p + 1) * LANES]
            return jnp.dot(kp, qm_scr[h], preferred_element_type=F32)

        prs, alphas = [], []
        for h in range(DSA_HEADS):
            logit = qk(h) + bias
            m_old = m_scr[h]
            m_new = jnp.maximum(m_old, jnp.max(_reduce_rows(logit, jnp.max), axis=0, keepdims=True))
            alpha = jnp.exp(m_old - m_new)
            pe = jnp.exp(logit - m_new)
            l_scr[h] = alpha * l_scr[h] + jnp.sum(_reduce_rows(pe, jnp.sum), axis=0, keepdims=True)
            m_scr[h] = m_new
            prs.append(pe.astype(BF16))
            alphas.append(alpha)
        for h in range(DSA_HEADS):
            rows = slice(h * DSA_DH, (h + 1) * DSA_DH)
            vth = vt_scr[rows, pl.ds(off, k_blk)]
            acc_scr[rows, :] = alphas[h] * acc_scr[rows, :] + jnp.dot(vth, prs[h], preferred_element_type=F32)
        return carry

    lax.fori_loop(0, n_blocks, att_body, 0)

    for h in range(DSA_HEADS):
        rows = slice(h * DSA_DH, (h + 1) * DSA_DH)
        acc_scr[rows, :] = acc_scr[rows, :] / l_scr[h]
    o_ref[...] = acc_scr[...].T


def _dsa_prompt(qb, qi, wi, ki16, k16, vt16, batch, seq, q_blk, k_blk):
    n_sel = min(DSA_TOPK, seq // 4)
    nq = seq // q_blk
    qspec = lambda w: pl.BlockSpec((q_blk, w), lambda b, i: (b * nq + i, 0))
    return pl.pallas_call(
        functools.partial(_dsa_prompt_kernel, q_blk=q_blk, k_blk=k_blk, seq=seq, n_sel=n_sel),
        grid=(batch, nq),
        in_specs=[qspec(DSA_W), qspec(IDX_W), qspec(LANES),
                  pl.BlockSpec((seq, IDX_DIM), lambda b, i: (b, 0)),
                  pl.BlockSpec(memory_space=pl.ANY), pl.BlockSpec(memory_space=pl.ANY)],
        out_specs=qspec(DSA_W),
        out_shape=jax.ShapeDtypeStruct((batch * seq, DSA_W), F32),
        scratch_shapes=[pltpu.VMEM((seq, DSA_W), BF16), pltpu.VMEM((DSA_W, seq), BF16),
                        pltpu.VMEM((seq, q_blk), F32),
                        pltpu.VMEM((DSA_HEADS, LANES, q_blk), BF16),
                        pltpu.VMEM((DSA_HEADS, 1, q_blk), F32), pltpu.VMEM((DSA_HEADS, 1, q_blk), F32),
                        pltpu.VMEM((DSA_W, q_blk), F32), pltpu.VMEM((k_blk, q_blk), F32),
                        pltpu.VMEM((1, q_blk), F32), pltpu.VMEM((k_blk, k_blk), BF16),
                        pltpu.SemaphoreType.DMA((2,))],
        compiler_params=_params(("arbitrary", "arbitrary")),
        name="dsa_prompt",
    )(qb, qi, wi, ki16, k16, vt16)


PAGES_PER_STEP = 32
ATTN_PARTS = 2
TIE_RANK_WIDTH = 1024
SCORE_PAGES_PER_STEP = 32
_STEP_KEYS = PAGES_PER_STEP * PAGE_SIZE
_NT = (((1,), (1,)), ((), ()))


def _page_specs(page_shape, n_pages, per_step, clamp_last):
    specs = []
    zeros = (0,) * len(page_shape)
    for s in range(per_step):
        def imap(b, c, pt, s=s):
            page = jnp.minimum(c * per_step + s, n_pages - per_step + s) if clamp_last else c * per_step + s
            return (pt[b, page],) + zeros
        specs.append(pl.BlockSpec((1,) + page_shape, imap))
    return specs


def _stack_heads(qi):
    return jnp.concatenate([qi[:, h * IDX_DIM:(h + 1) * IDX_DIM] for h in range(IDX_HEADS)], axis=0)


def _idx_score(s4, wi, t):
    s = wi[:, _WI_LANE:_WI_LANE + 1] * jnp.maximum(s4[:t], 0.0)
    for h in range(1, IDX_HEADS):
        s = s + wi[:, _WI_LANE + h:_WI_LANE + h + 1] * jnp.maximum(s4[h * t:(h + 1) * t], 0.0)
    return s


def _dsa_sample_score_kernel(pt_ref, qi_ref, wi_ref, kin_ref, *rest, t, past):
    pages = rest[:SCORE_PAGES_PER_STEP]
    sc_ref, = rest[SCORE_PAGES_PER_STEP:]
    c = pl.program_id(1)
    nch = pl.num_programs(1)
    step_keys = SCORE_PAGES_PER_STEP * PAGE_SIZE
    qi4 = _stack_heads(qi_ref[...]).astype(BF16)
    wi = wi_ref[...]
    keys_t = jnp.concatenate([p[0] for p in pages], axis=1).astype(BF16)
    off = pl.multiple_of(c * step_keys, step_keys)
    sc_ref[0, :, pl.ds(off, step_keys)] = _idx_score(
        jnp.dot(qi4, keys_t, preferred_element_type=F32), wi, t)

    @pl.when(c == nch - 1)
    def _():
        knew = jnp.concatenate([kin_ref[...], jnp.zeros((LANES - t, IDX_DIM), F32)], axis=0).astype(BF16)
        s = _idx_score(lax.dot_general(qi4, knew, _NT, preferred_element_type=F32), wi, t)
        col = lax.broadcasted_iota(I32, (t, LANES), 1)
        row = lax.broadcasted_iota(I32, (t, LANES), 0)
        sc_ref[0, :, past:past + LANES] = jnp.where(col <= row, s, -jnp.inf)


THRESHOLD_GROUP = 8


def _dsa_sample_threshold_kernel(sc_ref, thr_ref, need_ref, flag_ref, *, t, past, n_sel):
    g = sc_ref.shape[0]
    rows = g * t

    def scores(_):
        return sc_ref[...].reshape(rows, sc_ref.shape[2])

    s = scores(0)
    s_max = jnp.max(s, axis=1, keepdims=True)
    s_min = jnp.min(jnp.where(s > -jnp.inf, s, jnp.inf), axis=1, keepdims=True)
    q_in_seq = lax.broadcasted_iota(I32, (g, t, 1), 1).reshape(rows, 1)
    active = (past + 1 + q_in_seq) > n_sel
    thr, need, ties = _kth_largest_threshold(scores, 1, rows, n_sel, active, bounds=(s_min, s_max))
    thr_ref[...] = thr.reshape(g, t, 1)
    need_ref[...] = need.reshape(g, t, 1)
    flag_ref[...] = ties.astype(F32).reshape(g, t, 1)


def _dsa_sample_attn_kernel(pt_ref, qb_ref, sc_ref, thr_ref, need_ref, flag_ref, kn_ref, vn_ref, *rest,
                            t, past):
    kpages = rest[:PAGES_PER_STEP]
    vpages = rest[PAGES_PER_STEP:2 * PAGES_PER_STEP]
    o_ref, m_scr, l_scr, acc_scr, bias_scr, biasn_scr, tie_scr, earlier_scr = rest[2 * PAGES_PER_STEP:]
    c = pl.program_id(1)
    nch = pl.num_programs(1) - 1
    flag = jnp.max(flag_ref[0]).astype(I32)

    @pl.when(jnp.logical_and(c == 0, flag != 0))
    def _():
        earlier_scr[...] = _earlier_matrix(TIE_RANK_WIDTH, 1)

    @pl.when(c == 0)
    def _():
        m_scr[...] = jnp.full(m_scr.shape, NEG_BIG, F32)
        l_scr[...] = jnp.zeros(l_scr.shape, F32)
        acc_scr[...] = jnp.zeros(acc_scr.shape, F32)
        tie_scr[...] = jnp.zeros(tie_scr.shape, F32)

    rows = DSA_HEADS * t
    blockmask = (lax.broadcasted_iota(I32, (rows, DSA_W), 0) // t
                 == lax.broadcasted_iota(I32, (rows, DSA_W), 1) // DSA_DH)
    qs = qb_ref[...] * (DSA_DH ** -0.5)
    qbd = jnp.where(blockmask, jnp.concatenate([qs] * DSA_HEADS, axis=0), 0.0).astype(BF16)
    thr = thr_ref[0]
    need = need_ref[0]

    def probs(part, bias, k_op, keys_on_lanes):
        if keys_on_lanes:
            logit = jnp.dot(qbd, k_op, preferred_element_type=F32)
        else:
            logit = lax.dot_general(qbd, k_op, _NT, preferred_element_type=F32)
        logit = logit + jnp.concatenate([bias] * DSA_HEADS, axis=0)
        m_old = m_scr[part]
        m_new = jnp.maximum(m_old, jnp.max(logit, axis=-1, keepdims=True))
        alpha = jnp.exp(m_old - m_new)
        pr = jnp.exp(logit - m_new).astype(BF16)
        l_scr[part] = alpha * l_scr[part] + jnp.sum(pr.astype(F32), axis=-1, keepdims=True)
        m_scr[part] = m_new
        return pr, alpha

    def accumulate(part, pr, alpha, v_op, keys_on_lanes):
        if keys_on_lanes:
            pv = lax.dot_general(pr, v_op, _NT, preferred_element_type=F32)
        else:
            pv = jnp.dot(pr, v_op, preferred_element_type=F32)
        acc_scr[part] = alpha * acc_scr[part] + pv

    @pl.when(c < nch)
    def _():
        off = pl.multiple_of(c * _STEP_KEYS, _STEP_KEYS)
        _selection_bias(sc_ref[0, :, pl.ds(off, _STEP_KEYS)], thr, need, flag, tie_scr, bias_scr, earlier_scr)
        bias = bias_scr[...]
        per = PAGES_PER_STEP // ATTN_PARTS
        stats = []
        for part in range(ATTN_PARTS):
            kc = jnp.concatenate([p[0].reshape(DSA_W, PAGE_SIZE)
                                  for p in kpages[part * per:(part + 1) * per]], axis=1).astype(BF16)
            stats.append(probs(part, bias[:, part * per * PAGE_SIZE:(part + 1) * per * PAGE_SIZE], kc, True))
        for part in range(ATTN_PARTS):
            vc = jnp.concatenate([p[0].reshape(DSA_W, PAGE_SIZE)
                                  for p in vpages[part * per:(part + 1) * per]], axis=1).astype(BF16)
            accumulate(part, stats[part][0], stats[part][1], vc, True)

    @pl.when(c == nch)
    def _():
        _selection_bias(sc_ref[0, :, past:past + LANES], thr, need, flag, tie_scr, biasn_scr, earlier_scr)
        zpad = jnp.zeros((LANES - t, DSA_W), F32)
        kc = jnp.concatenate([kn_ref[...], zpad], axis=0).astype(BF16)
        vc = jnp.concatenate([vn_ref[...], zpad], axis=0).astype(BF16)
        pr, alpha = probs(0, biasn_scr[...], kc, False)
        accumulate(0, pr, alpha, vc, False)
        m_all = m_scr[0]
        for part in range(1, ATTN_PARTS):
            m_all = jnp.maximum(m_all, m_scr[part])
        l_all = jnp.zeros_like(m_all)
        acc_all = jnp.zeros(acc_scr.shape[1:], F32)
        for part in range(ATTN_PARTS):
            w_part = jnp.exp(m_scr[part] - m_all)
            l_all = l_all + w_part * l_scr[part]
            acc_all = acc_all + w_part * acc_scr[part]
        outn = jnp.where(blockmask, acc_all / l_all, 0.0)
        out = outn[:t]
        for h in range(1, DSA_HEADS):
            out = out + outn[h * t:(h + 1) * t]
        o_ref[...] = out


def _dsa_sample(qb, qi, wi, ki_new, k_new, v_new, cache_k, cache_v, cache_idx_k, page_table, batch, t):
    n_pages = page_table.shape[1]
    past = n_pages * PAGE_SIZE
    n_sel = min(DSA_TOPK, (past + t) // 4)
    nch = n_pages // PAGES_PER_STEP
    nch_score = n_pages // SCORE_PAGES_PER_STEP
    lp = past + LANES
    kv_page = (DSA_HEADS, DSA_DH, PAGE_SIZE)
    ck = jnp.transpose(cache_k, (0, 2, 3, 1))
    cv = jnp.transpose(cache_v, (0, 2, 3, 1))
    cik = jnp.transpose(cache_idx_k, (0, 2, 1))
    rspec = lambda w: pl.BlockSpec((t, w), lambda b, c, pt: (b, 0))
    bspec = lambda w: pl.BlockSpec((1, t, w), lambda b, c, pt: (b, 0, 0))
    scores = pl.pallas_call(
        functools.partial(_dsa_sample_score_kernel, t=t, past=past),
        grid_spec=pltpu.PrefetchScalarGridSpec(
            num_scalar_prefetch=1,
            grid=(batch, nch_score),
            in_specs=[rspec(IDX_W), rspec(LANES), rspec(IDX_DIM)]
            + _page_specs((IDX_DIM, PAGE_SIZE), n_pages, SCORE_PAGES_PER_STEP, False),
            out_specs=bspec(lp),
        ),
        out_shape=jax.ShapeDtypeStruct((batch, t, lp), F32),
        compiler_params=_params(("arbitrary", "arbitrary")),
        name="dsa_sample_score",
    )(page_table, qi, wi, ki_new, *([cik] * SCORE_PAGES_PER_STEP))
    grp = THRESHOLD_GROUP if batch % THRESHOLD_GROUP == 0 else 1
    gspec = lambda w: pl.BlockSpec((grp, t, w), lambda i: (i, 0, 0))
    thr, need, flag = pl.pallas_call(
        functools.partial(_dsa_sample_threshold_kernel, t=t, past=past, n_sel=n_sel),
        grid=(batch // grp,),
        in_specs=[gspec(lp)],
        out_specs=[gspec(1)] * 3,
        out_shape=[jax.ShapeDtypeStruct((batch, t, 1), F32)] * 3,
        compiler_params=_params(("parallel",)),
        name="dsa_sample_threshold",
    )(scores)
    return pl.pallas_call(
        functools.partial(_dsa_sample_attn_kernel, t=t, past=past),
        grid_spec=pltpu.PrefetchScalarGridSpec(
            num_scalar_prefetch=1,
            grid=(batch, nch + 1),
            in_specs=[rspec(DSA_W), bspec(lp), bspec(1), bspec(1), bspec(1), rspec(DSA_W), rspec(DSA_W)]
            + _page_specs(kv_page, n_pages, PAGES_PER_STEP, True)
            + _page_specs(kv_page, n_pages, PAGES_PER_STEP, True),
            out_specs=rspec(DSA_W),
            scratch_shapes=[pltpu.VMEM((ATTN_PARTS, DSA_HEADS * t, 1), F32),
                            pltpu.VMEM((ATTN_PARTS, DSA_HEADS * t, 1), F32),
                            pltpu.VMEM((ATTN_PARTS, DSA_HEADS * t, DSA_W), F32), pltpu.VMEM((t, _STEP_KEYS), F32),
                            pltpu.VMEM((t, LANES), F32), pltpu.VMEM((t, 1), F32),
                            pltpu.VMEM((TIE_RANK_WIDTH, TIE_RANK_WIDTH), BF16)],
        ),
        out_shape=jax.ShapeDtypeStruct((batch * t, DSA_W), F32),
        compiler_params=_params(("arbitrary", "arbitrary")),
        name="dsa_sample_attn",
    )(page_table, qb, scores, thr, need, flag, k_new, v_new,
      *([ck] * PAGES_PER_STEP), *([cv] * PAGES_PER_STEP))


MOE_ROWS = 256
ROUTE_TILE = 512
TOKEN_TILE = 256
MAX_TOKEN_TILE = 1280


def _token_tile(n, also_divides=None):
    best = TOKEN_TILE
    for t in range(TOKEN_TILE, MAX_TOKEN_TILE + 1, TOKEN_TILE):
        if n % t == 0 and (also_divides is None or also_divides % t == 0):
            best = t
    return best


def _split_bf16(a):
    hi = a.astype(BF16)
    lo = (a - hi.astype(F32)).astype(BF16)
    return hi, lo


def _router_kernel(x_ref, whi_ref, wlo_ref, b_ref, idx_ref, gate_ref, rank_ref, cnt_ref, carry_scr):
    i = pl.program_id(0)
    tm = x_ref.shape[0]

    @pl.when(i == 0)
    def _():
        carry_scr[...] = jnp.zeros(carry_scr.shape, F32)

    xhi, xlo = _split_bf16(x_ref[...])
    whi = whi_ref[...]
    logits = (jnp.dot(xhi, whi, preferred_element_type=F32)
              + jnp.dot(xlo, whi, preferred_element_type=F32)
              + jnp.dot(xhi, wlo_ref[...], preferred_element_type=F32)) + b_ref[...]
    lane = lax.broadcasted_iota(I32, (tm, N_EXPERTS), 1)
    slot = lax.broadcasted_iota(I32, (tm, LANES), 1)
    vals, idxs = [], []
    cur = logits
    for _ in range(TOP_K):
        m = jnp.max(cur, axis=-1, keepdims=True)
        ix = jnp.min(jnp.where(cur == m, lane, N_EXPERTS), axis=-1, keepdims=True)
        vals.append(m)
        idxs.append(ix)
        cur = jnp.where(lane == ix, -jnp.inf, cur)
    es = [jnp.exp(v - vals[0]) for v in vals]
    denom = es[0] + es[1] + es[2] + es[3]
    onehot = jnp.zeros((tm, N_EXPERTS), F32)
    for ix in idxs:
        onehot = onehot + jnp.where(lane == ix, 1.0, 0.0)
    earlier = (lax.broadcasted_iota(I32, (tm, tm), 1) < lax.broadcasted_iota(I32, (tm, tm), 0))
    excl = jnp.dot(jnp.where(earlier, 1.0, 0.0).astype(BF16), onehot.astype(BF16),
                   preferred_element_type=F32) + carry_scr[...]
    idx_out = jnp.zeros((tm, LANES), I32)
    gate_out = jnp.zeros((tm, LANES), F32)
    rank_out = jnp.zeros((tm, LANES), F32)
    for k in range(TOP_K):
        rk = jnp.sum(jnp.where(lane == idxs[k], excl, 0.0), axis=-1, keepdims=True)
        idx_out = jnp.where(slot == k, idxs[k], idx_out)
        gate_out = jnp.where(slot == k, es[k] / denom, gate_out)
        rank_out = jnp.where(slot == k, rk, rank_out)
    idx_ref[...] = idx_out
    gate_ref[...] = gate_out
    rank_ref[...] = rank_out.astype(I32)
    total = carry_scr[...] + jnp.sum(onehot, axis=0, keepdims=True)
    carry_scr[...] = total
    cnt_ref[...] = total


def _router(x, w_router, b_router):
    n = x.shape[0]
    tm = ROUTE_TILE if n % ROUTE_TILE == 0 else TOKEN_TILE
    whi, wlo = _split_bf16(w_router)
    b2 = b_router.reshape(1, N_EXPERTS)
    kspec = _row_spec(tm, LANES)
    return pl.pallas_call(
        _router_kernel,
        grid=(n // tm,),
        in_specs=[_row_spec(tm, D_MODEL), _full_spec(whi), _full_spec(wlo), _full_spec(b2)],
        out_specs=[kspec, kspec, kspec, pl.BlockSpec((1, N_EXPERTS), lambda i: (0, 0))],
        out_shape=[jax.ShapeDtypeStruct((n, LANES), I32), jax.ShapeDtypeStruct((n, LANES), F32),
                   jax.ShapeDtypeStruct((n, LANES), I32), jax.ShapeDtypeStruct((1, N_EXPERTS), F32)],
        scratch_shapes=[pltpu.VMEM((1, N_EXPERTS), F32)],
        compiler_params=_params(("arbitrary",)),
        name="router",
    )(x, whi, wlo, b2)


def _row_copy(src, src_row, dst, dst_row, sem):
    return pltpu.make_async_copy(src.at[pl.ds(src_row, 1)], dst.at[pl.ds(dst_row, 1)], sem)


def _dispatch_kernel(dest_ref, x_ref, xs_in, xs_out, sem):
    del xs_in
    tm = x_ref.shape[0]

    def issue(r, c):
        for k in range(TOP_K):
            _row_copy(x_ref, r, xs_out, dest_ref[r * TOP_K + k], sem).start(priority=k % 2)
        return c

    lax.fori_loop(0, tm, issue, 0)
    for _ in range(TOP_K):
        pltpu.make_async_copy(x_ref, x_ref, sem).wait()


def _dispatch(x, dest_flat, n_rows):
    n = x.shape[0]
    tm = _token_tile(n)
    return pl.pallas_call(
        _dispatch_kernel,
        grid=(n // tm,),
        in_specs=[pl.BlockSpec((tm * TOP_K,), lambda i: (i,), memory_space=pltpu.SMEM),
                  _row_spec(tm, D_MODEL), pl.BlockSpec(memory_space=pl.ANY)],
        out_specs=pl.BlockSpec(memory_space=pl.ANY),
        out_shape=jax.ShapeDtypeStruct((n_rows, D_MODEL), F32),
        scratch_shapes=[pltpu.SemaphoreType.DMA(())],
        input_output_aliases={2: 0},
        compiler_params=_params(("arbitrary",)),
        name="moe_dispatch",
    )(dest_flat, x, jnp.zeros((n_rows, D_MODEL), F32))


def _expert_kernel(be_ref, na_ref, x_ref, w1_ref, b1_ref, w2_ref, b2_ref, o_ref, w1b_scr, w2b_scr):
    i = pl.program_id(0)
    active = i < na_ref[0]
    changed = jnp.logical_or(i == 0, be_ref[i] != be_ref[jnp.maximum(i - 1, 0)])

    @pl.when(jnp.logical_and(active, changed))
    def _():
        w1b_scr[...] = w1_ref[0].astype(BF16)
        w2b_scr[...] = w2_ref[0].astype(BF16)

    @pl.when(active)
    def _():
        h = jnp.dot(x_ref[...].astype(BF16), w1b_scr[...], preferred_element_type=F32) + b1_ref[0]
        gt = jnp.minimum(h[:, :D_FF], SWIGLU_LIMIT)
        up = jnp.clip(h[:, D_FF:], -SWIGLU_LIMIT, SWIGLU_LIMIT)
        act = (up + 1.0) * gt * jax.nn.sigmoid(SWIGLU_ALPHA * gt)
        o_ref[...] = jnp.dot(act.astype(BF16), w2b_scr[...], preferred_element_type=F32) + b2_ref[0]

    @pl.when(jnp.logical_not(active))
    def _():
        o_ref[...] = jnp.zeros(o_ref.shape, F32)


def _experts(xs, block_e, n_active, w1, b1, w2, b2):
    n_rows = xs.shape[0]
    nblk = n_rows // MOE_ROWS
    w1 = w1.reshape(-1, D_MODEL, 2 * D_FF)
    w2 = w2.reshape(-1, D_FF, D_MODEL)
    b1r = b1.reshape(-1, 1, 2 * D_FF)
    b2r = b2.reshape(-1, 1, D_MODEL)
    last = lambda i, na: jnp.minimum(i, na[0] - 1)
    return pl.pallas_call(
        _expert_kernel,
        grid_spec=pltpu.PrefetchScalarGridSpec(
            num_scalar_prefetch=2,
            grid=(nblk,),
            in_specs=[pl.BlockSpec((MOE_ROWS, D_MODEL), lambda i, be, na: (last(i, na), 0)),
                      pl.BlockSpec((1, D_MODEL, 2 * D_FF), lambda i, be, na: (be[i], 0, 0)),
                      pl.BlockSpec((1, 1, 2 * D_FF), lambda i, be, na: (be[i], 0, 0)),
                      pl.BlockSpec((1, D_FF, D_MODEL), lambda i, be, na: (be[i], 0, 0)),
                      pl.BlockSpec((1, 1, D_MODEL), lambda i, be, na: (be[i], 0, 0))],
            out_specs=pl.BlockSpec((MOE_ROWS, D_MODEL), lambda i, be, na: (i, 0)),
            scratch_shapes=[pltpu.VMEM((D_MODEL, 2 * D_FF), BF16), pltpu.VMEM((D_FF, D_MODEL), BF16)],
        ),
        out_shape=jax.ShapeDtypeStruct((n_rows, D_MODEL), F32),
        compiler_params=_params(("arbitrary",)),
        name="moe_experts",
    )(block_e, n_active, xs, w1, b1r, w2, b2r)


def _combine_kernel(dest_ref, x_ref, gate_ref, g_ref, b_ref, yb_hbm, *rest, first_tiles):
    outs, (buf, sem) = rest[:-2], rest[-2:]
    tm = x_ref.shape[0]

    def issue(r, c):
        for k in range(TOP_K):
            _row_copy(yb_hbm, dest_ref[r * TOP_K + k], buf.at[k], r, sem).start(priority=k % 2)
        return c

    lax.fori_loop(0, tm, issue, 0)
    for k in range(TOP_K):
        pltpu.make_async_copy(buf.at[k], buf.at[k], sem).wait()
    gate = gate_ref[...]
    y = gate[:, 0:1] * buf[0]
    for k in range(1, TOP_K):
        y = y + gate[:, k:k + 1] * buf[k]
    res = _layer_norm(DN_ALPHA * x_ref[...] + y, g_ref[...], b_ref[...])
    if first_tiles is None:
        outs[0][...] = res
    else:
        i = pl.program_id(0)

        @pl.when(i < first_tiles)
        def _():
            outs[0][...] = res

        @pl.when(i >= first_tiles)
        def _():
            outs[1][...] = res


def _combine(x, yb, dest_flat, gate, g, b, split=None):
    n = x.shape[0]
    tm = _token_tile(n, split)
    g2, b2 = g.reshape(1, D_MODEL), b.reshape(1, D_MODEL)
    if split is None:
        first_tiles = None
        out_specs = _row_spec(tm, D_MODEL)
        out_shape = jax.ShapeDtypeStruct((n, D_MODEL), F32)
    else:
        first_tiles = split // tm
        out_specs = [pl.BlockSpec((tm, D_MODEL), lambda i: (jnp.minimum(i, first_tiles - 1), 0)),
                     pl.BlockSpec((tm, D_MODEL), lambda i: (jnp.maximum(i - first_tiles, 0), 0))]
        out_shape = [jax.ShapeDtypeStruct((split, D_MODEL), F32), jax.ShapeDtypeStruct((n - split, D_MODEL), F32)]
    return pl.pallas_call(
        functools.partial(_combine_kernel, first_tiles=first_tiles),
        grid=(n // tm,),
        in_specs=[pl.BlockSpec((tm * TOP_K,), lambda i: (i,), memory_space=pltpu.SMEM),
                  _row_spec(tm, D_MODEL), _row_spec(tm, LANES),
                  _full_spec(g2), _full_spec(b2), pl.BlockSpec(memory_space=pl.ANY)],
        out_specs=out_specs,
        out_shape=out_shape,
        scratch_shapes=[pltpu.VMEM((TOP_K, tm, D_MODEL), F32), pltpu.SemaphoreType.DMA(())],
        compiler_params=_params(("arbitrary",)),
        name="moe_combine",
    )(dest_flat, x, gate, g2, b2, yb)


def _moe_layer(x, layer, w_router, b_router, w1, b1, w2, b2, g, b, split=None):
    n = x.shape[0]
    idx, gate, rank, counts = _router(x, w_router, b_router)
    nblk = (n * TOP_K) // MOE_ROWS + N_EXPERTS
    cnt = counts[0].astype(I32)
    padded = (cnt + MOE_ROWS - 1) // MOE_ROWS * MOE_ROWS
    pad_end = jnp.cumsum(padded)
    pad_start = pad_end - padded
    dest = (pad_start[idx] + rank)[:, :TOP_K].reshape(-1)
    n_active = (pad_end[-1] // MOE_ROWS).astype(I32)
    blk = jnp.arange(nblk, dtype=I32)
    blk = jnp.minimum(blk, n_active - 1)
    block_e = jnp.sum((pad_end[None, :] <= (blk * MOE_ROWS)[:, None]).astype(I32), axis=1)
    block_e = jnp.minimum(block_e, N_EXPERTS - 1) + layer * N_EXPERTS
    xs = _dispatch(x, dest, nblk * MOE_ROWS)
    yb = _experts(xs, block_e, n_active.reshape(1), w1, b1, w2, b2)
    return _combine(x, yb, dest, gate, g, b, split)


POOL_HALO = 16


def _pool_kernel(x_ref, halo_ref, w_ref, sc_ref, g_ref, b_ref, o_ref, ext_scr, *, tiles_per_seq, n_prev):
    tm = x_ref.shape[0]
    tile = pl.program_id(0) % tiles_per_seq
    x = x_ref[...]
    halo = halo_ref[...]
    if n_prev == 0:
        halo = jnp.where(tile == 0, 0.0, halo)
    ext_scr[0:POOL_HALO] = halo
    ext_scr[POOL_HALO:POOL_HALO + tm] = x
    pos = n_prev + tile * tm + lax.broadcasted_iota(I32, (tm, 1), 0)
    parts = []
    for g, w in enumerate(POOL_WINDOWS):
        cols = slice(g * POOL_GROUP, (g + 1) * POOL_GROUP)
        xg = x[:, cols]
        s = xg
        for d in range(1, w):
            s = s + ext_scr[POOL_HALO - d:POOL_HALO - d + tm, cols]
        cnt = jnp.minimum(pos + 1, w).astype(F32)
        dg = s / cnt - xg
        parts.append(jnp.dot(dg.astype(BF16), w_ref[g], preferred_element_type=F32))
    mix = jnp.concatenate(parts, axis=1) * sc_ref[...]
    o_ref[...] = _layer_norm(DN_ALPHA * x + mix, g_ref[...], b_ref[...])


def _pool(x, halo_src, pool_w, pool_scale, g, b, tm, tiles_per_seq, n_prev, halo_map, n, row_off=0):
    w16 = pool_w.astype(BF16)
    sc = pool_scale.reshape(1, D_MODEL)
    g2, b2 = g.reshape(1, D_MODEL), b.reshape(1, D_MODEL)
    blk_off = row_off // tm
    return pl.pallas_call(
        functools.partial(_pool_kernel, tiles_per_seq=tiles_per_seq, n_prev=n_prev),
        grid=(n // tm,),
        in_specs=[pl.BlockSpec((tm, D_MODEL), lambda i: (i + blk_off, 0)),
                  pl.BlockSpec((POOL_HALO, D_MODEL), halo_map),
                  _full_spec(w16), _full_spec(sc), _full_spec(g2), _full_spec(b2)],
        out_specs=_row_spec(tm, D_MODEL),
        out_shape=jax.ShapeDtypeStruct((n, D_MODEL), F32),
        scratch_shapes=[pltpu.VMEM((POOL_HALO + tm, D_MODEL), F32)],
        compiler_params=_params(("parallel",)),
        name="pool",
    )(x, halo_src, w16, sc, g2, b2)


PROJ_TILE = 512
DSA_Q_BLOCK = 512
DSA_K_BLOCK = 512
POOL_TILE = 512


def kernel(x_prompt, x_sample, cache_k, cache_v, cache_idx_k, page_table, state_gla, state_pool,
           w_in, gla_fg_w2, gla_fg_b, gla_norm_g, idx_kn_g, idx_kn_b, w_out,
           pool_w, pool_scale, ln_mix_g, ln_mix_b, ln_ffn_g, ln_ffn_b,
           moe_router_w, moe_router_b, moe_w1, moe_b1, moe_w2, moe_b2):
    bp, sp, _ = x_prompt.shape
    bs, ts, _ = x_sample.shape
    n_p, n_s = bp * sp, bs * ts
    xp = x_prompt.reshape(n_p, D_MODEL)
    xs = x_sample.reshape(n_s, D_MODEL)
    proj_w = (w_in, gla_fg_w2, gla_fg_b, idx_kn_g, idx_kn_b)

    (qa, ka, va, ga, lf, qb, _, _, qi, _, wi, kb16, vt16, ki16,
     kt, vt, kit) = _project(xp, *proj_w, tm=PROJ_TILE, seq=sp)
    oa_p, gla_p = _gla(qa, ka, va, lf, ga, jnp.zeros((bp, GLA_HEADS, GLA_DK, GLA_DV), F32), gla_norm_g, bp, sp)
    ob_p = _dsa_prompt(qb, qi, wi, ki16, kb16, vt16, bp, sp, DSA_Q_BLOCK, DSA_K_BLOCK)
    x1p = _merge(xp, oa_p, ob_p, w_out, ln_mix_g[0], ln_mix_b[0], PROJ_TILE)
    k_p = jnp.transpose(kt.reshape(bp, DSA_HEADS, DSA_DH, sp), (0, 3, 1, 2))
    v_p = jnp.transpose(vt.reshape(bp, DSA_HEADS, DSA_DH, sp), (0, 3, 1, 2))
    kidx_p = jnp.transpose(kit, (0, 2, 1))

    qa, ka, va, ga, lf, qb, kb, vb, qi, ki, wi, _, _, _ = _project(xs, *proj_w, tm=n_s)
    oa_s, gla_s = _gla(qa, ka, va, lf, ga, state_gla, gla_norm_g, bs, ts)
    ob_s = _dsa_sample(qb, qi, wi, ki, kb, vb, cache_k, cache_v, cache_idx_k, page_table, bs, ts)
    x1s = _merge(xs, oa_s, ob_s, w_out, ln_mix_g[0], ln_mix_b[0], n_s)
    k_s = kb.reshape(bs, ts, DSA_HEADS, DSA_DH)
    v_s = vb.reshape(bs, ts, DSA_HEADS, DSA_DH)
    kidx_s = ki.reshape(bs, ts, IDX_DIM)

    x_all = _moe_layer(jnp.concatenate([x1p, x1s], axis=0), 0, moe_router_w[0], moe_router_b[0],
                       moe_w1, moe_b1, moe_w2, moe_b2, ln_ffn_g[0], ln_ffn_b[0])

    pool_p = jnp.stack([x_all[(s + 1) * sp - POOL_STATE:(s + 1) * sp] for s in range(bp)])
    xs3 = x_all[n_p:].reshape(bs, ts, D_MODEL)
    pool_s = jnp.concatenate([state_pool, xs3], axis=1)[:, -POOL_STATE:]
    per_seq = sp // POOL_TILE
    halo_step = POOL_TILE // POOL_HALO
    x2p = _pool(x_all, x_all, pool_w, pool_scale, ln_mix_g[1], ln_mix_b[1], POOL_TILE, per_seq, 0,
                lambda i: (jnp.maximum(i * halo_step - 1, 0), 0), n_p)
    halo_s = jnp.concatenate([jnp.zeros((bs, POOL_HALO - POOL_STATE, D_MODEL), F32), state_pool], axis=1)
    x2s = _pool(x_all, halo_s.reshape(bs * POOL_HALO, D_MODEL), pool_w, pool_scale, ln_mix_g[1], ln_mix_b[1],
                ts, 1, POOL_STATE, lambda i: (i, 0), n_s, row_off=n_p)
    y_p, y_s = _moe_layer(jnp.concatenate([x2p, x2s], axis=0), 1, moe_router_w[1], moe_router_b[1],
                          moe_w1, moe_b1, moe_w2, moe_b2, ln_ffn_g[1], ln_ffn_b[1], split=n_p)
    y_p = y_p.reshape(bp, sp, D_MODEL)
    y_s = y_s.reshape(bs, ts, D_MODEL)
    return (y_p, y_s, k_p, v_p, kidx_p, gla_p, pool_p, k_s, v_s, kidx_s, gla_s, pool_s)
```

```python
import functools
import math

import jax
import jax.numpy as jnp
from jax import lax
from jax.experimental import pallas as pl
from jax.experimental.pallas import tpu as pltpu

F32 = jnp.float32
BF16 = jnp.bfloat16
I32 = jnp.int32

D_MODEL = 1024
DEPTH = 2
PAGE_SIZE = 128
GLA_HEADS = 4
GLA_DK = 64
GLA_DV = 128
GLA_GATE_RANK = 16
GLA_TAU = 16.0
GLA_CHUNK = 64
GLA_TILE = 2 * GLA_CHUNK
DSA_HEADS = 8
DSA_DH = 64
IDX_HEADS = 4
IDX_DIM = 64
DSA_TOPK = 256
IDX_W_SCALE = (IDX_HEADS ** -0.5) * (IDX_DIM ** -0.5)
POOL_WINDOWS = (2, 4, 8, 16)
POOL_GROUP = D_MODEL // 4
POOL_STATE = 16 - 1
N_EXPERTS = 32
TOP_K = 4
D_FF = D_MODEL
SWIGLU_ALPHA = 1.702
SWIGLU_LIMIT = 7.0
DN_ALPHA = (2 * DEPTH) ** 0.25
LN_EPS = 1e-5
GLA_QK = GLA_HEADS * GLA_DK
GLA_V = GLA_HEADS * GLA_DV
DSA_W = DSA_HEADS * DSA_DH
IDX_W = IDX_HEADS * IDX_DIM

LANES = 128
SUBLANES = 8
VMEM_LIMIT_BYTES = 56 * 1024 * 1024

NEG_BIG = -1e30


def _params(sem, vmem=VMEM_LIMIT_BYTES):
    return pltpu.CompilerParams(dimension_semantics=sem, vmem_limit_bytes=vmem)


def _layer_norm(x, g, b):
    mu = jnp.mean(x, axis=-1, keepdims=True)
    xc = x - mu
    var = jnp.mean(xc * xc, axis=-1, keepdims=True)
    return xc * lax.rsqrt(var + LN_EPS) * g + b


def _row_spec(tm, w):
    return pl.BlockSpec((tm, w), lambda i: (i, 0))


def _full_spec(a):
    nd = a.ndim
    return pl.BlockSpec(a.shape, lambda i: (0,) * nd)


_MAIN_W = (GLA_QK, GLA_QK, GLA_V, GLA_V, DSA_W, DSA_W, DSA_W, IDX_W)
_MAIN_OFF = tuple(sum(_MAIN_W[:i]) for i in range(len(_MAIN_W) + 1))


def _proj_kernel(x_ref, wm_ref, ws_ref, fgw_ref, fgb_ref, kng_ref, knb_ref,
                 qa_ref, ka_ref, va_ref, ga_ref, lf_ref, qb_ref, kb_ref, vb_ref,
                 qi_ref, ki_ref, wi_ref, kb16_ref, vt16_ref, ki16_ref, *channel_major):
    xb = x_ref[...].astype(BF16)

    def mm(n):
        return jnp.dot(xb, wm_ref[:, _MAIN_OFF[n]:_MAIN_OFF[n + 1]], preferred_element_type=F32)

    qa_ref[...] = mm(0) * (GLA_DK ** -0.5)
    ka_ref[...] = mm(1)
    va_ref[...] = mm(2)
    ga_ref[...] = mm(3)
    qb_ref[...] = mm(4)
    kb = mm(5)
    kb_ref[...] = kb
    kb16_ref[...] = kb.astype(BF16)
    vb = mm(6)
    vb_ref[...] = vb
    vbt = vb.T
    vt16_ref[...] = vbt.astype(BF16)
    qi_ref[...] = mm(7)

    small = jnp.dot(xb, ws_ref[...], preferred_element_type=F32)
    ki = _layer_norm(small[:, :IDX_DIM], kng_ref[...], knb_ref[...])
    ki_ref[...] = ki
    ki16_ref[...] = ki.astype(BF16)
    if channel_major:
        kt_ref, vt_ref, kit_ref = channel_major
        kt_ref[0] = kb.T
        vt_ref[0] = vbt
        kit_ref[0] = jnp.concatenate([ki, jnp.zeros_like(ki)], axis=1).T[:IDX_DIM]
    fa = small[:, IDX_DIM:IDX_DIM + GLA_GATE_RANK]
    z = jnp.dot(fa, fgw_ref[...], preferred_element_type=F32,
                precision=lax.Precision.HIGHEST) + fgb_ref[...]
    lf_ref[...] = (jnp.minimum(z, 0.0) - jnp.log(1.0 + jnp.exp(-jnp.abs(z)))) * (1.0 / GLA_TAU)
    wi_ref[...] = small * IDX_W_SCALE


def _project(x2d, w_in, gla_fg_w2, gla_fg_b, idx_kn_g, idx_kn_b, tm, seq=None):
    n = x2d.shape[0]
    pts = [0]
    for s in (GLA_QK, GLA_QK, GLA_V, GLA_V, GLA_GATE_RANK, DSA_W, DSA_W, DSA_W, IDX_W, IDX_DIM, IDX_HEADS):
        pts.append(pts[-1] + s)
    seg = lambda i: w_in[:, pts[i]:pts[i + 1]]
    wm = jnp.concatenate([seg(0), seg(1), seg(2), seg(3), seg(5), seg(6), seg(7), seg(8)], axis=1).astype(BF16)
    pad = LANES - IDX_DIM - GLA_GATE_RANK - IDX_HEADS
    ws = jnp.concatenate([seg(9), seg(4), seg(10), jnp.zeros((D_MODEL, pad), w_in.dtype)], axis=1).astype(BF16)
    fgb = gla_fg_b.reshape(1, GLA_QK)
    kng = idx_kn_g.reshape(1, IDX_DIM)
    knb = idx_kn_b.reshape(1, IDX_DIM)
    widths = (GLA_QK, GLA_QK, GLA_V, GLA_V, GLA_QK, DSA_W, DSA_W, DSA_W, IDX_W, IDX_DIM, LANES)
    out_shape = [jax.ShapeDtypeStruct((n, w), F32) for w in widths]
    out_shape += [jax.ShapeDtypeStruct((n, DSA_W), BF16), jax.ShapeDtypeStruct((DSA_W, n), BF16),
                  jax.ShapeDtypeStruct((n, IDX_DIM), BF16)]
    out_specs = [_row_spec(tm, w) for w in widths]
    out_specs += [_row_spec(tm, DSA_W), pl.BlockSpec((DSA_W, tm), lambda i: (0, i)), _row_spec(tm, IDX_DIM)]
    if seq is not None:
        tps = seq // tm
        for w in (DSA_W, DSA_W, IDX_DIM):
            out_shape.append(jax.ShapeDtypeStruct((n // seq, w, seq), F32))
            out_specs.append(pl.BlockSpec((1, w, tm), lambda i: (i // tps, 0, i % tps)))
    return pl.pallas_call(
        _proj_kernel,
        grid=(n // tm,),
        in_specs=[_row_spec(tm, D_MODEL), _full_spec(wm), _full_spec(ws), _full_spec(gla_fg_w2),
                  _full_spec(fgb), _full_spec(kng), _full_spec(knb)],
        out_specs=out_specs,
        out_shape=out_shape,
        compiler_params=_params(("parallel",)),
        name="proj",
    )(x2d, wm, ws, gla_fg_w2, fgb, kng, knb)


def _gla_kernel(q_ref, k_ref, v_ref, lf_ref, ga_ref, s0_ref, ng_ref, o_ref, sout_ref, s_scr, *, chunk):
    c = pl.program_id(1)
    nc = pl.num_programs(1)

    @pl.when(c == 0)
    def _():
        s_scr[...] = s0_ref[0]

    g = lf_ref[...]
    ri = lax.broadcasted_iota(I32, (chunk, chunk), 0)
    ci = lax.broadcasted_iota(I32, (chunk, chunk), 1)
    causal = ci <= ri
    tri = jnp.where(causal, 1.0, 0.0).astype(F32)
    b = jnp.dot(tri, g, preferred_element_type=F32, precision=lax.Precision.HIGHEST)
    mid = chunk // 2
    b_mid = b[mid:mid + 1, :]
    b_last = b[chunk - 1:chunk, :]
    q = q_ref[...]
    k = k_ref[...]
    q_in = q * jnp.exp(b)
    q_rel = q * jnp.exp(b - b_mid)
    k_rel = k * jnp.exp(b_mid - b)
    k_out = k * jnp.exp(b_last - b)
    dec_last = jnp.exp(b_last)
    v = v_ref[...]
    ga = ga_ref[...]
    ng = ng_ref[...]
    eye = jnp.where(lax.broadcasted_iota(I32, (GLA_DK, GLA_DK), 0)
                    == lax.broadcasted_iota(I32, (GLA_DK, GLA_DK), 1), 1.0, 0.0).astype(F32)
    for h in range(GLA_HEADS):
        ks = slice(h * GLA_DK, (h + 1) * GLA_DK)
        vs = slice(h * GLA_DV, (h + 1) * GLA_DV)
        s_h = s_scr[h]
        vh = v[:, vs]
        inter = jnp.dot(q_in[:, ks], s_h, preferred_element_type=F32)
        att = lax.dot_general(q_rel[:, ks], k_rel[:, ks], (((1,), (1,)), ((), ())),
                              preferred_element_type=F32)
        att = jnp.where(causal, att, 0.0)
        o = inter + jnp.dot(att, vh, preferred_element_type=F32)
        kv = lax.dot_general(k_out[:, ks], vh, (((0,), (0,)), ((), ())), preferred_element_type=F32)
        s_scr[h] = jnp.dot(eye * dec_last[:, ks], s_h, preferred_element_type=F32,
                           precision=lax.Precision.HIGHEST) + kv
        ms = jnp.mean(o * o, axis=-1, keepdims=True)
        gh = ga[:, vs]
        o_ref[:, vs] = o * lax.rsqrt(ms + LN_EPS) * ng * (gh * jax.nn.sigmoid(gh))

    @pl.when(c == nc - 1)
    def _():
        sout_ref[0] = s_scr[...]


def _gla(qa, ka, va, lf, ga, s0, gla_norm_g, batch, seq):
    chunk = math.gcd(seq, GLA_TILE)
    nc = seq // chunk
    ng = gla_norm_g.reshape(1, GLA_DV)
    spec = lambda w: pl.BlockSpec((chunk, w), lambda b, c: (b * nc + c, 0))
    sspec = pl.BlockSpec((1, GLA_HEADS, GLA_DK, GLA_DV), lambda b, c: (b, 0, 0, 0))
    return pl.pallas_call(
        functools.partial(_gla_kernel, chunk=chunk),
        grid=(batch, nc),
        in_specs=[spec(GLA_QK), spec(GLA_QK), spec(GLA_V), spec(GLA_QK), spec(GLA_V), sspec,
                  pl.BlockSpec((1, GLA_DV), lambda b, c: (0, 0))],
        out_specs=[spec(GLA_V), sspec],
        out_shape=[jax.ShapeDtypeStruct((batch * seq, GLA_V), F32),
                   jax.ShapeDtypeStruct((batch, GLA_HEADS, GLA_DK, GLA_DV), F32)],
        scratch_shapes=[pltpu.VMEM((GLA_HEADS, GLA_DK, GLA_DV), F32)],
        compiler_params=_params(("parallel", "arbitrary")),
        name="gla",
    )(qa, ka, va, lf, ga, s0, ng)


def _merge_kernel(x_ref, oa_ref, ob_ref, w_ref, g_ref, b_ref, o_ref):
    mix = jnp.dot(oa_ref[...].astype(BF16), w_ref[:GLA_V, :], preferred_element_type=F32)
    mix = mix + jnp.dot(ob_ref[...].astype(BF16), w_ref[GLA_V:, :], preferred_element_type=F32)
    o_ref[...] = _layer_norm(DN_ALPHA * x_ref[...] + mix, g_ref[...], b_ref[...])


def _merge(x2d, oa, ob, w_out, g, b, tm):
    n = x2d.shape[0]
    w16 = w_out.astype(BF16)
    g2, b2 = g.reshape(1, D_MODEL), b.reshape(1, D_MODEL)
    return pl.pallas_call(
        _merge_kernel,
        grid=(n // tm,),
        in_specs=[_row_spec(tm, D_MODEL), _row_spec(tm, GLA_V), _row_spec(tm, DSA_W),
                  _full_spec(w16), _full_spec(g2), _full_spec(b2)],
        out_specs=_row_spec(tm, D_MODEL),
        out_shape=jax.ShapeDtypeStruct((n, D_MODEL), F32),
        compiler_params=_params(("parallel",)),
        name="merge",
    )(x2d, oa, ob, w16, g2, b2)


_KEY_NEG_INF = -2139095041
_KEY_POS_INF = 2139095040
_WI_LANE = IDX_DIM + GLA_GATE_RANK


def _unkey(kk):
    return lax.bitcast_convert_type(jnp.where(kk < 0, kk ^ 0x7FFFFFFF, kk), F32)


def _key(v):
    i = lax.bitcast_convert_type(v, I32)
    return jnp.where(i < 0, i ^ 0x7FFFFFFF, i)


_REDUCE_CHAINS = 8


def _reduce_rows(x, reduce_fn):
    r, q = x.shape
    groups = r // SUBLANES
    chains = _REDUCE_CHAINS if groups % _REDUCE_CHAINS == 0 else 1
    y = x.reshape(chains, groups // chains, SUBLANES, q)
    return reduce_fn(reduce_fn(y, axis=1), axis=0)


def _fold_keys(m, key_axis):
    if key_axis == 1:
        out = m[:, :LANES]
        for t in range(1, m.shape[1] // LANES):
            out = out + m[:, t * LANES:(t + 1) * LANES]
        return out
    return _reduce_rows(m, jnp.sum)


_VALUE_BISECTIONS = 12
_KEY_BITS = 32


def _kth_largest_threshold(read_block, n_blocks, rows, k, active, key_axis=1, bounds=None):
    kf = float(k)
    qshape = (rows, 1) if key_axis == 1 else (1, rows)
    part = (rows, LANES) if key_axis == 1 else (SUBLANES, rows)
    value_steps = _VALUE_BISECTIONS if bounds is not None else 0
    max_steps = value_steps + _KEY_BITS + 2

    def count(preds):
        def body(c, accs):
            s = read_block(c)
            return tuple(a + _fold_keys(jnp.where(p(s), 1.0, 0.0), key_axis) for a, p in zip(accs, preds))
        accs = lax.fori_loop(0, n_blocks, body, tuple(jnp.zeros(part, F32) for _ in preds))
        return [jnp.sum(a, axis=key_axis, keepdims=True) for a in accs]

    def cond(st):
        return st[5] > 0

    def body(st):
        it, lo, hi, cnt_lo, open_, _ = st
        mid = (lo >> 1) + (hi >> 1) + (lo & hi & 1)
        if value_steps:
            vmid = _key(0.5 * _unkey(lo) + 0.5 * _unkey(hi))
            vmid = jnp.minimum(jnp.maximum(vmid, lo + 1), jnp.maximum(hi - 1, lo + 1))
            mid = jnp.where(jnp.logical_and(it < value_steps, positive), vmid, mid)
        cand = _unkey(mid)
        cnt, = count([lambda s: s >= cand])
        ge = jnp.logical_and(open_ > 0, cnt >= kf)
        lt = jnp.logical_and(open_ > 0, cnt < kf)
        lo = jnp.where(ge, mid, lo)
        cnt_lo = jnp.where(ge, cnt, cnt_lo)
        hi = jnp.where(lt, mid, hi)
        settled = (cnt_lo == kf) | (hi == lo + 1)
        open_ = jnp.where(settled, 0, open_)
        go = jnp.where(it < max_steps, jnp.max(open_), 0)
        return it + 1, lo, hi, cnt_lo, open_, go

    if bounds is None:
        lo0 = jnp.full(qshape, _KEY_NEG_INF, I32)
        hi0 = jnp.full(qshape, _KEY_POS_INF, I32)
        cnt0 = jnp.full(qshape, -1.0, F32)
        open0 = active
    else:
        n_gt0, n_ge0 = count([lambda s: s > 0.0, lambda s: s >= 0.0])
        positive = n_gt0 >= kf
        non_negative = n_ge0 >= kf
        key_zero = _key(jnp.zeros(qshape, F32))
        lo0 = jnp.where(non_negative, key_zero, _key(bounds[0]))
        cnt0 = jnp.where(non_negative, n_ge0, -1.0)
        hi0 = jnp.where(positive, _key(bounds[1]) + 1, jnp.where(non_negative, key_zero + 1, key_zero))
        lo0 = jnp.where(active, lo0, _KEY_NEG_INF)
        hi0 = jnp.where(active, hi0, _KEY_POS_INF)
        open0 = jnp.logical_and(active, jnp.logical_not((cnt0 == kf) | (hi0 == lo0 + 1)))
    open0 = jnp.where(open0, 1, 0).astype(I32)
    _, lo, _, _, _, _ = lax.while_loop(cond, body, (jnp.int32(0), lo0, hi0, cnt0, open0, jnp.max(open0)))
    thr = jnp.where(active, _unkey(lo), -jnp.inf)
    n_gt, n_ge = count([lambda s: s > thr, lambda s: s >= thr])
    need = jnp.where(active, kf - n_gt, 0.0)
    ties = jnp.where(jnp.logical_and(active, (n_ge - n_gt) > need), 1, 0).astype(I32)
    return thr, need, ties


def _earlier_matrix(w, key_axis):
    r_i = lax.broadcasted_iota(I32, (w, w), 0)
    c_i = lax.broadcasted_iota(I32, (w, w), 1)
    return jnp.where(r_i < c_i if key_axis == 1 else c_i < r_i, 1.0, 0.0).astype(BF16)


def _selection_bias(s, thr, need, tie_flag, tie_cnt_ref, bias_ref, earlier_ref, key_axis=1):
    w = s.shape[key_axis]

    @pl.when(tie_flag == 0)
    def _():
        sel = jnp.logical_and(s >= thr, s > -jnp.inf)
        bias_ref[...] = jnp.where(sel, 0.0, NEG_BIG)

    @pl.when(tie_flag != 0)
    def _():
        step = min(w, earlier_ref.shape[0])
        earlier = earlier_ref[:step, :step]
        seen = tie_cnt_ref[...]
        for a in range(0, w, step):
            piece = (slice(None), slice(a, a + step)) if key_axis == 1 else (slice(a, a + step), slice(None))
            sp = s[piece]
            eq = sp == thr
            eqf = jnp.where(eq, 1.0, 0.0)
            if key_axis == 1:
                rank = jnp.dot(eqf.astype(BF16), earlier, preferred_element_type=F32)
            else:
                rank = jnp.dot(earlier, eqf.astype(BF16), preferred_element_type=F32)
            sel = jnp.logical_or(sp > thr, jnp.logical_and(eq, rank + seen < need))
            bias_ref[piece] = jnp.where(sel, 0.0, NEG_BIG)
            seen = seen + jnp.sum(eqf, axis=key_axis, keepdims=True)
        tie_cnt_ref[...] = seen


def _dsa_prompt_kernel(qb_ref, qi_ref, wi_ref, ki_ref, k_hbm, vt_hbm, o_ref,
                       k_scr, vt_scr, sc_scr, qm_scr, m_scr, l_scr, acc_scr, bias_scr, tie_scr, earlier_scr, sem,
                       *, q_blk, k_blk, seq, n_sel):
    b = pl.program_id(0)
    i = pl.program_id(1)

    @pl.when(i == 0)
    def _():
        ck = pltpu.make_async_copy(k_hbm.at[pl.ds(b * seq, seq)], k_scr, sem.at[0])
        cv = pltpu.make_async_copy(vt_hbm.at[:, pl.ds(b * seq, seq)], vt_scr, sem.at[1])
        ck.start()
        cv.start()
        ck.wait()
        cv.wait()

    q0 = i * q_blk
    n_blocks = (q0 + q_blk + k_blk - 1) // k_blk
    q_pos = q0 + lax.broadcasted_iota(I32, (1, q_blk), 1)

    qit = qi_ref[...].T.astype(BF16)
    qit_h = [qit[h * IDX_DIM:(h + 1) * IDX_DIM] for h in range(IDX_HEADS)]
    wit = wi_ref[...].T
    w_h = [wit[_WI_LANE + h:_WI_LANE + h + 1] for h in range(IDX_HEADS)]

    def score_body(c, carry):
        off = pl.multiple_of(c * k_blk, k_blk)
        kc = ki_ref[pl.ds(off, k_blk), :]
        s = w_h[0] * jnp.maximum(jnp.dot(kc, qit_h[0], preferred_element_type=F32), 0.0)
        for h in range(1, IDX_HEADS):
            s = s + w_h[h] * jnp.maximum(jnp.dot(kc, qit_h[h], preferred_element_type=F32), 0.0)
        k_pos = off + lax.broadcasted_iota(I32, (k_blk, 1), 0)
        adm = k_pos <= q_pos
        sc_scr[pl.ds(off, k_blk), :] = jnp.where(adm, s, -jnp.inf)
        hi_part, lo_part = carry
        hi_part = jnp.maximum(hi_part, _reduce_rows(jnp.where(adm, s, -jnp.inf), jnp.max))
        lo_part = jnp.minimum(lo_part, _reduce_rows(jnp.where(adm, s, jnp.inf), jnp.min))
        return hi_part, lo_part

    hi_part, lo_part = lax.fori_loop(
        0, n_blocks, score_body,
        (jnp.full((SUBLANES, q_blk), -jnp.inf, F32), jnp.full((SUBLANES, q_blk), jnp.inf, F32)))
    s_max = jnp.max(hi_part, axis=0, keepdims=True)
    s_min = jnp.min(lo_part, axis=0, keepdims=True)

    def read_block(c):
        return sc_scr[pl.ds(pl.multiple_of(c * k_blk, k_blk), k_blk), :]

    active = (q_pos + 1) > n_sel
    thr, need, ties = _kth_largest_threshold(read_block, n_blocks, q_blk, n_sel, active, key_axis=0,
                                             bounds=(s_min, s_max))
    tie_flag = jnp.max(ties)

    qt = (qb_ref[...] * (DSA_DH ** -0.5)).T
    row_lo = lax.broadcasted_iota(I32, (LANES, 1), 0) < DSA_DH
    for p in range(DSA_HEADS // 2):
        qp = qt[p * LANES:(p + 1) * LANES]
        qm_scr[2 * p] = jnp.where(row_lo, qp, 0.0).astype(BF16)
        qm_scr[2 * p + 1] = jnp.where(row_lo, 0.0, qp).astype(BF16)
    m_scr[...] = jnp.full(m_scr.shape, NEG_BIG, F32)
    l_scr[...] = jnp.zeros(l_scr.shape, F32)
    acc_scr[...] = jnp.zeros(acc_scr.shape, F32)
    tie_scr[...] = jnp.zeros(tie_scr.shape, F32)

    @pl.when(tie_flag != 0)
    def _():
        earlier_scr[...] = _earlier_matrix(k_blk, 0)

    def att_body(j, carry):
        off = pl.multiple_of(j * k_blk, k_blk)
        _selection_bias(sc_scr[pl.ds(off, k_blk), :], thr, need, tie_flag, tie_scr, bias_scr, earlier_scr,
                        key_axis=0)
        bias = bias_scr[...]

        def qk(h):
            p = h // 2
            kp = k_scr[pl.ds(off, k_blk), p * LANES:(p + 1) * LANES]
            return jnp.dot(kp, qm_scr[h], preferred_element_type=F32)

        prs, alphas = [], []
        for h in range(DSA_HEADS):
            logit = qk(h) + bias
            m_old = m_scr[h]
            m_new = jnp.maximum(m_old, jnp.max(_reduce_rows(logit, jnp.max), axis=0, keepdims=True))
            alpha = jnp.exp(m_old - m_new)
            pe = jnp.exp(logit - m_new)
            l_scr[h] = alpha * l_scr[h] + jnp.sum(_reduce_rows(pe, jnp.sum), axis=0, keepdims=True)
            m_scr[h] = m_new
            prs.append(pe.astype(BF16))
            alphas.append(alpha)
        for h in range(DSA_HEADS):
            rows = slice(h * DSA_DH, (h + 1) * DSA_DH)
            vth = vt_scr[rows, pl.ds(off, k_blk)]
            acc_scr[rows, :] = alphas[h] * acc_scr[rows, :] + jnp.dot(vth, prs[h], preferred_element_type=F32)
        return carry

    lax.fori_loop(0, n_blocks, att_body, 0)

    for h in range(DSA_HEADS):
        rows = slice(h * DSA_DH, (h + 1) * DSA_DH)
        acc_scr[rows, :] = acc_scr[rows, :] / l_scr[h]
    o_ref[...] = acc_scr[...].T


def _dsa_prompt(qb, qi, wi, ki16, k16, vt16, batch, seq, q_blk, k_blk):
    n_sel = min(DSA_TOPK, seq // 4)
    nq = seq // q_blk
    qspec = lambda w: pl.BlockSpec((q_blk, w), lambda b, i: (b * nq + i, 0))
    return pl.pallas_call(
        functools.partial(_dsa_prompt_kernel, q_blk=q_blk, k_blk=k_blk, seq=seq, n_sel=n_sel),
        grid=(batch, nq),
        in_specs=[qspec(DSA_W), qspec(IDX_W), qspec(LANES),
                  pl.BlockSpec((seq, IDX_DIM), lambda b, i: (b, 0)),
                  pl.BlockSpec(memory_space=pl.ANY), pl.BlockSpec(memory_space=pl.ANY)],
        out_specs=qspec(DSA_W),
        out_shape=jax.ShapeDtypeStruct((batch * seq, DSA_W), F32),
        scratch_shapes=[pltpu.VMEM((seq, DSA_W), BF16), pltpu.VMEM((DSA_W, seq), BF16),
                        pltpu.VMEM((seq, q_blk), F32),
                        pltpu.VMEM((DSA_HEADS, LANES, q_blk), BF16),
                        pltpu.VMEM((DSA_HEADS, 1, q_blk), F32), pltpu.VMEM((DSA_HEADS, 1, q_blk), F32),
                        pltpu.VMEM((DSA_W, q_blk), F32), pltpu.VMEM((k_blk, q_blk), F32),
                        pltpu.VMEM((1, q_blk), F32), pltpu.VMEM((k_blk, k_blk), BF16),
                        pltpu.SemaphoreType.DMA((2,))],
        compiler_params=_params(("arbitrary", "arbitrary")),
        name="dsa_prompt",
    )(qb, qi, wi, ki16, k16, vt16)


PAGES_PER_STEP = 32
ATTN_PARTS = 2
TIE_RANK_WIDTH = 1024
SCORE_PAGES_PER_STEP = 32
_STEP_KEYS = PAGES_PER_STEP * PAGE_SIZE
_NT = (((1,), (1,)), ((), ()))


def _page_specs(page_shape, n_pages, per_step, clamp_last):
    specs = []
    zeros = (0,) * len(page_shape)
    for s in range(per_step):
        def imap(b, c, pt, s=s):
            page = jnp.minimum(c * per_step + s, n_pages - per_step + s) if clamp_last else c * per_step + s
            return (pt[b, page],) + zeros
        specs.append(pl.BlockSpec((1,) + page_shape, imap))
    return specs


def _stack_heads(qi):
    return jnp.concatenate([qi[:, h * IDX_DIM:(h + 1) * IDX_DIM] for h in range(IDX_HEADS)], axis=0)


def _idx_score(s4, wi, t):
    s = wi[:, _WI_LANE:_WI_LANE + 1] * jnp.maximum(s4[:t], 0.0)
    for h in range(1, IDX_HEADS):
        s = s + wi[:, _WI_LANE + h:_WI_LANE + h + 1] * jnp.maximum(s4[h * t:(h + 1) * t], 0.0)
    return s


def _dsa_sample_score_kernel(pt_ref, qi_ref, wi_ref, kin_ref, *rest, t, past):
    pages = rest[:SCORE_PAGES_PER_STEP]
    sc_ref, = rest[SCORE_PAGES_PER_STEP:]
    c = pl.program_id(1)
    nch = pl.num_programs(1)
    step_keys = SCORE_PAGES_PER_STEP * PAGE_SIZE
    qi4 = _stack_heads(qi_ref[...]).astype(BF16)
    wi = wi_ref[...]
    keys_t = jnp.concatenate([p[0] for p in pages], axis=1).astype(BF16)
    off = pl.multiple_of(c * step_keys, step_keys)
    sc_ref[0, :, pl.ds(off, step_keys)] = _idx_score(
        jnp.dot(qi4, keys_t, preferred_element_type=F32), wi, t)

    @pl.when(c == nch - 1)
    def _():
        knew = jnp.concatenate([kin_ref[...], jnp.zeros((LANES - t, IDX_DIM), F32)], axis=0).astype(BF16)
        s = _idx_score(lax.dot_general(qi4, knew, _NT, preferred_element_type=F32), wi, t)
        col = lax.broadcasted_iota(I32, (t, LANES), 1)
        row = lax.broadcasted_iota(I32, (t, LANES), 0)
        sc_ref[0, :, past:past + LANES] = jnp.where(col <= row, s, -jnp.inf)


THRESHOLD_GROUP = 8


def _dsa_sample_threshold_kernel(sc_ref, thr_ref, need_ref, flag_ref, *, t, past, n_sel):
    g = sc_ref.shape[0]
    rows = g * t

    def scores(_):
        return sc_ref[...].reshape(rows, sc_ref.shape[2])

    s = scores(0)
    s_max = jnp.max(s, axis=1, keepdims=True)
    s_min = jnp.min(jnp.where(s > -jnp.inf, s, jnp.inf), axis=1, keepdims=True)
    q_in_seq = lax.broadcasted_iota(I32, (g, t, 1), 1).reshape(rows, 1)
    active = (past + 1 + q_in_seq) > n_sel
    thr, need, ties = _kth_largest_threshold(scores, 1, rows, n_sel, active, bounds=(s_min, s_max))
    thr_ref[...] = thr.reshape(g, t, 1)
    need_ref[...] = need.reshape(g, t, 1)
    flag_ref[...] = ties.astype(F32).reshape(g, t, 1)


def _dsa_sample_attn_kernel(pt_ref, qb_ref, sc_ref, thr_ref, need_ref, flag_ref, kn_ref, vn_ref, *rest,
                            t, past):
    kpages = rest[:PAGES_PER_STEP]
    vpages = rest[PAGES_PER_STEP:2 * PAGES_PER_STEP]
    o_ref, m_scr, l_scr, acc_scr, bias_scr, biasn_scr, tie_scr, earlier_scr = rest[2 * PAGES_PER_STEP:]
    c = pl.program_id(1)
    nch = pl.num_programs(1) - 1
    flag = jnp.max(flag_ref[0]).astype(I32)

    @pl.when(jnp.logical_and(c == 0, flag != 0))
    def _():
        earlier_scr[...] = _earlier_matrix(TIE_RANK_WIDTH, 1)

    @pl.when(c == 0)
    def _():
        m_scr[...] = jnp.full(m_scr.shape, NEG_BIG, F32)
        l_scr[...] = jnp.zeros(l_scr.shape, F32)
        acc_scr[...] = jnp.zeros(acc_scr.shape, F32)
        tie_scr[...] = jnp.zeros(tie_scr.shape, F32)

    rows = DSA_HEADS * t
    blockmask = (lax.broadcasted_iota(I32, (rows, DSA_W), 0) // t
                 == lax.broadcasted_iota(I32, (rows, DSA_W), 1) // DSA_DH)
    qs = qb_ref[...] * (DSA_DH ** -0.5)
    qbd = jnp.where(blockmask, jnp.concatenate([qs] * DSA_HEADS, axis=0), 0.0).astype(BF16)
    thr = thr_ref[0]
    need = need_ref[0]

    def probs(part, bias, k_op, keys_on_lanes):
        if keys_on_lanes:
            logit = jnp.dot(qbd, k_op, preferred_element_type=F32)
        else:
            logit = lax.dot_general(qbd, k_op, _NT, preferred_element_type=F32)
        logit = logit + jnp.concatenate([bias] * DSA_HEADS, axis=0)
        m_old = m_scr[part]
        m_new = jnp.maximum(m_old, jnp.max(logit, axis=-1, keepdims=True))
        alpha = jnp.exp(m_old - m_new)
        pr = jnp.exp(logit - m_new).astype(BF16)
        l_scr[part] = alpha * l_scr[part] + jnp.sum(pr.astype(F32), axis=-1, keepdims=True)
        m_scr[part] = m_new
        return pr, alpha

    def accumulate(part, pr, alpha, v_op, keys_on_lanes):
        if keys_on_lanes:
            pv = lax.dot_general(pr, v_op, _NT, preferred_element_type=F32)
        else:
            pv = jnp.dot(pr, v_op, preferred_element_type=F32)
        acc_scr[part] = alpha * acc_scr[part] + pv

    @pl.when(c < nch)
    def _():
        off = pl.multiple_of(c * _STEP_KEYS, _STEP_KEYS)
        _selection_bias(sc_ref[0, :, pl.ds(off, _STEP_KEYS)], thr, need, flag, tie_scr, bias_scr, earlier_scr)
        bias = bias_scr[...]
        per = PAGES_PER_STEP // ATTN_PARTS
        stats = []
        for part in range(ATTN_PARTS):
            kc = jnp.concatenate([p[0].reshape(DSA_W, PAGE_SIZE)
                                  for p in kpages[part * per:(part + 1) * per]], axis=1).astype(BF16)
            stats.append(probs(part, bias[:, part * per * PAGE_SIZE:(part + 1) * per * PAGE_SIZE], kc, True))
        for part in range(ATTN_PARTS):
            vc = jnp.concatenate([p[0].reshape(DSA_W, PAGE_SIZE)
                                  for p in vpages[part * per:(part + 1) * per]], axis=1).astype(BF16)
            accumulate(part, stats[part][0], stats[part][1], vc, True)

    @pl.when(c == nch)
    def _():
        _selection_bias(sc_ref[0, :, past:past + LANES], thr, need, flag, tie_scr, biasn_scr, earlier_scr)
        zpad = jnp.zeros((LANES - t, DSA_W), F32)
        kc = jnp.concatenate([kn_ref[...], zpad], axis=0).astype(BF16)
        vc = jnp.concatenate([vn_ref[...], zpad], axis=0).astype(BF16)
        pr, alpha = probs(0, biasn_scr[...], kc, False)
        accumulate(0, pr, alpha, vc, False)
        m_all = m_scr[0]
        for part in range(1, ATTN_PARTS):
            m_all = jnp.maximum(m_all, m_scr[part])
        l_all = jnp.zeros_like(m_all)
        acc_all = jnp.zeros(acc_scr.shape[1:], F32)
        for part in range(ATTN_PARTS):
            w_part = jnp.exp(m_scr[part] - m_all)
            l_all = l_all + w_part * l_scr[part]
            acc_all = acc_all + w_part * acc_scr[part]
        outn = jnp.where(blockmask, acc_all / l_all, 0.0)
        out = outn[:t]
        for h in range(1, DSA_HEADS):
            out = out + outn[h * t:(h + 1) * t]
        o_ref[...] = out


def _dsa_sample(qb, qi, wi, ki_new, k_new, v_new, cache_k, cache_v, cache_idx_k, page_table, batch, t):
    n_pages = page_table.shape[1]
    past = n_pages * PAGE_SIZE
    n_sel = min(DSA_TOPK, (past + t) // 4)
    nch = n_pages // PAGES_PER_STEP
    nch_score = n_pages // SCORE_PAGES_PER_STEP
    lp = past + LANES
    kv_page = (DSA_HEADS, DSA_DH, PAGE_SIZE)
    ck = jnp.transpose(cache_k, (0, 2, 3, 1))
    cv = jnp.transpose(cache_v, (0, 2, 3, 1))
    cik = jnp.transpose(cache_idx_k, (0, 2, 1))
    rspec = lambda w: pl.BlockSpec((t, w), lambda b, c, pt: (b, 0))
    bspec = lambda w: pl.BlockSpec((1, t, w), lambda b, c, pt: (b, 0, 0))
    scores = pl.pallas_call(
        functools.partial(_dsa_sample_score_kernel, t=t, past=past),
        grid_spec=pltpu.PrefetchScalarGridSpec(
            num_scalar_prefetch=1,
            grid=(batch, nch_score),
            in_specs=[rspec(IDX_W), rspec(LANES), rspec(IDX_DIM)]
            + _page_specs((IDX_DIM, PAGE_SIZE), n_pages, SCORE_PAGES_PER_STEP, False),
            out_specs=bspec(lp),
        ),
        out_shape=jax.ShapeDtypeStruct((batch, t, lp), F32),
        compiler_params=_params(("arbitrary", "arbitrary")),
        name="dsa_sample_score",
    )(page_table, qi, wi, ki_new, *([cik] * SCORE_PAGES_PER_STEP))
    grp = THRESHOLD_GROUP if batch % THRESHOLD_GROUP == 0 else 1
    gspec = lambda w: pl.BlockSpec((grp, t, w), lambda i: (i, 0, 0))
    thr, need, flag = pl.pallas_call(
        functools.partial(_dsa_sample_threshold_kernel, t=t, past=past, n_sel=n_sel),
        grid=(batch // grp,),
        in_specs=[gspec(lp)],
        out_specs=[gspec(1)] * 3,
        out_shape=[jax.ShapeDtypeStruct((batch, t, 1), F32)] * 3,
        compiler_params=_params(("parallel",)),
        name="dsa_sample_threshold",
    )(scores)
    return pl.pallas_call(
        functools.partial(_dsa_sample_attn_kernel, t=t, past=past),
        grid_spec=pltpu.PrefetchScalarGridSpec(
            num_scalar_prefetch=1,
            grid=(batch, nch + 1),
            in_specs=[rspec(DSA_W), bspec(lp), bspec(1), bspec(1), bspec(1), rspec(DSA_W), rspec(DSA_W)]
            + _page_specs(kv_page, n_pages, PAGES_PER_STEP, True)
            + _page_specs(kv_page, n_pages, PAGES_PER_STEP, True),
            out_specs=rspec(DSA_W),
            scratch_shapes=[pltpu.VMEM((ATTN_PARTS, DSA_HEADS * t, 1), F32),
                            pltpu.VMEM((ATTN_PARTS, DSA_HEADS * t, 1), F32),
                            pltpu.VMEM((ATTN_PARTS, DSA_HEADS * t, DSA_W), F32), pltpu.VMEM((t, _STEP_KEYS), F32),
                            pltpu.VMEM((t, LANES), F32), pltpu.VMEM((t, 1), F32),
                            pltpu.VMEM((TIE_RANK_WIDTH, TIE_RANK_WIDTH), BF16)],
        ),
        out_shape=jax.ShapeDtypeStruct((batch * t, DSA_W), F32),
        compiler_params=_params(("arbitrary", "arbitrary")),
        name="dsa_sample_attn",
    )(page_table, qb, scores, thr, need, flag, k_new, v_new,
      *([ck] * PAGES_PER_STEP), *([cv] * PAGES_PER_STEP))


MOE_ROWS = 256
ROUTE_TILE = 512
TOKEN_TILE = 256
MAX_TOKEN_TILE = 1280


def _token_tile(n, also_divides=None):
    best = TOKEN_TILE
    for t in range(TOKEN_TILE, MAX_TOKEN_TILE + 1, TOKEN_TILE):
        if n % t == 0 and (also_divides is None or also_divides % t == 0):
            best = t
    return best


def _split_bf16(a):
    hi = a.astype(BF16)
    lo = (a - hi.astype(F32)).astype(BF16)
    return hi, lo


def _router_kernel(x_ref, whi_ref, wlo_ref, b_ref, idx_ref, gate_ref, rank_ref, cnt_ref, carry_scr):
    i = pl.program_id(0)
    tm = x_ref.shape[0]

    @pl.when(i == 0)
    def _():
        carry_scr[...] = jnp.zeros(carry_scr.shape, F32)

    xhi, xlo = _split_bf16(x_ref[...])
    prod = jnp.dot(jnp.concatenate([xhi, xlo], axis=0),
                   jnp.concatenate([whi_ref[...], wlo_ref[...]], axis=1), preferred_element_type=F32)
    logits = (prod[:tm, :N_EXPERTS] + prod[tm:, :N_EXPERTS] + prod[:tm, N_EXPERTS:]) + b_ref[...]
    lane = lax.broadcasted_iota(I32, (tm, N_EXPERTS), 1)
    slot = lax.broadcasted_iota(I32, (tm, LANES), 1)
    vals, idxs = [], []
    cur = logits
    for _ in range(TOP_K):
        m = jnp.max(cur, axis=-1, keepdims=True)
        ix = jnp.min(jnp.where(cur == m, lane, N_EXPERTS), axis=-1, keepdims=True)
        vals.append(m)
        idxs.append(ix)
        cur = jnp.where(lane == ix, -jnp.inf, cur)
    es = [jnp.exp(v - vals[0]) for v in vals]
    denom = es[0] + es[1] + es[2] + es[3]
    onehot = jnp.zeros((tm, N_EXPERTS), F32)
    for ix in idxs:
        onehot = onehot + jnp.where(lane == ix, 1.0, 0.0)
    earlier = (lax.broadcasted_iota(I32, (tm, tm), 1) < lax.broadcasted_iota(I32, (tm, tm), 0))
    excl = jnp.dot(jnp.where(earlier, 1.0, 0.0).astype(BF16), onehot.astype(BF16),
                   preferred_element_type=F32) + carry_scr[...]
    idx_out = jnp.zeros((tm, LANES), I32)
    gate_out = jnp.zeros((tm, LANES), F32)
    rank_out = jnp.zeros((tm, LANES), F32)
    for k in range(TOP_K):
        rk = jnp.sum(jnp.where(lane == idxs[k], excl, 0.0), axis=-1, keepdims=True)
        idx_out = jnp.where(slot == k, idxs[k], idx_out)
        gate_out = jnp.where(slot == k, es[k] / denom, gate_out)
        rank_out = jnp.where(slot == k, rk, rank_out)
    idx_ref[...] = idx_out
    gate_ref[...] = gate_out
    rank_ref[...] = rank_out.astype(I32)
    total = carry_scr[...] + jnp.sum(onehot, axis=0, keepdims=True)
    carry_scr[...] = total
    cnt_ref[...] = total


def _router(x, w_router, b_router):
    n = x.shape[0]
    tm = ROUTE_TILE if n % ROUTE_TILE == 0 else TOKEN_TILE
    whi, wlo = _split_bf16(w_router)
    b2 = b_router.reshape(1, N_EXPERTS)
    kspec = _row_spec(tm, LANES)
    return pl.pallas_call(
        _router_kernel,
        grid=(n // tm,),
        in_specs=[_row_spec(tm, D_MODEL), _full_spec(whi), _full_spec(wlo), _full_spec(b2)],
        out_specs=[kspec, kspec, kspec, pl.BlockSpec((1, N_EXPERTS), lambda i: (0, 0))],
        out_shape=[jax.ShapeDtypeStruct((n, LANES), I32), jax.ShapeDtypeStruct((n, LANES), F32),
                   jax.ShapeDtypeStruct((n, LANES), I32), jax.ShapeDtypeStruct((1, N_EXPERTS), F32)],
        scratch_shapes=[pltpu.VMEM((1, N_EXPERTS), F32)],
        compiler_params=_params(("arbitrary",)),
        name="router",
    )(x, whi, wlo, b2)


def _row_copy(src, src_row, dst, dst_row, sem):
    return pltpu.make_async_copy(src.at[pl.ds(src_row, 1)], dst.at[pl.ds(dst_row, 1)], sem)


def _dispatch_kernel(dest_ref, x_ref, xs_in, xs_out, sem):
    del xs_in
    tm = x_ref.shape[0]

    def issue(r, c):
        for k in range(TOP_K):
            _row_copy(x_ref, r, xs_out, dest_ref[r * TOP_K + k], sem).start(priority=k % 2)
        return c

    lax.fori_loop(0, tm, issue, 0)
    for _ in range(TOP_K):
        pltpu.make_async_copy(x_ref, x_ref, sem).wait()


def _dispatch(x, dest_flat, n_rows):
    n = x.shape[0]
    tm = _token_tile(n)
    return pl.pallas_call(
        _dispatch_kernel,
        grid=(n // tm,),
        in_specs=[pl.BlockSpec((tm * TOP_K,), lambda i: (i,), memory_space=pltpu.SMEM),
                  _row_spec(tm, D_MODEL), pl.BlockSpec(memory_space=pl.ANY)],
        out_specs=pl.BlockSpec(memory_space=pl.ANY),
        out_shape=jax.ShapeDtypeStruct((n_rows, D_MODEL), F32),
        scratch_shapes=[pltpu.SemaphoreType.DMA(())],
        input_output_aliases={2: 0},
        compiler_params=_params(("arbitrary",)),
        name="moe_dispatch",
    )(dest_flat, x, jnp.zeros((n_rows, D_MODEL), F32))


def _expert_kernel(be_ref, na_ref, x_ref, w1_ref, b1_ref, w2_ref, b2_ref, o_ref, w1b_scr, w2b_scr):
    i = pl.program_id(0)
    active = i < na_ref[0]
    changed = jnp.logical_or(i == 0, be_ref[i] != be_ref[jnp.maximum(i - 1, 0)])

    @pl.when(jnp.logical_and(active, changed))
    def _():
        w1b_scr[...] = w1_ref[0].astype(BF16)
        w2b_scr[...] = w2_ref[0].astype(BF16)

    @pl.when(active)
    def _():
        h = jnp.dot(x_ref[...].astype(BF16), w1b_scr[...], preferred_element_type=F32) + b1_ref[0]
        gt = jnp.minimum(h[:, :D_FF], SWIGLU_LIMIT)
        up = jnp.clip(h[:, D_FF:], -SWIGLU_LIMIT, SWIGLU_LIMIT)
        act = (up + 1.0) * gt * jax.nn.sigmoid(SWIGLU_ALPHA * gt)
        o_ref[...] = jnp.dot(act.astype(BF16), w2b_scr[...], preferred_element_type=F32) + b2_ref[0]

    @pl.when(jnp.logical_not(active))
    def _():
        o_ref[...] = jnp.zeros(o_ref.shape, F32)


def _experts(xs, block_e, n_active, w1, b1, w2, b2):
    n_rows = xs.shape[0]
    nblk = n_rows // MOE_ROWS
    w1 = w1.reshape(-1, D_MODEL, 2 * D_FF)
    w2 = w2.reshape(-1, D_FF, D_MODEL)
    b1r = b1.reshape(-1, 1, 2 * D_FF)
    b2r = b2.reshape(-1, 1, D_MODEL)
    last = lambda i, na: jnp.minimum(i, na[0] - 1)
    return pl.pallas_call(
        _expert_kernel,
        grid_spec=pltpu.PrefetchScalarGridSpec(
            num_scalar_prefetch=2,
            grid=(nblk,),
            in_specs=[pl.BlockSpec((MOE_ROWS, D_MODEL), lambda i, be, na: (last(i, na), 0)),
                      pl.BlockSpec((1, D_MODEL, 2 * D_FF), lambda i, be, na: (be[i], 0, 0)),
                      pl.BlockSpec((1, 1, 2 * D_FF), lambda i, be, na: (be[i], 0, 0)),
                      pl.BlockSpec((1, D_FF, D_MODEL), lambda i, be, na: (be[i], 0, 0)),
                      pl.BlockSpec((1, 1, D_MODEL), lambda i, be, na: (be[i], 0, 0))],
            out_specs=pl.BlockSpec((MOE_ROWS, D_MODEL), lambda i, be, na: (i, 0)),
            scratch_shapes=[pltpu.VMEM((D_MODEL, 2 * D_FF), BF16), pltpu.VMEM((D_FF, D_MODEL), BF16)],
        ),
        out_shape=jax.ShapeDtypeStruct((n_rows, D_MODEL), F32),
        compiler_params=_params(("arbitrary",)),
        name="moe_experts",
    )(block_e, n_active, xs, w1, b1r, w2, b2r)


def _combine_kernel(dest_ref, x_ref, gate_ref, g_ref, b_ref, yb_hbm, *rest, first_tiles):
    outs, (buf, sem) = rest[:-2], rest[-2:]
    tm = x_ref.shape[0]

    def issue(r, c):
        for k in range(TOP_K):
            _row_copy(yb_hbm, dest_ref[r * TOP_K + k], buf.at[k], r, sem).start(priority=k % 2)
        return c

    lax.fori_loop(0, tm, issue, 0)
    for k in range(TOP_K):
        pltpu.make_async_copy(buf.at[k], buf.at[k], sem).wait()
    gate = gate_ref[...]
    y = gate[:, 0:1] * buf[0]
    for k in range(1, TOP_K):
        y = y + gate[:, k:k + 1] * buf[k]
    res = _layer_norm(DN_ALPHA * x_ref[...] + y, g_ref[...], b_ref[...])
    if first_tiles is None:
        outs[0][...] = res
    else:
        i = pl.program_id(0)

        @pl.when(i < first_tiles)
        def _():
            outs[0][...] = res

        @pl.when(i >= first_tiles)
        def _():
            outs[1][...] = res


def _combine(x, yb, dest_flat, gate, g, b, split=None):
    n = x.shape[0]
    tm = _token_tile(n, split)
    g2, b2 = g.reshape(1, D_MODEL), b.reshape(1, D_MODEL)
    if split is None:
        first_tiles = None
        out_specs = _row_spec(tm, D_MODEL)
        out_shape = jax.ShapeDtypeStruct((n, D_MODEL), F32)
    else:
        first_tiles = split // tm
        out_specs = [pl.BlockSpec((tm, D_MODEL), lambda i: (jnp.minimum(i, first_tiles - 1), 0)),
                     pl.BlockSpec((tm, D_MODEL), lambda i: (jnp.maximum(i - first_tiles, 0), 0))]
        out_shape = [jax.ShapeDtypeStruct((split, D_MODEL), F32), jax.ShapeDtypeStruct((n - split, D_MODEL), F32)]
    return pl.pallas_call(
        functools.partial(_combine_kernel, first_tiles=first_tiles),
        grid=(n // tm,),
        in_specs=[pl.BlockSpec((tm * TOP_K,), lambda i: (i,), memory_space=pltpu.SMEM),
                  _row_spec(tm, D_MODEL), _row_spec(tm, LANES),
                  _full_spec(g2), _full_spec(b2), pl.BlockSpec(memory_space=pl.ANY)],
        out_specs=out_specs,
        out_shape=out_shape,
        scratch_shapes=[pltpu.VMEM((TOP_K, tm, D_MODEL), F32), pltpu.SemaphoreType.DMA(())],
        compiler_params=_params(("arbitrary",)),
        name="moe_combine",
    )(dest_flat, x, gate, g2, b2, yb)


def _moe_layer(x, layer, w_router, b_router, w1, b1, w2, b2, g, b, split=None):
    n = x.shape[0]
    idx, gate, rank, counts = _router(x, w_router, b_router)
    nblk = (n * TOP_K) // MOE_ROWS + N_EXPERTS
    cnt = counts[0].astype(I32)
    padded = (cnt + MOE_ROWS - 1) // MOE_ROWS * MOE_ROWS
    pad_end = jnp.cumsum(padded)
    pad_start = pad_end - padded
    dest = (pad_start[idx] + rank)[:, :TOP_K].reshape(-1)
    n_active = (pad_end[-1] // MOE_ROWS).astype(I32)
    blk = jnp.arange(nblk, dtype=I32)
    blk = jnp.minimum(blk, n_active - 1)
    block_e = jnp.sum((pad_end[None, :] <= (blk * MOE_ROWS)[:, None]).astype(I32), axis=1)
    block_e = jnp.minimum(block_e, N_EXPERTS - 1) + layer * N_EXPERTS
    xs = _dispatch(x, dest, nblk * MOE_ROWS)
    yb = _experts(xs, block_e, n_active.reshape(1), w1, b1, w2, b2)
    return _combine(x, yb, dest, gate, g, b, split)


POOL_HALO = 16


def _pool_kernel(x_ref, halo_ref, w_ref, sc_ref, g_ref, b_ref, o_ref, ext_scr, *, tiles_per_seq, n_prev):
    tm = x_ref.shape[0]
    tile = pl.program_id(0) % tiles_per_seq
    x = x_ref[...]
    halo = halo_ref[...]
    if n_prev == 0:
        halo = jnp.where(tile == 0, 0.0, halo)
    ext_scr[0:POOL_HALO] = halo
    ext_scr[POOL_HALO:POOL_HALO + tm] = x
    pos = n_prev + tile * tm + lax.broadcasted_iota(I32, (tm, 1), 0)
    parts = []
    for g, w in enumerate(POOL_WINDOWS):
        cols = slice(g * POOL_GROUP, (g + 1) * POOL_GROUP)
        xg = x[:, cols]
        s = xg
        for d in range(1, w):
            s = s + ext_scr[POOL_HALO - d:POOL_HALO - d + tm, cols]
        cnt = jnp.minimum(pos + 1, w).astype(F32)
        dg = s / cnt - xg
        parts.append(jnp.dot(dg.astype(BF16), w_ref[g], preferred_element_type=F32))
    mix = jnp.concatenate(parts, axis=1) * sc_ref[...]
    o_ref[...] = _layer_norm(DN_ALPHA * x + mix, g_ref[...], b_ref[...])


def _pool(x, halo_src, pool_w, pool_scale, g, b, tm, tiles_per_seq, n_prev, halo_map, n, row_off=0):
    w16 = pool_w.astype(BF16)
    sc = pool_scale.reshape(1, D_MODEL)
    g2, b2 = g.reshape(1, D_MODEL), b.reshape(1, D_MODEL)
    blk_off = row_off // tm
    return pl.pallas_call(
        functools.partial(_pool_kernel, tiles_per_seq=tiles_per_seq, n_prev=n_prev),
        grid=(n // tm,),
        in_specs=[pl.BlockSpec((tm, D_MODEL), lambda i: (i + blk_off, 0)),
                  pl.BlockSpec((POOL_HALO, D_MODEL), halo_map),
                  _full_spec(w16), _full_spec(sc), _full_spec(g2), _full_spec(b2)],
        out_specs=_row_spec(tm, D_MODEL),
        out_shape=jax.ShapeDtypeStruct((n, D_MODEL), F32),
        scratch_shapes=[pltpu.VMEM((POOL_HALO + tm, D_MODEL), F32)],
        compiler_params=_params(("parallel",)),
        name="pool",
    )(x, halo_src, w16, sc, g2, b2)


PROJ_TILE = 512
DSA_Q_BLOCK = 512
DSA_K_BLOCK = 512
POOL_TILE = 512


def kernel(x_prompt, x_sample, cache_k, cache_v, cache_idx_k, page_table, state_gla, state_pool,
           w_in, gla_fg_w2, gla_fg_b, gla_norm_g, idx_kn_g, idx_kn_b, w_out,
           pool_w, pool_scale, ln_mix_g, ln_mix_b, ln_ffn_g, ln_ffn_b,
           moe_router_w, moe_router_b, moe_w1, moe_b1, moe_w2, moe_b2):
    bp, sp, _ = x_prompt.shape
    bs, ts, _ = x_sample.shape
    n_p, n_s = bp * sp, bs * ts
    xp = x_prompt.reshape(n_p, D_MODEL)
    xs = x_sample.reshape(n_s, D_MODEL)
    proj_w = (w_in, gla_fg_w2, gla_fg_b, idx_kn_g, idx_kn_b)

    (qa, ka, va, ga, lf, qb, _, _, qi, _, wi, kb16, vt16, ki16,
     kt, vt, kit) = _project(xp, *proj_w, tm=PROJ_TILE, seq=sp)
    oa_p, gla_p = _gla(qa, ka, va, lf, ga, jnp.zeros((bp, GLA_HEADS, GLA_DK, GLA_DV), F32), gla_norm_g, bp, sp)
    ob_p = _dsa_prompt(qb, qi, wi, ki16, kb16, vt16, bp, sp, DSA_Q_BLOCK, DSA_K_BLOCK)
    x1p = _merge(xp, oa_p, ob_p, w_out, ln_mix_g[0], ln_mix_b[0], PROJ_TILE)
    k_p = jnp.transpose(kt.reshape(bp, DSA_HEADS, DSA_DH, sp), (0, 3, 1, 2))
    v_p = jnp.transpose(vt.reshape(bp, DSA_HEADS, DSA_DH, sp), (0, 3, 1, 2))
    kidx_p = jnp.transpose(kit, (0, 2, 1))

    qa, ka, va, ga, lf, qb, kb, vb, qi, ki, wi, _, _, _ = _project(xs, *proj_w, tm=n_s)
    oa_s, gla_s = _gla(qa, ka, va, lf, ga, state_gla, gla_norm_g, bs, ts)
    ob_s = _dsa_sample(qb, qi, wi, ki, kb, vb, cache_k, cache_v, cache_idx_k, page_table, bs, ts)
    x1s = _merge(xs, oa_s, ob_s, w_out, ln_mix_g[0], ln_mix_b[0], n_s)
    k_s = kb.reshape(bs, ts, DSA_HEADS, DSA_DH)
    v_s = vb.reshape(bs, ts, DSA_HEADS, DSA_DH)
    kidx_s = ki.reshape(bs, ts, IDX_DIM)

    x_all = _moe_layer(jnp.concatenate([x1p, x1s], axis=0), 0, moe_router_w[0], moe_router_b[0],
                       moe_w1, moe_b1, moe_w2, moe_b2, ln_ffn_g[0], ln_ffn_b[0])

    pool_p = jnp.stack([x_all[(s + 1) * sp - POOL_STATE:(s + 1) * sp] for s in range(bp)])
    xs3 = x_all[n_p:].reshape(bs, ts, D_MODEL)
    pool_s = jnp.concatenate([state_pool, xs3], axis=1)[:, -POOL_STATE:]
    per_seq = sp // POOL_TILE
    halo_step = POOL_TILE // POOL_HALO
    x2p = _pool(x_all, x_all, pool_w, pool_scale, ln_mix_g[1], ln_mix_b[1], POOL_TILE, per_seq, 0,
                lambda i: (jnp.maximum(i * halo_step - 1, 0), 0), n_p)
    halo_s = jnp.concatenate([jnp.zeros((bs, POOL_HALO - POOL_STATE, D_MODEL), F32), state_pool], axis=1)
    x2s = _pool(x_all, halo_s.reshape(bs * POOL_HALO, D_MODEL), pool_w, pool_scale, ln_mix_g[1], ln_mix_b[1],
                ts, 1, POOL_STATE, lambda i: (i, 0), n_s, row_off=n_p)
    y_p, y_s = _moe_layer(jnp.concatenate([x2p, x2s], axis=0), 1, moe_router_w[1], moe_router_b[1],
                          moe_w1, moe_b1, moe_w2, moe_b2, ln_ffn_g[1], ln_ffn_b[1], split=n_p)
    y_p = y_p.reshape(bp, sp, D_MODEL)
    y_s = y_s.reshape(bs, ts, D_MODEL)
    return (y_p, y_s, k_p, v_p, kidx_p, gla_p, pool_p, k_s, v_s, kidx_s, gla_s, pool_s)
```
